```python
import math
import jax
import jax.numpy as jnp
from jax import lax
import numpy as np

D_MODEL = 1024
BATCH = 16
SEQ = 2048
DEPTH = 4

GRID_W = 64
CTX_LEN = 256
N_EVEN = (DEPTH + 1) // 2
N_ODD = DEPTH // 2
EPS = 1e-6
NEG_INF = -1e30

SSD_HEADS = 16
SSD_HEAD_DIM = 64
SSD_INNER = SSD_HEADS * SSD_HEAD_DIM
SSD_GROUPS = 4
SSD_HPG = SSD_HEADS // SSD_GROUPS
SSD_STATE = 128
SSD_CHUNK = 128
SSD_CONV = 4
SSD_XBC = SSD_INNER + 2 * SSD_GROUPS * SSD_STATE

LRU_WIDTH = 1024
LRU_BLOCKS = 16
LRU_BLOCK_DIM = LRU_WIDTH // LRU_BLOCKS
LRU_CONV = 4
LRU_C = 8.0

STATE_COLS = SSD_XBC + 2 * SSD_HEADS + LRU_WIDTH
EVEN_IN = STATE_COLS + SSD_INNER + LRU_WIDTH
EVEN_MIX = SSD_INNER + LRU_WIDTH

NA_HEADS = 16
NA_HEAD_DIM = 64
NA_DIM = NA_HEADS * NA_HEAD_DIM
NA_KH = 8
NA_KW = 16

PEER_HEADS = 8
PEER_KEYS = 128
PEER_EXPERTS = PEER_KEYS * PEER_KEYS
PEER_QDIM = 256
PEER_TOPK = 16
PEER_BLOCK = 128

kernel_name = "hybrid_ssd_rglru_natten_peer_prefix_trunk"


def rms_norm(x, g):
    xf = x.astype(jnp.float32)
    y = xf * lax.rsqrt(jnp.mean(xf * xf, axis=-1, keepdims=True) + EPS)
    return (y * g.astype(jnp.float32)).astype(x.dtype)


def modulate(x, shift, scale):
    return x * (1 + scale) + shift


def flip_seq(t, rev):
    return jnp.flip(t, axis=1) if rev else t


def dw_conv(x, w, b):
    k, ch = w.shape
    y = lax.conv_general_dilated(x, w[:, None, :].astype(x.dtype), window_strides=(1,),
                                 padding=[(k // 2, k - 1 - k // 2)],
                                 dimension_numbers=("NWC", "WIO", "NWC"),
                                 feature_group_count=ch)
    return y + b


def segsum_exp(la):
    q = la.shape[-1]
    cs = jnp.cumsum(la, axis=-1)
    diff = cs[..., :, None] - cs[..., None, :]
    mask = jnp.tril(jnp.ones((q, q), dtype=bool))
    return jnp.where(mask, jnp.exp(jnp.where(mask, diff, 0.0)), 0.0)


def ssd_scan(xs, dt, a, bm, cm, h0, want_y):
    f32 = jnp.float32
    bsz, seq, g, r, p = xs.shape
    n = bm.shape[-1]
    nc, q = seq // SSD_CHUNK, SSD_CHUNK
    xd = (xs.astype(f32) * dt[..., None]).reshape(bsz, nc, q, g, r, p)
    la = jnp.moveaxis((dt * a).reshape(bsz, nc, q, g, r), 2, -1)
    bq = bm.astype(f32).reshape(bsz, nc, q, g, n)
    cs = jnp.cumsum(la, axis=-1)
    states = jnp.einsum("bcsgn,bcgrs,bcsgrp->bcgrpn", bq, jnp.exp(cs[..., -1:] - cs), xd)

    def step(h, inp):
        decay, s = inp
        return h * decay[..., None, None] + s, (h if want_y else None)

    final, h_in = lax.scan(step, h0, (jnp.moveaxis(jnp.exp(cs[..., -1]), 1, 0),
                                      jnp.moveaxis(states, 1, 0)))
    if not want_y:
        return None, final
    cq = cm.astype(f32).reshape(bsz, nc, q, g, n)
    cb = jnp.einsum("bclgn,bcsgn->bcgls", cq, bq)
    y_diag = jnp.einsum("bcgls,bcgrls,bcsgrp->bclgrp", cb, segsum_exp(la), xd)
    y_off = jnp.einsum("bclgn,bcgrpn,bcgrl->bclgrp", cq, jnp.moveaxis(h_in, 0, 1), jnp.exp(cs))
    return (y_diag + y_off).reshape(bsz, seq, g, r, p), final


def ssd_direction(xbc, dt_raw, a_log, dt_bias, h0, want_y):
    bsz, seq, _ = xbc.shape
    xs = xbc[..., :SSD_INNER].reshape(bsz, seq, SSD_GROUPS, SSD_HPG, SSD_HEAD_DIM)
    bm = xbc[..., SSD_INNER:SSD_INNER + SSD_GROUPS * SSD_STATE].reshape(bsz, seq, SSD_GROUPS, SSD_STATE)
    cm = xbc[..., SSD_INNER + SSD_GROUPS * SSD_STATE:].reshape(bsz, seq, SSD_GROUPS, SSD_STATE)
    dt = jax.nn.softplus(dt_raw.astype(jnp.float32) + dt_bias.astype(jnp.float32))
    dt = dt.reshape(bsz, seq, SSD_GROUPS, SSD_HPG)
    a = -jnp.exp(a_log.astype(jnp.float32)).reshape(SSD_GROUPS, SSD_HPG)
    return ssd_scan(xs, dt, a, bm, cm, h0, want_y)


def rglru_scan(xr, wa, ba, wx, bx, lam, h0):
    f32 = jnp.float32
    bsz, seq, w = xr.shape
    xb = xr.reshape(bsz, seq, LRU_BLOCKS, LRU_BLOCK_DIM)
    r = jax.nn.sigmoid(jnp.einsum("blni,nij->blnj", xb, wa.astype(f32)).reshape(bsz, seq, w) + ba.astype(f32))
    i = jax.nn.sigmoid(jnp.einsum("blni,nij->blnj", xb, wx.astype(f32)).reshape(bsz, seq, w) + bx.astype(f32))
    log_a = -LRU_C * r * jax.nn.softplus(-lam.astype(f32))
    a = jnp.exp(log_a)
    b = jnp.sqrt(-jnp.expm1(2.0 * log_a)) * (i * xr)
    b = b.at[:, 0].add(a[:, 0] * h0)
    _, h = lax.associative_scan(lambda lft, rgt: (lft[0] * rgt[0], rgt[0] * lft[1] + rgt[1]), (a, b), axis=1)
    return h, h[:, -1]


def even_stream(xbc, dt_raw, xl, z, gate, init, a_log, dt_bias, d_skip, ssd_g,
                wa, ba, wx, bx, lam, want_y):
    f32 = jnp.float32
    bsz, seq, _ = xbc.shape
    xl32 = xl.astype(f32)
    y_ssd, y_lru = 0.0, 0.0
    fin_ssd, fin_lru = [], []
    for d in range(2):
        rev = d == 1
        ys, s_fin = ssd_direction(flip_seq(xbc, rev),
                                  flip_seq(dt_raw[..., d * SSD_HEADS:(d + 1) * SSD_HEADS], rev),
                                  a_log[d], dt_bias[d], init[0][d], want_y)
        hl, l_fin = rglru_scan(flip_seq(xl32, rev), wa[d], ba[d], wx[d], bx[d], lam[d], init[1][d])
        fin_ssd.append(s_fin)
        fin_lru.append(l_fin)
        if want_y:
            y_ssd = y_ssd + flip_seq(ys, rev)
            y_lru = y_lru + flip_seq(hl, rev)
    finals = (fin_ssd, fin_lru)
    if not want_y:
        return None, finals
    xs = xbc[..., :SSD_INNER].reshape(bsz, seq, SSD_GROUPS, SSD_HPG, SSD_HEAD_DIM).astype(f32)
    y_ssd = (y_ssd + d_skip.astype(f32).reshape(SSD_GROUPS, SSD_HPG, 1) * xs).reshape(bsz, seq, SSD_INNER)
    gated = (y_ssd * jax.nn.silu(z.astype(f32))).reshape(bsz, seq, SSD_GROUPS, SSD_INNER // SSD_GROUPS)
    y_ssd = rms_norm(gated, ssd_g.reshape(SSD_GROUPS, -1)).reshape(bsz, seq, SSD_INNER)
    y_lru = y_lru * jax.nn.gelu(gate.astype(f32))
    return jnp.concatenate([y_ssd, y_lru], axis=-1).astype(xbc.dtype), finals


def even_mixer(hc, hx, w_in, conv_w, conv_b, a_log, dt_bias, d_skip, ssd_g,
               lconv_w, lconv_b, wa, ba, wx, bx, lam, w_out, want_ctx):
    bsz = hx.shape[0]

    def prep(h, cols):
        proj = h @ w_in[:, :cols]
        xbc = jax.nn.silu(dw_conv(proj[..., :SSD_XBC], conv_w, conv_b))
        dt_raw = proj[..., SSD_XBC:SSD_XBC + 2 * SSD_HEADS]
        xl = dw_conv(proj[..., SSD_XBC + 2 * SSD_HEADS:STATE_COLS], lconv_w, lconv_b)
        z = proj[..., STATE_COLS:STATE_COLS + SSD_INNER] if cols == EVEN_IN else None
        gate = proj[..., STATE_COLS + SSD_INNER:] if cols == EVEN_IN else None
        return xbc, dt_raw, xl, z, gate

    params = (a_log, dt_bias, d_skip, ssd_g, wa, ba, wx, bx, lam)
    z_ssd = jnp.zeros((bsz, SSD_GROUPS, SSD_HPG, SSD_HEAD_DIM, SSD_STATE), jnp.float32)
    z_lru = jnp.zeros((bsz, LRU_WIDTH), jnp.float32)
    mix_c, ctx_states = even_stream(*prep(hc, EVEN_IN if want_ctx else STATE_COLS),
                                    ([z_ssd, z_ssd], [z_lru, z_lru]), *params, want_ctx)
    mix_x, _ = even_stream(*prep(hx, EVEN_IN), ctx_states, *params, True)
    y_ctx = mix_c @ w_out if want_ctx else None
    return y_ctx, mix_x @ w_out


def na_mixer(hc, hx, w_qkv, q_g, k_g, rpb, w_o, want_ctx):
    f32 = jnp.float32
    bsz, seq, _ = hx.shape
    nh, hd = NA_HEADS, NA_HEAD_DIM
    scale = hd ** -0.5
    kv_c = (hc @ w_qkv[:, NA_DIM:]).reshape(bsz, -1, 2, nh, hd)
    kc = rms_norm(kv_c[:, :, 0], k_g)
    vc = kv_c[:, :, 1]
    qkv = (hx @ w_qkv).reshape(bsz, seq, 3, nh, hd)
    rows = seq // GRID_W
    kh = min(NA_KH, rows)
    qg = rms_norm(qkv[:, :, 0], q_g).reshape(bsz, rows, GRID_W, nh, hd)
    kg = rms_norm(qkv[:, :, 1], k_g).reshape(bsz, rows, GRID_W, nh, hd)
    vg = qkv[:, :, 2].reshape(bsz, rows, GRID_W, nh, hd)
    row_start = jnp.clip(jnp.arange(rows) - kh // 2, 0, rows - kh)
    col = jnp.arange(GRID_W)
    col_start = jnp.clip(col - NA_KW // 2, 0, GRID_W - NA_KW)
    col_mask = (col[None, :] >= col_start[:, None]) & (col[None, :] < col_start[:, None] + NA_KW)
    dc_idx = jnp.clip(col[None, :] - col[:, None] + NA_KW - 1, 0, 2 * NA_KW - 2)
    n_loc = kh * GRID_W

    def row_block(r):
        rs = row_start[r]
        q = lax.dynamic_index_in_dim(qg, r, axis=1, keepdims=False)
        ks = lax.dynamic_slice_in_dim(kg, rs, kh, axis=1)
        vs = lax.dynamic_slice_in_dim(vg, rs, kh, axis=1)
        dr_idx = rs + jnp.arange(kh) - r + NA_KH - 1
        bias = jnp.transpose(rpb[:, dr_idx][:, :, dc_idx], (0, 2, 1, 3)).astype(f32)
        s_loc = jnp.einsum("bqhd,bikhd->bhqik", q, ks).astype(f32) * scale + bias
        s_loc = jnp.where(col_mask[:, None, :], s_loc, NEG_INF)
        s_ctx = jnp.einsum("bqhd,bchd->bhqc", q, kc).astype(f32) * scale
        p = jax.nn.softmax(jnp.concatenate([s_loc.reshape(bsz, nh, GRID_W, n_loc), s_ctx], axis=-1), axis=-1)
        p = p.astype(vs.dtype)
        out = jnp.einsum("bhqik,bikhd->bqhd", p[..., :n_loc].reshape(bsz, nh, GRID_W, kh, GRID_W), vs)
        return out + jnp.einsum("bhqc,bchd->bqhd", p[..., n_loc:], vc)

    o = lax.map(row_block, jnp.arange(rows))
    y_lat = jnp.moveaxis(o, 0, 1).reshape(bsz, seq, NA_DIM) @ w_o
    y_ctx = None
    if want_ctx:
        qc = rms_norm((hc @ w_qkv[:, :NA_DIM]).reshape(bsz, -1, nh, hd), q_g)
        pc = jax.nn.softmax(jnp.einsum("bqhd,bkhd->bhqk", qc, kc).astype(f32) * scale, axis=-1)
        y_ctx = jnp.einsum("bhqk,bkhd->bqhd", pc.astype(vc.dtype), vc).reshape(bsz, -1, NA_DIM) @ w_o
    return y_ctx, y_lat


def peer(tokens, w_q, sub_keys, u, v):
    t_all, d = tokens.shape
    k = PEER_TOPK

    def block(xb):
        q = (xb @ w_q).reshape(-1, PEER_HEADS, 2, PEER_QDIM // 2)
        s = jnp.einsum("thzd,hzkd->thzk", q, sub_keys).astype(jnp.float32)
        s_top, i_top = lax.top_k(s, k)
        cand = (s_top[:, :, 0, :, None] + s_top[:, :, 1, None, :]).reshape(-1, PEER_HEADS, k * k)
        cand_idx = (i_top[:, :, 0, :, None] * PEER_KEYS + i_top[:, :, 1, None, :]).reshape(-1, PEER_HEADS, k * k)
        best, pos = lax.top_k(cand, k)
        expert = jnp.take_along_axis(cand_idx, pos, axis=-1)
        gate = jax.nn.softmax(best, axis=-1)
        act = jax.nn.gelu(jnp.einsum("td,thkd->thk", xb, u[expert]).astype(jnp.float32))
        return jnp.einsum("thk,thkd->td", (gate * act).astype(v.dtype), v[expert])

    return lax.map(block, tokens.reshape(t_all // PEER_BLOCK, PEER_BLOCK, d)).reshape(t_all, d)


def setup_inputs(seed: int = 0) -> dict:
    key = jax.random.key(seed)
    ks = iter(jax.random.split(key, 48))
    f32 = jnp.float32
    d = D_MODEL

    def nrm(shape, scale):
        return jax.random.normal(next(ks), shape, f32) * scale

    def unif(shape, lo, hi):
        return jax.random.uniform(next(ks), shape, f32, minval=lo, maxval=hi)

    dt0 = jnp.exp(unif((N_EVEN, 2, SSD_HEADS), math.log(1e-3), math.log(1e-1)))
    a_root = unif((N_EVEN, 2, LRU_WIDTH), 0.9, 0.999) ** (1.0 / LRU_C)
    return {
        "x": nrm((BATCH, SEQ, d), 1.0),
        "c": nrm((BATCH, d), 1.0),
        "ctx": nrm((BATCH, CTX_LEN, d), 1.0),
        "c_ctx": nrm((d,), 1.0),
        "ada_w": nrm((DEPTH, d, 6 * d), 0.5 * d ** -0.5),
        "ada_b": nrm((DEPTH, 6 * d), 0.02),
        "norm1_g": 1.0 + nrm((DEPTH, d), 0.05),
        "norm2_g": 1.0 + nrm((DEPTH, d), 0.05),
        "ev_w_in": nrm((N_EVEN, d, EVEN_IN), d ** -0.5),
        "ev_conv_w": nrm((N_EVEN, SSD_CONV, SSD_XBC), SSD_CONV ** -0.5),
        "ev_conv_b": nrm((N_EVEN, SSD_XBC), 0.02),
        "ev_a_log": jnp.log(unif((N_EVEN, 2, SSD_HEADS), 1.0, 16.0)),
        "ev_dt_bias": dt0 + jnp.log(-jnp.expm1(-dt0)),
        "ev_d": 1.0 + nrm((N_EVEN, SSD_HEADS), 0.1),
        "ev_ssd_norm_g": 1.0 + nrm((N_EVEN, SSD_INNER), 0.05),
        "ev_lru_conv_w": nrm((N_EVEN, LRU_CONV, LRU_WIDTH), LRU_CONV ** -0.5),
        "ev_lru_conv_b": nrm((N_EVEN, LRU_WIDTH), 0.02),
        "ev_lru_wa": nrm((N_EVEN, 2, LRU_BLOCKS, LRU_BLOCK_DIM, LRU_BLOCK_DIM), LRU_BLOCK_DIM ** -0.5),
        "ev_lru_ba": nrm((N_EVEN, 2, LRU_WIDTH), 0.02),
        "ev_lru_wx": nrm((N_EVEN, 2, LRU_BLOCKS, LRU_BLOCK_DIM, LRU_BLOCK_DIM), LRU_BLOCK_DIM ** -0.5),
        "ev_lru_bx": nrm((N_EVEN, 2, LRU_WIDTH), 0.02),
        "ev_lru_lam": jnp.log(a_root) - jnp.log1p(-a_root),
        "ev_w_out": nrm((N_EVEN, EVEN_MIX, d), EVEN_MIX ** -0.5),
        "od_w_qkv": nrm((N_ODD, d, 3 * NA_DIM), d ** -0.5),
        "od_q_norm_g": 1.0 + nrm((N_ODD, NA_HEAD_DIM), 0.05),
        "od_k_norm_g": 1.0 + nrm((N_ODD, NA_HEAD_DIM), 0.05),
        "od_rpb": nrm((N_ODD, NA_HEADS, 2 * NA_KH - 1, 2 * NA_KW - 1), 0.1),
        "od_w_o": nrm((N_ODD, NA_DIM, d), NA_DIM ** -0.5),
        "pe_w_q": nrm((DEPTH, d, PEER_HEADS * PEER_QDIM), d ** -0.5),
        "pe_keys": nrm((DEPTH, PEER_HEADS, 2, PEER_KEYS, PEER_QDIM // 2), (PEER_QDIM // 2) ** -0.5),
        "pe_u": nrm((DEPTH, PEER_EXPERTS, d), d ** -0.5),
        "pe_v": nrm((DEPTH, PEER_EXPERTS, d), PEER_HEADS ** -0.5),
    }


def reference(x, c, ctx, c_ctx, ada_w, ada_b, norm1_g, norm2_g,
              ev_w_in, ev_conv_w, ev_conv_b, ev_a_log, ev_dt_bias, ev_d, ev_ssd_norm_g,
              ev_lru_conv_w, ev_lru_conv_b, ev_lru_wa, ev_lru_ba, ev_lru_wx, ev_lru_bx, ev_lru_lam,
              ev_w_out, od_w_qkv, od_q_norm_g, od_k_norm_g, od_rpb, od_w_o,
              pe_w_q, pe_keys, pe_u, pe_v):
    bsz, seq, d = x.shape
    silu_c = jax.nn.silu(c)
    silu_cc = jax.nn.silu(c_ctx)
    for layer in range(DEPTH):
        last = layer == DEPTH - 1
        j = layer // 2
        mod_x = (silu_c @ ada_w[layer] + ada_b[layer])[:, None, :]
        sh1, sc1, g1, sh2, sc2, g2 = jnp.split(mod_x, 6, axis=-1)
        n_mod = 2 if last else 6
        mod_c = silu_cc @ ada_w[layer][:, :n_mod * d] + ada_b[layer][:n_mod * d]
        mc = jnp.split(mod_c, n_mod)

        hx = modulate(rms_norm(x, norm1_g[layer]), sh1, sc1)
        hc = modulate(rms_norm(ctx, norm1_g[layer]), mc[0], mc[1])
        if layer % 2 == 0:
            y_ctx, y_lat = even_mixer(hc, hx, ev_w_in[j], ev_conv_w[j], ev_conv_b[j], ev_a_log[j],
                                      ev_dt_bias[j], ev_d[j], ev_ssd_norm_g[j], ev_lru_conv_w[j],
                                      ev_lru_conv_b[j], ev_lru_wa[j], ev_lru_ba[j], ev_lru_wx[j],
                                      ev_lru_bx[j], ev_lru_lam[j], ev_w_out[j], not last)
        else:
            y_ctx, y_lat = na_mixer(hc, hx, od_w_qkv[j], od_q_norm_g[j], od_k_norm_g[j],
                                    od_rpb[j], od_w_o[j], not last)
        x = x + g1 * y_lat

        hx = modulate(rms_norm(x, norm2_g[layer]), sh2, sc2)
        if last:
            x = x + g2 * peer(hx.reshape(-1, d), pe_w_q[layer], pe_keys[layer],
                              pe_u[layer], pe_v[layer]).reshape(bsz, seq, d)
        else:
            ctx = ctx + mc[2] * y_ctx
            hc = modulate(rms_norm(ctx, norm2_g[layer]), mc[3], mc[4])
            tok = jnp.concatenate([hx.reshape(-1, d), hc.reshape(-1, d)], axis=0)
            out = peer(tok, pe_w_q[layer], pe_keys[layer], pe_u[layer], pe_v[layer])
            x = x + g2 * out[:bsz * seq].reshape(bsz, seq, d)
            ctx = ctx + mc[5] * out[bsz * seq:].reshape(ctx.shape)
    return x
```

```python
import functools
import math

import jax
import jax.numpy as jnp
from jax import lax
from jax.experimental import pallas as pl
from jax.experimental.pallas import tpu as pltpu

F32 = jnp.float32
BF16 = jnp.bfloat16
HIGHEST = lax.Precision.HIGHEST

EPS = 1e-6
NEG_INF = -1e30

GRID_W = 64
SSD_HEADS = 16
SSD_HEAD_DIM = 64
SSD_GROUPS = 4
SSD_HPG = SSD_HEADS // SSD_GROUPS
SSD_STATE = 128
SSD_CHUNK = 128
SSD_INNER = SSD_HEADS * SSD_HEAD_DIM
SSD_GW = SSD_INNER // SSD_GROUPS
SSD_XBC = SSD_INNER + 2 * SSD_GROUPS * SSD_STATE
LRU_WIDTH = 1024
LRU_BLOCKS = 16
LRU_BLOCK_DIM = LRU_WIDTH // LRU_BLOCKS
LRU_C = 8.0
LRU_TILE = 256
NA_HEADS = 16
NA_HEAD_DIM = 64
NA_KH = 8
NA_KW = 16
NA_QROWS = 4
NA_KROWS = 12
PEER_HEADS = 8
PEER_KEYS = 128
PEER_TOPK = 16
PEER_ICHUNK = 8

LANES = 128
SUBLANES = 8
VMEM_LIMIT_BYTES = 56 * 1024 * 1024


def _cparams(*sem):
    return pltpu.CompilerParams(dimension_semantics=sem, vmem_limit_bytes=VMEM_LIMIT_BYTES)


def _silu(x):
    return x * (1.0 / (1.0 + jnp.exp(-x)))


def _sigmoid(x):
    return 1.0 / (1.0 + jnp.exp(-x))


def _softplus(x):
    return jnp.maximum(x, 0.0) + jnp.log(1.0 + jnp.exp(-jnp.abs(x)))


def _gelu_tanh(x):
    c = math.sqrt(2.0 / math.pi)
    return 0.5 * x * (1.0 + jnp.tanh(c * (x + 0.044715 * (x * x * x))))


def _rms_mod(x, g, shift, scale):
    ms = jnp.mean(x * x, axis=-1, keepdims=True)
    y = x * lax.rsqrt(ms + EPS) * g
    return y * (1.0 + scale) + shift


def _dot(a, b):
    return jnp.dot(a, b, preferred_element_type=F32)


def _dot_nt(a, b):
    return lax.dot_general(a, b, (((1,), (1,)), ((), ())), preferred_element_type=F32)


def _ada_kernel(c_ref, w_ref, b_ref, o_ref):
    s = _silu(c_ref[...])
    o_ref[0] = jnp.dot(s, w_ref[0], preferred_element_type=F32, precision=HIGHEST) + b_ref[0]


def _ada_mods(c_all, ada_w, ada_b):
    depth, d, n = ada_w.shape
    rows = c_all.shape[0]
    tn = 1536
    return pl.pallas_call(
        _ada_kernel,
        grid=(depth, n // tn),
        in_specs=[pl.BlockSpec((rows, d), lambda l, j: (0, 0)),
                  pl.BlockSpec((1, d, tn), lambda l, j: (l, 0, j)),
                  pl.BlockSpec((1, 1, tn), lambda l, j: (l, 0, j))],
        out_specs=pl.BlockSpec((1, rows, tn), lambda l, j: (l, 0, j)),
        out_shape=jax.ShapeDtypeStruct((depth, rows, n), F32),
        compiler_params=_cparams("arbitrary", "arbitrary"),
        name="ada_mods",
    )(c_all, ada_w, ada_b.reshape(depth, 1, n))


def _head_block_ones(n):
    r = lax.broadcasted_iota(jnp.int32, (n, n), 0) // NA_HEAD_DIM
    c = lax.broadcasted_iota(jnp.int32, (n, n), 1) // NA_HEAD_DIM
    return (r == c).astype(F32)


def _nm_linear_kernel(*refs, n_out, head_norm, tn):
    x_ref, g_ref, mod_ref = refs[:3]
    w_refs = refs[3:3 + n_out]
    hg_refs = refs[3 + n_out:3 + n_out + sum(head_norm)]
    o_refs = refs[3 + n_out + sum(head_norm):]
    h = _rms_mod(x_ref[0], g_ref[...], mod_ref[0, 0:1, :], mod_ref[0, 1:2, :]).astype(BF16)
    hg_i = 0
    for w_ref, o_ref, hn in zip(w_refs, o_refs, head_norm):
        n = w_ref.shape[1]
        for j in range(n // tn):
            y = _dot(h, w_ref[:, j * tn:(j + 1) * tn])
            if hn:
                ss = jnp.dot(y * y, _head_block_ones(tn), preferred_element_type=F32, precision=HIGHEST)
                y = y * lax.rsqrt(ss * (1.0 / NA_HEAD_DIM) + EPS) * hg_refs[hg_i][:, j * tn:(j + 1) * tn]
            o_ref[0, :, j * tn:(j + 1) * tn] = y.astype(o_ref.dtype)
        hg_i += hn


def _nm_linear(x, g, mod, ws, out_dtypes, head_gains=None, tm=512, tn=256):
    bn, sn, d = x.shape
    tm = min(tm, sn)
    n_out = len(ws)
    head_gains = head_gains or [None] * n_out
    head_norm = tuple(hg is not None for hg in head_gains)
    hgs = [hg for hg in head_gains if hg is not None]
    in_specs = [pl.BlockSpec((1, tm, d), lambda b, i: (b, i, 0)),
                pl.BlockSpec((1, d), lambda b, i: (0, 0)),
                pl.BlockSpec((1, 2, d), lambda b, i: (b, 0, 0))]
    in_specs += [pl.BlockSpec(w.shape, lambda b, i: (0, 0)) for w in ws]
    in_specs += [pl.BlockSpec(hg.shape, lambda b, i: (0, 0)) for hg in hgs]
    out_specs = [pl.BlockSpec((1, tm, w.shape[1]), lambda b, i: (b, i, 0)) for w in ws]
    out_shape = [jax.ShapeDtypeStruct((bn, sn, w.shape[1]), dt) for w, dt in zip(ws, out_dtypes)]
    return pl.pallas_call(
        functools.partial(_nm_linear_kernel, n_out=n_out, head_norm=head_norm, tn=tn),
        grid=(bn, sn // tm),
        in_specs=in_specs, out_specs=out_specs, out_shape=out_shape,
        compiler_params=_cparams("arbitrary", "arbitrary"),
        name="nm_linear",
    )(x, g, mod, *ws, *hgs)


def _out_linear_kernel(*refs, n_in):
    a_refs = refs[:n_in]
    w_refs = refs[n_in:2 * n_in]
    x_ref, mod_ref, g_ref, xo_ref, ho_ref = refs[2 * n_in:]
    y = _dot(a_refs[0][0], w_refs[0][...])
    for a_ref, w_ref in zip(a_refs[1:], w_refs[1:]):
        y = y + _dot(a_ref[0], w_ref[...])
    xn = x_ref[0] + mod_ref[0, 0:1, :] * y
    xo_ref[0] = xn
    ho_ref[0] = _rms_mod(xn, g_ref[...], mod_ref[0, 1:2, :], mod_ref[0, 2:3, :]).astype(BF16)


def _out_linear(acts, ws, x, mod, g, tm=512):
    bn, sn, d = x.shape
    tm = min(tm, sn)
    n_in = len(acts)
    in_specs = [pl.BlockSpec((1, tm, a.shape[2]), lambda b, i: (b, i, 0)) for a in acts]
    in_specs += [pl.BlockSpec(w.shape, lambda b, i: (0, 0)) for w in ws]
    in_specs += [pl.BlockSpec((1, tm, d), lambda b, i: (b, i, 0)),
                 pl.BlockSpec((1, 3, d), lambda b, i: (b, 0, 0)),
                 pl.BlockSpec((1, d), lambda b, i: (0, 0))]
    return pl.pallas_call(
        functools.partial(_out_linear_kernel, n_in=n_in),
        grid=(bn, sn // tm),
        in_specs=in_specs,
        out_specs=[pl.BlockSpec((1, tm, d), lambda b, i: (b, i, 0))] * 2,
        out_shape=[jax.ShapeDtypeStruct((bn, sn, d), F32), jax.ShapeDtypeStruct((bn, sn, d), BF16)],
        compiler_params=_cparams("arbitrary", "arbitrary"),
        name="out_linear",
    )(*acts, *ws, x, mod, g)


CONV_HALO = 16


def _conv_chunk(src_ref, s, seg_len, w, bias, rows=SSD_CHUNK):
    ncol = src_ref.shape[2]
    if s > 0:
        prev = src_ref[0, s - CONV_HALO:s, :].astype(F32)
    else:
        prev = jnp.zeros((CONV_HALO, ncol), F32)
    cur = src_ref[0, s:s + rows, :].astype(F32)
    if s + rows < seg_len:
        nxt = src_ref[0, s + rows:s + rows + CONV_HALO, :].astype(F32)
    else:
        nxt = jnp.zeros((CONV_HALO, ncol), F32)
    win = jnp.concatenate([prev, cur, nxt], axis=0)
    taps = w.shape[0]
    acc = bias
    for k in range(taps):
        off = CONV_HALO - taps // 2 + k
        acc = acc + win[off:off + rows, :] * w[k:k + 1, :]
    return acc


def _lane_head_expand(cols, width):
    nh = len(cols)
    hd = width // nh
    rows = cols[0].shape[0]
    lane_head = lax.broadcasted_iota(jnp.int32, (rows, width), 1) // hd
    out = jnp.broadcast_to(cols[nh - 1], (rows, width))
    for h in range(nh - 2, -1, -1):
        out = jnp.where(lane_head == h, jnp.broadcast_to(cols[h], (rows, width)), out)
    return out


def _ssd_kernel(xs_x, bm_x, cm_x, xs_c, bm_c, cm_c, dt_x, dt_c, z_x, z_c,
                cw_xs, cw_b, cw_c, cb_xs, cb_b, cb_c, alog_ref, dtb_ref, dsk_ref, ng_ref,
                y_x, y_c,
                xs_s, bm_s, cm_s, dt_s, y_s, st_s, *, sx, sc):
    q = SSD_CHUNK
    nc_c, nc_x = sc // q, sx // q
    nc = nc_c + nc_x
    gw = xs_s.shape[1]

    for seg_ref3, seg_len, base in (((xs_c, bm_c, cm_c), sc, 0), ((xs_x, bm_x, cm_x), sx, sc)):
        for ci in range(seg_len // q):
            s = ci * q
            for src, dst, w_ref, b_ref in zip(seg_ref3, (xs_s, bm_s, cm_s), (cw_xs, cw_b, cw_c),
                                              (cb_xs, cb_b, cb_c)):
                dst[base + s:base + s + q, :] = _silu(_conv_chunk(src, s, seg_len, w_ref[...], b_ref[...]))
    dt_s[0:sc, :] = _softplus(dt_c[0] + dtb_ref[0])
    dt_s[sc:sc + sx, :] = _softplus(dt_x[0] + dtb_ref[0])

    a_neg = -jnp.exp(alog_ref[0])
    row = lax.broadcasted_iota(jnp.int32, (q, q), 0)
    col = lax.broadcasted_iota(jnp.int32, (q, q), 1)
    tri = ((col <= row).astype(F32), (col >= row).astype(F32))
    keep = (col <= row, col >= row)
    lane_head = lax.broadcasted_iota(jnp.int32, (q, gw), 1) // SSD_HEAD_DIM

    for d in range(2):
        st_s[...] = jnp.zeros_like(st_s)

        def chunk_body(i, carry, d=d):
            if d == 0:
                ci = i
            else:
                ci = jnp.where(i < nc_c, nc_c - 1 - i, nc + nc_c - 1 - i)
            r0 = pl.multiple_of(ci * q, q)
            xs = xs_s[pl.ds(r0, q), :]
            bm = bm_s[pl.ds(r0, q), :]
            cm = cm_s[pl.ds(r0, q), :]
            dt = dt_s[pl.ds(r0, q), :]
            la = dt * a_neg
            cs = jnp.dot(tri[d], la, preferred_element_type=F32, precision=HIGHEST)
            cs_t = cs.T
            cb = _dot_nt(cm.astype(BF16), bm.astype(BF16))
            heads = [d * SSD_HPG + h for h in range(SSD_HPG)]
            dt_mat = _lane_head_expand([dt[:, c:c + 1] for c in heads], gw)
            cs_mat = _lane_head_expand([cs[:, c:c + 1] for c in heads], gw)
            xd = xs * dt_mat
            xd_b = xd.astype(BF16)
            y = jnp.zeros((q, gw), F32)
            for h, c in enumerate(heads):
                diff = cs[:, c:c + 1] - cs_t[c:c + 1, :]
                lmat = jnp.exp(jnp.where(keep[d], diff, NEG_INF))
                y = jnp.where(lane_head == h, _dot((cb * lmat).astype(BF16), xd_b), y)
            st = st_s[...]
            y = y + _dot(cm.astype(BF16), st.astype(BF16)) * jnp.exp(cs_mat)
            end = q - 1 if d == 0 else 0
            cs_end = cs_mat[end:end + 1, :]
            s_new = _dot(bm.T.astype(BF16), (xd * jnp.exp(cs_end - cs_mat)).astype(BF16))
            st_s[...] = st * jnp.exp(cs_end) + s_new
            if d == 0:
                y_s[pl.ds(r0, q), :] = y
            else:
                y_s[pl.ds(r0, q), :] = y_s[pl.ds(r0, q), :] + y
            return carry

        lax.fori_loop(0, nc, chunk_body, 0)

    for ci in range(nc):
        s = ci * q
        if ci < nc_c:
            z = z_c[0, s:s + q, :]
        else:
            z = z_x[0, s - sc:s - sc + q, :]
        y = (y_s[s:s + q, :] + dsk_ref[0] * xs_s[s:s + q, :]) * _silu(z.astype(F32))
        ms = jnp.mean(y * y, axis=-1, keepdims=True)
        out = (y * lax.rsqrt(ms + EPS) * ng_ref[0]).astype(BF16)
        if ci < nc_c:
            y_c[0, s:s + q, :] = out
        else:
            y_x[0, s - sc:s - sc + q, :] = out


def _ssd_mixer(xbc_x, xbc_c, dt_x, dt_c, z_x, z_c, conv_w, conv_b, alog_g, dtb_g, dsk_g, ng_g):
    bsz, sx, _ = xbc_x.shape
    sc = xbc_c.shape[1]
    g, gw, n = SSD_GROUPS, SSD_GW, SSD_STATE
    nb = SSD_INNER // n
    taps = conv_w.shape[0]

    def seq(s, w, off):
        return pl.BlockSpec((1, s, w), lambda b, j, off=off: (b, 0, off + j))

    def par(r, w, off):
        return pl.BlockSpec((r, w), lambda b, j, off=off: (0, off + j))

    def grp(w):
        return pl.BlockSpec((1, 1, w), lambda b, j: (j, 0, 0))

    in_specs = [seq(sx, gw, 0), seq(sx, n, nb), seq(sx, n, nb + g),
                seq(sc, gw, 0), seq(sc, n, nb), seq(sc, n, nb + g),
                seq(sx, LANES, 0), seq(sc, LANES, 0), seq(sx, gw, 0), seq(sc, gw, 0),
                par(taps, gw, 0), par(taps, n, nb), par(taps, n, nb + g),
                par(1, gw, 0), par(1, n, nb), par(1, n, nb + g),
                grp(LANES), grp(LANES), grp(gw), grp(gw)]
    stot = sx + sc
    return pl.pallas_call(
        functools.partial(_ssd_kernel, sx=sx, sc=sc),
        grid=(bsz, g),
        in_specs=in_specs,
        out_specs=[seq(sx, gw, 0), seq(sc, gw, 0)],
        out_shape=[jax.ShapeDtypeStruct((bsz, sx, SSD_INNER), BF16),
                   jax.ShapeDtypeStruct((bsz, sc, SSD_INNER), BF16)],
        scratch_shapes=[pltpu.VMEM((stot, gw), F32), pltpu.VMEM((stot, n), F32), pltpu.VMEM((stot, n), F32),
                        pltpu.VMEM((stot, LANES), F32), pltpu.VMEM((stot, gw), F32), pltpu.VMEM((n, gw), F32)],
        compiler_params=_cparams("arbitrary", "arbitrary"),
        name="ssd_mixer",
    )(xbc_x, xbc_x, xbc_x, xbc_c, xbc_c, xbc_c, dt_x, dt_c, z_x, z_c,
      conv_w, conv_w, conv_w, conv_b, conv_b, conv_b, alog_g, dtb_g, dsk_g, ng_g)


def _lru_kernel(xl_x, xl_c, gt_x, gt_c, cw, cb, wa, wx, ba, bx, lam, y_x, y_c,
                xr_s, a_s, b_s, y_s, *, sx, sc):
    q = SSD_CHUNK
    stot = sx + sc
    w = xr_s.shape[1]
    for src, seg_len, base in ((xl_c, sc, 0), (xl_x, sx, sc)):
        for ci in range(seg_len // q):
            s = ci * q
            xr_s[base + s:base + s + q, :] = _conv_chunk(src, s, seg_len, cw[...], cb[...])

    ng = stot // SUBLANES
    ng_c = sc // SUBLANES
    sub = lax.broadcasted_iota(jnp.int32, (SUBLANES, w), 0)
    rt = q
    for d in range(2):
        nsp = _softplus(-lam[d:d + 1, :])
        for ci in range(stot // rt):
            s = ci * rt
            xr = xr_s[s:s + rt, :]
            xb = xr.astype(BF16)
            r = _sigmoid(_dot(xb, wa[d, 0]) + ba[d:d + 1, :])
            ig = _sigmoid(_dot(xb, wx[d, 0]) + bx[d:d + 1, :])
            a = jnp.exp(-LRU_C * r * nsp)
            a_s[s:s + rt, :] = a
            b_s[s:s + rt, :] = jnp.sqrt(1.0 - a * a) * (ig * xr)

        def group_body(k, carry, d=d):
            if d == 0:
                gi = k
            else:
                gi = jnp.where(k < ng_c, ng_c - 1 - k, ng + ng_c - 1 - k)
            r0 = pl.multiple_of(gi * SUBLANES, SUBLANES)
            a = a_s[pl.ds(r0, SUBLANES), :]
            b = b_s[pl.ds(r0, SUBLANES), :]
            for sh in (1, 2, 4):
                if d == 0:
                    valid = sub >= sh
                    a_sh = pltpu.roll(a, sh, axis=0)
                    b_sh = pltpu.roll(b, sh, axis=0)
                else:
                    valid = sub < SUBLANES - sh
                    a_sh = pltpu.roll(a, SUBLANES - sh, axis=0)
                    b_sh = pltpu.roll(b, SUBLANES - sh, axis=0)
                b = jnp.where(valid, a * b_sh + b, b)
                a = jnp.where(valid, a * a_sh, a)
            h = a * carry + b
            if d == 0:
                y_s[pl.ds(r0, SUBLANES), :] = h
                last = h[SUBLANES - 1:SUBLANES, :]
            else:
                y_s[pl.ds(r0, SUBLANES), :] = y_s[pl.ds(r0, SUBLANES), :] + h
                last = h[0:1, :]
            return jnp.broadcast_to(last, (SUBLANES, w))

        lax.fori_loop(0, ng, group_body, jnp.zeros((SUBLANES, w), F32))

    for ci in range(stot // rt):
        s = ci * rt
        if s < sc:
            gate = gt_c[0, s:s + rt, :]
        else:
            gate = gt_x[0, s - sc:s - sc + rt, :]
        out = (y_s[s:s + rt, :] * _gelu_tanh(gate.astype(F32))).astype(BF16)
        if s < sc:
            y_c[0, s:s + rt, :] = out
        else:
            y_x[0, s - sc:s - sc + rt, :] = out


def _lru_mixer(xl_x, xl_c, gt_x, gt_c, conv_w, conv_b, wa_bd, wx_bd, ba, bx, lam):
    bsz, sx, width = xl_x.shape
    sc = xl_c.shape[1]
    w = LRU_TILE
    taps = conv_w.shape[0]

    def seq(s):
        return pl.BlockSpec((1, s, w), lambda b, j: (b, 0, j))

    def par(r):
        return pl.BlockSpec((r, w), lambda b, j: (0, j))

    wspec = pl.BlockSpec((2, 1, w, w), lambda b, j: (0, j, 0, 0))
    stot = sx + sc
    return pl.pallas_call(
        functools.partial(_lru_kernel, sx=sx, sc=sc),
        grid=(bsz, width // w),
        in_specs=[seq(sx), seq(sc), seq(sx), seq(sc), par(taps), par(1), wspec, wspec, par(2), par(2), par(2)],
        out_specs=[seq(sx), seq(sc)],
        out_shape=[jax.ShapeDtypeStruct((bsz, sx, width), BF16), jax.ShapeDtypeStruct((bsz, sc, width), BF16)],
        scratch_shapes=[pltpu.VMEM((stot, w), F32)] * 4,
        compiler_params=_cparams("arbitrary", "arbitrary"),
        name="lru_mixer",
    )(xl_x, xl_c, gt_x, gt_c, conv_w, conv_b, wa_bd, wx_bd, ba, bx, lam)


def _rpb_table_kernel(rpb_ref, o_ref, *, n_dr, n_dc):
    h = pl.program_id(0)
    w = GRID_W
    qcol = lax.broadcasted_iota(jnp.int32, (w, 2 * w), 0)
    lane = lax.broadcasted_iota(jnp.int32, (w, 2 * w), 1)
    kcol = lane % w
    hi = lane >= w
    rel = kcol - qcol + (NA_KW - 1)
    cstart = jnp.clip(qcol - NA_KW // 2, 0, w - NA_KW)
    in_win = (kcol >= cstart) & (kcol < cstart + NA_KW)
    for d in range(n_dr + 1):
        acc = jnp.full((w, 2 * w), NEG_INF, F32)
        for dc in range(n_dc):
            lo = rpb_ref[(h * n_dr + d - 1) * n_dc + dc] if d >= 1 else NEG_INF
            up = rpb_ref[(h * n_dr + d) * n_dc + dc] if d < n_dr else NEG_INF
            acc = jnp.where(rel == dc, jnp.where(hi, up, lo), acc)
        valid = in_win
        if d == 0:
            valid = valid & hi
        if d == n_dr:
            valid = valid & jnp.logical_not(hi)
        o_ref[0, d] = jnp.where(valid, acc, NEG_INF)


def _rpb_table(rpb):
    nh, n_dr, n_dc = rpb.shape
    return pl.pallas_call(
        functools.partial(_rpb_table_kernel, n_dr=n_dr, n_dc=n_dc),
        grid=(nh,),
        in_specs=[pl.BlockSpec(memory_space=pltpu.SMEM)],
        out_specs=pl.BlockSpec((1, n_dr + 1, GRID_W, 2 * GRID_W), lambda h: (h, 0, 0, 0)),
        out_shape=jax.ShapeDtypeStruct((nh, n_dr + 1, GRID_W, 2 * GRID_W), F32),
        compiler_params=_cparams("arbitrary"),
        name="rpb_table",
    )(rpb.reshape(-1))


def _na_kernel(*refs, sx, sc, want_ctx):
    if want_ctx:
        q_x, k_x, v_x, q_c, k_c, v_c, tab, o_x, o_c = refs
    else:
        q_x, k_x, v_x, k_c, v_c, tab, o_x = refs
    w = GRID_W
    rows = sx // w
    qb = NA_QROWS * w
    kb = NA_KROWS * w
    n_blk = rows // NA_QROWS
    scale = NA_HEAD_DIM ** -0.5
    lane = lax.broadcasted_iota(jnp.int32, (1, 2 * NA_HEAD_DIM), 1)
    in_head = (lane < NA_HEAD_DIM, lane >= NA_HEAD_DIM)
    keyrow = lax.broadcasted_iota(jnp.int32, (1, kb), 1) // w
    kc = k_c[0]
    vc = v_c[0]
    n_tab = tab.shape[1]

    def softmax_pv(parts):
        m = parts[0][0].max(axis=-1, keepdims=True)
        for s, _ in parts[1:]:
            m = jnp.maximum(m, s.max(axis=-1, keepdims=True))
        acc, den = None, None
        for s, v in parts:
            e = jnp.exp(s - m)
            den = e.sum(axis=-1, keepdims=True) if den is None else den + e.sum(axis=-1, keepdims=True)
            pv = _dot(e.astype(BF16), v)
            acc = pv if acc is None else acc + pv
        return acc / den

    def block_body(rb, carry):
        ws = jnp.clip(NA_QROWS * rb - NA_KH // 2, 0, rows - NA_KROWS)
        q0 = pl.multiple_of(rb * qb, qb)
        k0 = pl.multiple_of(ws * w, w)
        qblk = q_x[0, pl.ds(q0, qb), :]
        kwin = k_x[0, pl.ds(k0, kb), :]
        vwin = v_x[0, pl.ds(k0, kb), :]
        out = jnp.zeros((qb, 2 * NA_HEAD_DIM), F32)
        for hh in range(2):
            qm = jnp.where(in_head[hh], qblk, jnp.zeros_like(qblk))
            s_loc = _dot_nt(qm, kwin) * scale
            s_ctx = _dot_nt(qm, kc) * scale
            pieces = []
            for rq in range(NA_QROWS):
                r = NA_QROWS * rb + rq
                rs = jnp.clip(r - NA_KH // 2, 0, rows - NA_KH)
                lo = rs - ws
                valid = (keyrow >= lo) & (keyrow < lo + NA_KH)
                blocks = []
                for ip in range(NA_KROWS // 2):
                    dr_lo = ws + 2 * ip - r + NA_KH - 1
                    blocks.append(tab[hh, jnp.clip(dr_lo + 1, 0, n_tab - 1)])
                bias = jnp.concatenate(blocks, axis=1)
                piece = s_loc[rq * w:(rq + 1) * w, :] + bias
                pieces.append(jnp.where(valid, piece, NEG_INF))
            s_loc = jnp.concatenate(pieces, axis=0)
            o = softmax_pv([(s_loc, vwin), (s_ctx, vc)])
            out = jnp.where(in_head[hh], o, out)
        o_x[0, pl.ds(q0, qb), :] = out.astype(o_x.dtype)
        return carry

    lax.fori_loop(0, n_blk, block_body, 0)

    if want_ctx:
        qc = q_c[0]
        out = jnp.zeros((sc, 2 * NA_HEAD_DIM), F32)
        for hh in range(2):
            qm = jnp.where(in_head[hh], qc, jnp.zeros_like(qc))
            o = softmax_pv([(_dot_nt(qm, kc) * scale, vc)])
            out = jnp.where(in_head[hh], o, out)
        o_c[0] = out.astype(o_c.dtype)


def _na_attention(q_x, k_x, v_x, q_c, k_c, v_c, table, want_ctx):
    bsz, sx, dim = q_x.shape
    sc = k_c.shape[1]
    pw = 2 * NA_HEAD_DIM
    n_pair = dim // pw

    def seq(s):
        return pl.BlockSpec((1, s, pw), lambda p, b: (b, 0, p))

    tspec = pl.BlockSpec((2,) + table.shape[1:], lambda p, b: (p, 0, 0, 0))
    if want_ctx:
        args = (q_x, k_x, v_x, q_c, k_c, v_c, table)
        in_specs = [seq(sx)] * 3 + [seq(sc)] * 3 + [tspec]
        out_specs = [seq(sx), seq(sc)]
        out_shape = [jax.ShapeDtypeStruct((bsz, sx, dim), BF16), jax.ShapeDtypeStruct((bsz, sc, dim), BF16)]
    else:
        args = (q_x, k_x, v_x, k_c, v_c, table)
        in_specs = [seq(sx)] * 3 + [seq(sc)] * 2 + [tspec]
        out_specs = [seq(sx)]
        out_shape = [jax.ShapeDtypeStruct((bsz, sx, dim), BF16)]
    res = pl.pallas_call(
        functools.partial(_na_kernel, sx=sx, sc=sc, want_ctx=want_ctx),
        grid=(n_pair, bsz),
        in_specs=in_specs, out_specs=out_specs, out_shape=out_shape,
        compiler_params=_cparams("arbitrary", "arbitrary"),
        name="na_attention",
    )(*args)
    return res if want_ctx else (res[0], None)


def _take_top(work, rowid, n_rows):
    m = jnp.max(work, axis=0, keepdims=True)
    idx = jnp.min(jnp.where(work == m, rowid, n_rows), axis=0, keepdims=True)
    return m, rowid == idx


def _peer_route_kernel(h_ref, wq_ref, keys_ref, cnt_ref, rk_ref, e0_ref, e1_ref):
    nk, k = PEER_KEYS, PEER_TOPK
    hx = h_ref[0]
    tt = hx.shape[0]
    key_id = lax.broadcasted_iota(jnp.int32, (nk, tt), 0)
    top_id = lax.broadcasted_iota(jnp.int32, (k, tt), 0)
    n_cand = k + 8 * 8
    cand_id = lax.broadcasted_iota(jnp.int32, (n_cand, tt), 0)
    for h in range(PEER_HEADS):
        q = _dot(hx, wq_ref[:, h * 2 * nk:(h + 1) * 2 * nk]).astype(BF16)
        s = [_dot_nt(keys_ref[2 * h + z], q[:, z * nk:(z + 1) * nk]) for z in range(2)]
        tops, ranks = [], []
        for z in range(2):
            work = s[z]
            top = jnp.zeros((k, tt), F32)
            rank = jnp.full((nk, tt), float(k), F32)
            for it in range(k):
                m, sel = _take_top(work, key_id, nk)
                top = jnp.where(top_id == it, m, top)
                rank = jnp.where(sel, float(it), rank)
                work = jnp.where(sel, -jnp.inf, work)
            tops.append(top)
            ranks.append(rank)
        t0, t1 = tops
        cand = jnp.concatenate([t0[0:1] + t1] + [t0[a:a + 1] + t1[0:8] for a in range(1, 8)]
                               + [t0[8:16] + t1[0:1]], axis=0)
        best = cand[0:1]
        work = cand
        picked = jnp.zeros((n_cand, tt), jnp.bool_)
        for it in range(k):
            _, sel = _take_top(work, cand_id, n_cand)
            picked = picked | sel
            work = jnp.where(sel, -jnp.inf, work)
        pf = picked.astype(F32)
        z_sum = jnp.sum(jnp.where(picked, jnp.exp(cand - best), 0.0), axis=0, keepdims=True)
        cnts = [jnp.sum(pf[0:k], axis=0, keepdims=True)]
        cnts += [jnp.sum(pf[k + 8 * (a - 1):k + 8 * a], axis=0, keepdims=True) for a in range(1, 8)]
        cnts += [pf[k + 56 + a:k + 57 + a] for a in range(8)]
        cnt_i = jnp.zeros((nk, tt), F32)
        for a in range(k):
            cnt_i = jnp.where(ranks[0] == float(a), cnts[a], cnt_i)
        cnt_ref[0, h * nk:(h + 1) * nk, :] = cnt_i
        rk_ref[0, h * nk:(h + 1) * nk, :] = ranks[1]
        e0_ref[0, h * nk:(h + 1) * nk, :] = jnp.exp(s[0] - t0[0:1]) / z_sum
        e1_ref[0, h * nk:(h + 1) * nk, :] = jnp.exp(s[1] - t1[0:1])


def _peer_route(hx, w_q, keys, tt):
    bn, sn, d = hx.shape
    rows = PEER_HEADS * PEER_KEYS
    ospec = pl.BlockSpec((1, rows, tt), lambda b, i: (b, 0, i))
    return pl.pallas_call(
        _peer_route_kernel,
        grid=(bn, sn // tt),
        in_specs=[pl.BlockSpec((1, tt, d), lambda b, i: (b, i, 0)),
                  pl.BlockSpec(w_q.shape, lambda b, i: (0, 0)),
                  pl.BlockSpec(keys.shape, lambda b, i: (0, 0, 0))],
        out_specs=[ospec] * 4,
        out_shape=[jax.ShapeDtypeStruct((bn, rows, sn), F32)] * 4,
        compiler_params=_cparams("arbitrary", "arbitrary"),
        name="peer_route",
    )(hx, w_q, keys)


def _peer_dense_kernel(h_ref, cnt_ref, rk_ref, e0_ref, e1_ref, u_ref, vt_ref, x_ref, g_ref, o_ref, acc_ref):
    nk = PEER_KEYS
    ck = pl.program_id(2)

    @pl.when(ck == 0)
    def _():
        acc_ref[...] = jnp.zeros_like(acc_ref)

    hx = h_ref[0]
    tt = hx.shape[0]
    act = _gelu_tanh(_dot_nt(u_ref[...], hx))
    parts = []
    for il in range(PEER_ICHUNK):
        i = ck * PEER_ICHUNK + il
        wgt = jnp.zeros((nk, tt), F32)
        for h in range(PEER_HEADS):
            cnt_row = cnt_ref[0, pl.ds(h * nk + i, 1), :]
            e0_row = e0_ref[0, pl.ds(h * nk + i, 1), :]
            rk = rk_ref[0, h * nk:(h + 1) * nk, :]
            e1 = e1_ref[0, h * nk:(h + 1) * nk, :]
            wgt = wgt + jnp.where(rk < cnt_row, e1 * e0_row, 0.0)
        parts.append((wgt * act[il * nk:(il + 1) * nk, :]).astype(BF16))
    p = jnp.concatenate(parts, axis=0)
    acc_ref[...] += _dot(vt_ref[...], p)

    @pl.when(ck == pl.num_programs(2) - 1)
    def _():
        o_ref[0] = x_ref[0] + g_ref[0] * acc_ref[...].T


def _peer_dense(hx, route, u, v_t, x, gate, tt):
    bn, sn, d = hx.shape
    rows = PEER_HEADS * PEER_KEYS
    ne = PEER_ICHUNK * PEER_KEYS
    n_chunk = u.shape[0] // ne
    rspec = pl.BlockSpec((1, rows, tt), lambda b, i, c: (b, 0, i))
    tok = pl.BlockSpec((1, tt, d), lambda b, i, c: (b, i, 0))
    return pl.pallas_call(
        _peer_dense_kernel,
        grid=(bn, sn // tt, n_chunk),
        in_specs=[tok, rspec, rspec, rspec, rspec,
                  pl.BlockSpec((ne, d), lambda b, i, c: (c, 0)),
                  pl.BlockSpec((d, ne), lambda b, i, c: (0, c)),
                  tok,
                  pl.BlockSpec((1, 1, d), lambda b, i, c: (b, 0, 0))],
        out_specs=tok,
        out_shape=jax.ShapeDtypeStruct((bn, sn, d), F32),
        scratch_shapes=[pltpu.VMEM((d, tt), F32)],
        compiler_params=_cparams("arbitrary", "arbitrary", "arbitrary"),
        name="peer_dense",
    )(hx, *route, u, v_t, x, gate)


def _peer(hx, x, gate, w_q, keys, u, v_t, tt=512):
    tt = min(tt, hx.shape[1])
    route = _peer_route(hx, w_q, keys, tt)
    return _peer_dense(hx, route, u, v_t, x, gate, tt)


def _group_lanes(p, width):
    g = p.reshape(2, SSD_GROUPS, SSD_HPG).transpose(1, 0, 2).reshape(SSD_GROUPS, 2 * SSD_HPG)
    return jnp.pad(g, ((0, 0), (0, width - 2 * SSD_HPG))).reshape(SSD_GROUPS, 1, width)


def _block_diag(w, tile):
    two, nb, bd, _ = w.shape
    per = tile // bd
    w = w.reshape(two, nb // per, per, bd, bd)
    eye = jnp.eye(per, dtype=w.dtype)
    return jnp.einsum("dtpij,pq->dtpiqj", w, eye).reshape(two, nb // per, tile, tile)


def kernel(x, c, ctx, c_ctx, ada_w, ada_b, norm1_g, norm2_g, ev_w_in, ev_conv_w, ev_conv_b, ev_a_log,
           ev_dt_bias, ev_d, ev_ssd_norm_g, ev_lru_conv_w, ev_lru_conv_b, ev_lru_wa, ev_lru_ba, ev_lru_wx,
           ev_lru_bx, ev_lru_lam, ev_w_out, od_w_qkv, od_q_norm_g, od_k_norm_g, od_rpb, od_w_o,
           pe_w_q, pe_keys, pe_u, pe_v):
    bsz, sx, d = x.shape
    sc = ctx.shape[1]
    depth = ada_w.shape[0]

    n_c = bsz + 1
    rows = -(-n_c // SUBLANES) * SUBLANES
    c_all = jnp.concatenate([c, c_ctx[None], jnp.zeros((rows - n_c, d), F32)], axis=0)
    mods = _ada_mods(c_all, ada_w, ada_b).reshape(depth, rows, 6, d)

    ctx = ctx.reshape(1, bsz * sc, d)

    def per_batch(t):
        return t.reshape(bsz, sc, t.shape[-1])

    for layer in range(depth):
        last = layer == depth - 1
        j = layer // 2
        mod_x = mods[layer, :bsz]
        mod_c = mods[layer, bsz:bsz + 1]
        g1 = norm1_g[layer][None]
        g2 = norm2_g[layer][None]
        want_ctx = not last

        if layer % 2 == 0:
            w_in = ev_w_in[j]
            o_dt, o_xl = SSD_XBC, SSD_XBC + 2 * SSD_HEADS
            o_z = o_xl + LRU_WIDTH
            o_gate = o_z + SSD_INNER
            w_dt = w_in[:, o_dt:o_xl].reshape(d, 2, SSD_GROUPS, SSD_HPG).transpose(0, 2, 1, 3)
            w_dt = jnp.pad(w_dt.reshape(d, SSD_GROUPS, 2 * SSD_HPG), ((0, 0), (0, 0), (0, LANES - 2 * SSD_HPG)))
            ws = [w_in[:, :o_dt].astype(BF16), w_dt.reshape(d, SSD_GROUPS * LANES).astype(BF16),
                  w_in[:, o_xl:o_z].astype(BF16), w_in[:, o_z:o_gate].astype(BF16), w_in[:, o_gate:].astype(BF16)]
            dts = [BF16, F32, BF16, BF16, BF16]
            px = _nm_linear(x, g1, mod_x[:, 0:2], ws, dts)
            pc = [per_batch(t) for t in _nm_linear(ctx, g1, mod_c[:, 0:2], ws, dts)]
            y_ssd_x, y_ssd_c = _ssd_mixer(
                px[0], pc[0], px[1], pc[1], px[3], pc[3], ev_conv_w[j], ev_conv_b[j][None],
                _group_lanes(ev_a_log[j], LANES), _group_lanes(ev_dt_bias[j], LANES),
                jnp.repeat(ev_d[j], SSD_HEAD_DIM).reshape(SSD_GROUPS, 1, SSD_GW),
                ev_ssd_norm_g[j].reshape(SSD_GROUPS, 1, SSD_GW))
            y_lru_x, y_lru_c = _lru_mixer(
                px[2], pc[2], px[4], pc[4], ev_lru_conv_w[j], ev_lru_conv_b[j][None],
                _block_diag(ev_lru_wa[j], LRU_TILE).astype(BF16), _block_diag(ev_lru_wx[j], LRU_TILE).astype(BF16),
                ev_lru_ba[j], ev_lru_bx[j], ev_lru_lam[j])
            w_out = ev_w_out[j].astype(BF16)
            w_outs = [w_out[:SSD_INNER], w_out[SSD_INNER:]]
            acts_x = [y_ssd_x, y_lru_x]
            acts_c = [y_ssd_c.reshape(1, bsz * sc, -1), y_lru_c.reshape(1, bsz * sc, -1)]
        else:
            w_qkv = od_w_qkv[j].astype(BF16)
            nd = w_qkv.shape[1] // 3
            ws = [w_qkv[:, :nd], w_qkv[:, nd:2 * nd], w_qkv[:, 2 * nd:]]
            gains = [jnp.tile(od_q_norm_g[j], NA_HEADS)[None], jnp.tile(od_k_norm_g[j], NA_HEADS)[None], None]
            q_x, k_x, v_x = _nm_linear(x, g1, mod_x[:, 0:2], ws, [BF16] * 3, gains)
            q_c, k_c, v_c = [per_batch(t) for t in _nm_linear(ctx, g1, mod_c[:, 0:2], ws, [BF16] * 3, gains)]
            table = _rpb_table(od_rpb[j])
            o_x, o_c = _na_attention(q_x, k_x, v_x, q_c, k_c, v_c, table, want_ctx)
            w_outs = [od_w_o[j].astype(BF16)]
            acts_x = [o_x]
            acts_c = [o_c.reshape(1, bsz * sc, -1)] if want_ctx else None

        w_q = pe_w_q[layer].astype(BF16)
        keys = pe_keys[layer].reshape(2 * PEER_HEADS, PEER_KEYS, -1).astype(BF16)
        u = pe_u[layer].astype(BF16)
        v_t = pe_v[layer].T.astype(BF16)

        x, hx = _out_linear(acts_x, w_outs, x, mod_x[:, 2:5], g2)
        x = _peer(hx, x, mod_x[:, 5:6], w_q, keys, u, v_t)
        if want_ctx:
            ctx, hc = _out_linear(acts_c, w_outs, ctx, mod_c[:, 2:5], g2)
            ctx = _peer(hc, ctx, mod_c[:, 5:6], w_q, keys, u, v_t)
    return x
```

```python
import functools
import math

import jax
import jax.numpy as jnp
from jax import lax
from jax.experimental import pallas as pl
from jax.experimental.pallas import tpu as pltpu

F32 = jnp.float32
BF16 = jnp.bfloat16
HIGHEST = lax.Precision.HIGHEST

EPS = 1e-6
NEG_INF = -1e30

GRID_W = 64
SSD_HEADS = 16
SSD_HEAD_DIM = 64
SSD_GROUPS = 4
SSD_HPG = SSD_HEADS // SSD_GROUPS
SSD_STATE = 128
SSD_CHUNK = 128
SSD_INNER = SSD_HEADS * SSD_HEAD_DIM
SSD_GW = SSD_INNER // SSD_GROUPS
SSD_XBC = SSD_INNER + 2 * SSD_GROUPS * SSD_STATE
LRU_WIDTH = 1024
LRU_BLOCKS = 16
LRU_BLOCK_DIM = LRU_WIDTH // LRU_BLOCKS
LRU_C = 8.0
LRU_TILE = 256
NA_HEADS = 16
NA_HEAD_DIM = 64
NA_KH = 8
NA_KW = 16
NA_QROWS = 4
NA_KROWS = 12
PEER_HEADS = 8
PEER_KEYS = 128
PEER_TOPK = 16
PEER_ICHUNK = 8

LANES = 128
SUBLANES = 8
VMEM_LIMIT_BYTES = 56 * 1024 * 1024


def _cparams(*sem):
    return pltpu.CompilerParams(dimension_semantics=sem, vmem_limit_bytes=VMEM_LIMIT_BYTES)


def _silu(x):
    return x * (1.0 / (1.0 + jnp.exp(-x)))


def _sigmoid(x):
    return 1.0 / (1.0 + jnp.exp(-x))


def _softplus(x):
    return jnp.maximum(x, 0.0) + jnp.log(1.0 + jnp.exp(-jnp.abs(x)))


def _gelu_tanh(x):
    c = math.sqrt(2.0 / math.pi)
    return 0.5 * x * (1.0 + jnp.tanh(c * (x + 0.044715 * (x * x * x))))


def _rms_mod(x, g, shift, scale):
    ms = jnp.mean(x * x, axis=-1, keepdims=True)
    y = x * lax.rsqrt(ms + EPS) * g
    return y * (1.0 + scale) + shift


def _dot(a, b):
    return jnp.dot(a, b, preferred_element_type=F32)


def _dot_nt(a, b):
    return lax.dot_general(a, b, (((1,), (1,)), ((), ())), preferred_element_type=F32)


def _ada_kernel(c_ref, w_ref, b_ref, o_ref):
    s = _silu(c_ref[...])
    o_ref[0] = jnp.dot(s, w_ref[0], preferred_element_type=F32, precision=HIGHEST) + b_ref[0]


def _ada_mods(c_all, ada_w, ada_b):
    depth, d, n = ada_w.shape
    rows = c_all.shape[0]
    tn = 1536
    return pl.pallas_call(
        _ada_kernel,
        grid=(depth, n // tn),
        in_specs=[pl.BlockSpec((rows, d), lambda l, j: (0, 0)),
                  pl.BlockSpec((1, d, tn), lambda l, j: (l, 0, j)),
                  pl.BlockSpec((1, 1, tn), lambda l, j: (l, 0, j))],
        out_specs=pl.BlockSpec((1, rows, tn), lambda l, j: (l, 0, j)),
        out_shape=jax.ShapeDtypeStruct((depth, rows, n), F32),
        compiler_params=_cparams("arbitrary", "arbitrary"),
        name="ada_mods",
    )(c_all, ada_w, ada_b.reshape(depth, 1, n))


def _head_block_ones(n):
    r = lax.broadcasted_iota(jnp.int32, (n, n), 0) // NA_HEAD_DIM
    c = lax.broadcasted_iota(jnp.int32, (n, n), 1) // NA_HEAD_DIM
    return (r == c).astype(F32)


def _nm_linear_kernel(*refs, n_out, head_norm, tn):
    x_ref, g_ref, mod_ref = refs[:3]
    w_refs = refs[3:3 + n_out]
    hg_refs = refs[3 + n_out:3 + n_out + sum(head_norm)]
    o_refs = refs[3 + n_out + sum(head_norm):]
    h = _rms_mod(x_ref[0], g_ref[...], mod_ref[0, 0:1, :], mod_ref[0, 1:2, :]).astype(BF16)
    hg_i = 0
    for w_ref, o_ref, hn in zip(w_refs, o_refs, head_norm):
        n = w_ref.shape[1]
        for j in range(n // tn):
            y = _dot(h, w_ref[:, j * tn:(j + 1) * tn])
            if hn:
                ss = jnp.dot(y * y, _head_block_ones(tn), preferred_element_type=F32, precision=HIGHEST)
                y = y * lax.rsqrt(ss * (1.0 / NA_HEAD_DIM) + EPS) * hg_refs[hg_i][:, j * tn:(j + 1) * tn]
            o_ref[0, :, j * tn:(j + 1) * tn] = y.astype(o_ref.dtype)
        hg_i += hn


def _nm_linear(x, g, mod, ws, out_dtypes, head_gains=None, tm=512, tn=256):
    bn, sn, d = x.shape
    tm = min(tm, sn)
    n_out = len(ws)
    head_gains = head_gains or [None] * n_out
    head_norm = tuple(hg is not None for hg in head_gains)
    hgs = [hg for hg in head_gains if hg is not None]
    in_specs = [pl.BlockSpec((1, tm, d), lambda b, i: (b, i, 0)),
                pl.BlockSpec((1, d), lambda b, i: (0, 0)),
                pl.BlockSpec((1, 2, d), lambda b, i: (b, 0, 0))]
    in_specs += [pl.BlockSpec(w.shape, lambda b, i: (0, 0)) for w in ws]
    in_specs += [pl.BlockSpec(hg.shape, lambda b, i: (0, 0)) for hg in hgs]
    out_specs = [pl.BlockSpec((1, tm, w.shape[1]), lambda b, i: (b, i, 0)) for w in ws]
    out_shape = [jax.ShapeDtypeStruct((bn, sn, w.shape[1]), dt) for w, dt in zip(ws, out_dtypes)]
    return pl.pallas_call(
        functools.partial(_nm_linear_kernel, n_out=n_out, head_norm=head_norm, tn=tn),
        grid=(bn, sn // tm),
        in_specs=in_specs, out_specs=out_specs, out_shape=out_shape,
        compiler_params=_cparams("arbitrary", "arbitrary"),
        name="nm_linear",
    )(x, g, mod, *ws, *hgs)


def _out_linear_kernel(*refs, n_in):
    a_refs = refs[:n_in]
    w_refs = refs[n_in:2 * n_in]
    x_ref, mod_ref, g_ref, xo_ref, ho_ref = refs[2 * n_in:]
    y = _dot(a_refs[0][0], w_refs[0][...])
    for a_ref, w_ref in zip(a_refs[1:], w_refs[1:]):
        y = y + _dot(a_ref[0], w_ref[...])
    xn = x_ref[0] + mod_ref[0, 0:1, :] * y
    xo_ref[0] = xn
    ho_ref[0] = _rms_mod(xn, g_ref[...], mod_ref[0, 1:2, :], mod_ref[0, 2:3, :]).astype(BF16)


def _out_linear(acts, ws, x, mod, g, tm=512):
    bn, sn, d = x.shape
    tm = min(tm, sn)
    n_in = len(acts)
    in_specs = [pl.BlockSpec((1, tm, a.shape[2]), lambda b, i: (b, i, 0)) for a in acts]
    in_specs += [pl.BlockSpec(w.shape, lambda b, i: (0, 0)) for w in ws]
    in_specs += [pl.BlockSpec((1, tm, d), lambda b, i: (b, i, 0)),
                 pl.BlockSpec((1, 3, d), lambda b, i: (b, 0, 0)),
                 pl.BlockSpec((1, d), lambda b, i: (0, 0))]
    return pl.pallas_call(
        functools.partial(_out_linear_kernel, n_in=n_in),
        grid=(bn, sn // tm),
        in_specs=in_specs,
        out_specs=[pl.BlockSpec((1, tm, d), lambda b, i: (b, i, 0))] * 2,
        out_shape=[jax.ShapeDtypeStruct((bn, sn, d), F32), jax.ShapeDtypeStruct((bn, sn, d), BF16)],
        compiler_params=_cparams("arbitrary", "arbitrary"),
        name="out_linear",
    )(*acts, *ws, x, mod, g)


CONV_HALO = 16


def _conv_chunk(src_ref, s, seg_len, w, bias, rows=SSD_CHUNK):
    ncol = src_ref.shape[2]
    if s > 0:
        prev = src_ref[0, s - CONV_HALO:s, :].astype(F32)
    else:
        prev = jnp.zeros((CONV_HALO, ncol), F32)
    cur = src_ref[0, s:s + rows, :].astype(F32)
    if s + rows < seg_len:
        nxt = src_ref[0, s + rows:s + rows + CONV_HALO, :].astype(F32)
    else:
        nxt = jnp.zeros((CONV_HALO, ncol), F32)
    win = jnp.concatenate([prev, cur, nxt], axis=0)
    taps = w.shape[0]
    acc = bias
    for k in range(taps):
        off = CONV_HALO - taps // 2 + k
        acc = acc + win[off:off + rows, :] * w[k:k + 1, :]
    return acc


def _lane_head_expand(cols, width):
    nh = len(cols)
    hd = width // nh
    rows = cols[0].shape[0]
    lane_head = lax.broadcasted_iota(jnp.int32, (rows, width), 1) // hd
    out = jnp.broadcast_to(cols[nh - 1], (rows, width))
    for h in range(nh - 2, -1, -1):
        out = jnp.where(lane_head == h, jnp.broadcast_to(cols[h], (rows, width)), out)
    return out


def _ssd_kernel(xs_x, bm_x, cm_x, xs_c, bm_c, cm_c, dt_x, dt_c, z_x, z_c,
                cw_xs, cw_b, cw_c, cb_xs, cb_b, cb_c, alog_ref, dtb_ref, dsk_ref, ng_ref,
                y_x, y_c,
                xs_s, bm_s, cm_s, dt_s, y_s, st_s, *, sx, sc):
    q = SSD_CHUNK
    nc_c, nc_x = sc // q, sx // q
    nc = nc_c + nc_x
    gw = xs_s.shape[1]

    for seg_ref3, seg_len, base in (((xs_c, bm_c, cm_c), sc, 0), ((xs_x, bm_x, cm_x), sx, sc)):
        for ci in range(seg_len // q):
            s = ci * q
            for src, dst, w_ref, b_ref in zip(seg_ref3, (xs_s, bm_s, cm_s), (cw_xs, cw_b, cw_c),
                                              (cb_xs, cb_b, cb_c)):
                dst[base + s:base + s + q, :] = _silu(_conv_chunk(src, s, seg_len, w_ref[...], b_ref[...]))
    dt_s[0:sc, :] = _softplus(dt_c[0] + dtb_ref[0])
    dt_s[sc:sc + sx, :] = _softplus(dt_x[0] + dtb_ref[0])

    a_neg = -jnp.exp(alog_ref[0])
    row = lax.broadcasted_iota(jnp.int32, (q, q), 0)
    col = lax.broadcasted_iota(jnp.int32, (q, q), 1)
    tri = ((col <= row).astype(F32), (col >= row).astype(F32))
    keep = (col <= row, col >= row)
    lane_head = lax.broadcasted_iota(jnp.int32, (q, gw), 1) // SSD_HEAD_DIM

    for d in range(2):
        st_s[...] = jnp.zeros_like(st_s)

        def chunk_body(i, carry, d=d):
            if d == 0:
                ci = i
            else:
                ci = jnp.where(i < nc_c, nc_c - 1 - i, nc + nc_c - 1 - i)
            r0 = pl.multiple_of(ci * q, q)
            xs = xs_s[pl.ds(r0, q), :]
            bm = bm_s[pl.ds(r0, q), :]
            cm = cm_s[pl.ds(r0, q), :]
            dt = dt_s[pl.ds(r0, q), :]
            la = dt * a_neg
            cs = jnp.dot(tri[d], la, preferred_element_type=F32, precision=HIGHEST)
            cs_t = cs.T
            cb = _dot_nt(cm.astype(BF16), bm.astype(BF16))
            heads = [d * SSD_HPG + h for h in range(SSD_HPG)]
            dt_mat = _lane_head_expand([dt[:, c:c + 1] for c in heads], gw)
            cs_mat = _lane_head_expand([cs[:, c:c + 1] for c in heads], gw)
            xd = xs * dt_mat
            xd_b = xd.astype(BF16)
            y = jnp.zeros((q, gw), F32)
            for h, c in enumerate(heads):
                diff = cs[:, c:c + 1] - cs_t[c:c + 1, :]
                lmat = jnp.exp(jnp.where(keep[d], diff, NEG_INF))
                y = jnp.where(lane_head == h, _dot((cb * lmat).astype(BF16), xd_b), y)
            st = st_s[...]
            y = y + _dot(cm.astype(BF16), st.astype(BF16)) * jnp.exp(cs_mat)
            end = q - 1 if d == 0 else 0
            cs_end = cs_mat[end:end + 1, :]
            s_new = _dot(bm.T.astype(BF16), (xd * jnp.exp(cs_end - cs_mat)).astype(BF16))
            st_s[...] = st * jnp.exp(cs_end) + s_new
            if d == 0:
                y_s[pl.ds(r0, q), :] = y
            else:
                y_s[pl.ds(r0, q), :] = y_s[pl.ds(r0, q), :] + y
            return carry

        lax.fori_loop(0, nc, chunk_body, 0)

    for ci in range(nc):
        s = ci * q
        if ci < nc_c:
            z = z_c[0, s:s + q, :]
        else:
            z = z_x[0, s - sc:s - sc + q, :]
        y = (y_s[s:s + q, :] + dsk_ref[0] * xs_s[s:s + q, :]) * _silu(z.astype(F32))
        ms = jnp.mean(y * y, axis=-1, keepdims=True)
        out = (y * lax.rsqrt(ms + EPS) * ng_ref[0]).astype(BF16)
        if ci < nc_c:
            y_c[0, s:s + q, :] = out
        else:
            y_x[0, s - sc:s - sc + q, :] = out


def _ssd_mixer(xbc_x, xbc_c, dt_x, dt_c, z_x, z_c, conv_w, conv_b, alog_g, dtb_g, dsk_g, ng_g):
    bsz, sx, _ = xbc_x.shape
    sc = xbc_c.shape[1]
    g, gw, n = SSD_GROUPS, SSD_GW, SSD_STATE
    nb = SSD_INNER // n
    taps = conv_w.shape[0]

    def seq(s, w, off):
        return pl.BlockSpec((1, s, w), lambda b, j, off=off: (b, 0, off + j))

    def par(r, w, off):
        return pl.BlockSpec((r, w), lambda b, j, off=off: (0, off + j))

    def grp(w):
        return pl.BlockSpec((1, 1, w), lambda b, j: (j, 0, 0))

    in_specs = [seq(sx, gw, 0), seq(sx, n, nb), seq(sx, n, nb + g),
                seq(sc, gw, 0), seq(sc, n, nb), seq(sc, n, nb + g),
                seq(sx, LANES, 0), seq(sc, LANES, 0), seq(sx, gw, 0), seq(sc, gw, 0),
                par(taps, gw, 0), par(taps, n, nb), par(taps, n, nb + g),
                par(1, gw, 0), par(1, n, nb), par(1, n, nb + g),
                grp(LANES), grp(LANES), grp(gw), grp(gw)]
    stot = sx + sc
    return pl.pallas_call(
        functools.partial(_ssd_kernel, sx=sx, sc=sc),
        grid=(bsz, g),
        in_specs=in_specs,
        out_specs=[seq(sx, gw, 0), seq(sc, gw, 0)],
        out_shape=[jax.ShapeDtypeStruct((bsz, sx, SSD_INNER), BF16),
                   jax.ShapeDtypeStruct((bsz, sc, SSD_INNER), BF16)],
        scratch_shapes=[pltpu.VMEM((stot, gw), F32), pltpu.VMEM((stot, n), F32), pltpu.VMEM((stot, n), F32),
                        pltpu.VMEM((stot, LANES), F32), pltpu.VMEM((stot, gw), F32), pltpu.VMEM((n, gw), F32)],
        compiler_params=_cparams("arbitrary", "arbitrary"),
        name="ssd_mixer",
    )(xbc_x, xbc_x, xbc_x, xbc_c, xbc_c, xbc_c, dt_x, dt_c, z_x, z_c,
      conv_w, conv_w, conv_w, conv_b, conv_b, conv_b, alog_g, dtb_g, dsk_g, ng_g)


def _lru_kernel(xl_x, xl_c, gt_x, gt_c, cw, cb, wa, wx, ba, bx, lam, y_x, y_c,
                xr_s, a_s, b_s, y_s, *, sx, sc):
    q = SSD_CHUNK
    stot = sx + sc
    w = xr_s.shape[1]
    for src, seg_len, base in ((xl_c, sc, 0), (xl_x, sx, sc)):
        for ci in range(seg_len // q):
            s = ci * q
            xr_s[base + s:base + s + q, :] = _conv_chunk(src, s, seg_len, cw[...], cb[...])

    ng = stot // SUBLANES
    ng_c = sc // SUBLANES
    sub = lax.broadcasted_iota(jnp.int32, (SUBLANES, w), 0)
    rt = q
    for d in range(2):
        nsp = _softplus(-lam[d:d + 1, :])
        for ci in range(stot // rt):
            s = ci * rt
            xr = xr_s[s:s + rt, :]
            xb = xr.astype(BF16)
            r = _sigmoid(_dot(xb, wa[d, 0]) + ba[d:d + 1, :])
            ig = _sigmoid(_dot(xb, wx[d, 0]) + bx[d:d + 1, :])
            a = jnp.exp(-LRU_C * r * nsp)
            a_s[s:s + rt, :] = a
            b_s[s:s + rt, :] = jnp.sqrt(1.0 - a * a) * (ig * xr)

        def group_body(k, carry, d=d):
            if d == 0:
                gi = k
            else:
                gi = jnp.where(k < ng_c, ng_c - 1 - k, ng + ng_c - 1 - k)
            r0 = pl.multiple_of(gi * SUBLANES, SUBLANES)
            a = a_s[pl.ds(r0, SUBLANES), :]
            b = b_s[pl.ds(r0, SUBLANES), :]
            for sh in (1, 2, 4):
                if d == 0:
                    valid = sub >= sh
                    a_sh = pltpu.roll(a, sh, axis=0)
                    b_sh = pltpu.roll(b, sh, axis=0)
                else:
                    valid = sub < SUBLANES - sh
                    a_sh = pltpu.roll(a, SUBLANES - sh, axis=0)
                    b_sh = pltpu.roll(b, SUBLANES - sh, axis=0)
                b = jnp.where(valid, a * b_sh + b, b)
                a = jnp.where(valid, a * a_sh, a)
            h = a * carry + b
            if d == 0:
                y_s[pl.ds(r0, SUBLANES), :] = h
                last = h[SUBLANES - 1:SUBLANES, :]
            else:
                y_s[pl.ds(r0, SUBLANES), :] = y_s[pl.ds(r0, SUBLANES), :] + h
                last = h[0:1, :]
            return jnp.broadcast_to(last, (SUBLANES, w))

        lax.fori_loop(0, ng, group_body, jnp.zeros((SUBLANES, w), F32))

    for ci in range(stot // rt):
        s = ci * rt
        if s < sc:
            gate = gt_c[0, s:s + rt, :]
        else:
            gate = gt_x[0, s - sc:s - sc + rt, :]
        out = (y_s[s:s + rt, :] * _gelu_tanh(gate.astype(F32))).astype(BF16)
        if s < sc:
            y_c[0, s:s + rt, :] = out
        else:
            y_x[0, s - sc:s - sc + rt, :] = out


def _lru_mixer(xl_x, xl_c, gt_x, gt_c, conv_w, conv_b, wa_bd, wx_bd, ba, bx, lam):
    bsz, sx, width = xl_x.shape
    sc = xl_c.shape[1]
    w = LRU_TILE
    taps = conv_w.shape[0]

    def seq(s):
        return pl.BlockSpec((1, s, w), lambda b, j: (b, 0, j))

    def par(r):
        return pl.BlockSpec((r, w), lambda b, j: (0, j))

    wspec = pl.BlockSpec((2, 1, w, w), lambda b, j: (0, j, 0, 0))
    stot = sx + sc
    return pl.pallas_call(
        functools.partial(_lru_kernel, sx=sx, sc=sc),
        grid=(bsz, width // w),
        in_specs=[seq(sx), seq(sc), seq(sx), seq(sc), par(taps), par(1), wspec, wspec, par(2), par(2), par(2)],
        out_specs=[seq(sx), seq(sc)],
        out_shape=[jax.ShapeDtypeStruct((bsz, sx, width), BF16), jax.ShapeDtypeStruct((bsz, sc, width), BF16)],
        scratch_shapes=[pltpu.VMEM((stot, w), F32)] * 4,
        compiler_params=_cparams("arbitrary", "arbitrary"),
        name="lru_mixer",
    )(xl_x, xl_c, gt_x, gt_c, conv_w, conv_b, wa_bd, wx_bd, ba, bx, lam)


def _rpb_table_kernel(rpb_ref, o_ref, *, n_dr, n_dc):
    h = pl.program_id(0)
    w = GRID_W
    qcol = lax.broadcasted_iota(jnp.int32, (w, 2 * w), 0)
    lane = lax.broadcasted_iota(jnp.int32, (w, 2 * w), 1)
    kcol = lane % w
    hi = lane >= w
    rel = kcol - qcol + (NA_KW - 1)
    cstart = jnp.clip(qcol - NA_KW // 2, 0, w - NA_KW)
    in_win = (kcol >= cstart) & (kcol < cstart + NA_KW)
    for d in range(n_dr + 1):
        acc = jnp.full((w, 2 * w), NEG_INF, F32)
        for dc in range(n_dc):
            lo = rpb_ref[(h * n_dr + d - 1) * n_dc + dc] if d >= 1 else NEG_INF
            up = rpb_ref[(h * n_dr + d) * n_dc + dc] if d < n_dr else NEG_INF
            acc = jnp.where(rel == dc, jnp.where(hi, up, lo), acc)
        valid = in_win
        if d == 0:
            valid = valid & hi
        if d == n_dr:
            valid = valid & jnp.logical_not(hi)
        o_ref[0, d] = jnp.where(valid, acc, NEG_INF)


def _rpb_table(rpb):
    nh, n_dr, n_dc = rpb.shape
    return pl.pallas_call(
        functools.partial(_rpb_table_kernel, n_dr=n_dr, n_dc=n_dc),
        grid=(nh,),
        in_specs=[pl.BlockSpec(memory_space=pltpu.SMEM)],
        out_specs=pl.BlockSpec((1, n_dr + 1, GRID_W, 2 * GRID_W), lambda h: (h, 0, 0, 0)),
        out_shape=jax.ShapeDtypeStruct((nh, n_dr + 1, GRID_W, 2 * GRID_W), F32),
        compiler_params=_cparams("arbitrary"),
        name="rpb_table",
    )(rpb.reshape(-1))


def _na_kernel(*refs, sx, sc, want_ctx):
    if want_ctx:
        q_x, k_x, v_x, q_c, k_c, v_c, tab, o_x, o_c = refs
    else:
        q_x, k_x, v_x, k_c, v_c, tab, o_x = refs
    w = GRID_W
    rows = sx // w
    qb = NA_QROWS * w
    kb = NA_KROWS * w
    n_blk = rows // NA_QROWS
    scale = NA_HEAD_DIM ** -0.5
    lane = lax.broadcasted_iota(jnp.int32, (1, 2 * NA_HEAD_DIM), 1)
    in_head = (lane < NA_HEAD_DIM, lane >= NA_HEAD_DIM)
    keyrow = lax.broadcasted_iota(jnp.int32, (1, kb), 1) // w
    kc = k_c[0]
    vc = v_c[0]
    n_tab = tab.shape[1]

    def softmax_pv(parts):
        m = parts[0][0].max(axis=-1, keepdims=True)
        for s, _ in parts[1:]:
            m = jnp.maximum(m, s.max(axis=-1, keepdims=True))
        acc, den = None, None
        for s, v in parts:
            e = jnp.exp(s - m)
            den = e.sum(axis=-1, keepdims=True) if den is None else den + e.sum(axis=-1, keepdims=True)
            pv = _dot(e.astype(BF16), v)
            acc = pv if acc is None else acc + pv
        return acc / den

    def block_body(rb, carry):
        ws = jnp.clip(NA_QROWS * rb - NA_KH // 2, 0, rows - NA_KROWS)
        q0 = pl.multiple_of(rb * qb, qb)
        k0 = pl.multiple_of(ws * w, w)
        qblk = q_x[0, pl.ds(q0, qb), :]
        kwin = k_x[0, pl.ds(k0, kb), :]
        vwin = v_x[0, pl.ds(k0, kb), :]
        out = jnp.zeros((qb, 2 * NA_HEAD_DIM), F32)
        for hh in range(2):
            qm = jnp.where(in_head[hh], qblk, jnp.zeros_like(qblk))
            s_loc = _dot_nt(qm, kwin) * scale
            s_ctx = _dot_nt(qm, kc) * scale
            pieces = []
            for rq in range(NA_QROWS):
                r = NA_QROWS * rb + rq
                rs = jnp.clip(r - NA_KH // 2, 0, rows - NA_KH)
                lo = rs - ws
                valid = (keyrow >= lo) & (keyrow < lo + NA_KH)
                blocks = []
                for ip in range(NA_KROWS // 2):
                    dr_lo = ws + 2 * ip - r + NA_KH - 1
                    blocks.append(tab[hh, jnp.clip(dr_lo + 1, 0, n_tab - 1)])
                bias = jnp.concatenate(blocks, axis=1)
                piece = s_loc[rq * w:(rq + 1) * w, :] + bias
                pieces.append(jnp.where(valid, piece, NEG_INF))
            s_loc = jnp.concatenate(pieces, axis=0)
            o = softmax_pv([(s_loc, vwin), (s_ctx, vc)])
            out = jnp.where(in_head[hh], o, out)
        o_x[0, pl.ds(q0, qb), :] = out.astype(o_x.dtype)
        return carry

    lax.fori_loop(0, n_blk, block_body, 0)

    if want_ctx:
        qc = q_c[0]
        out = jnp.zeros((sc, 2 * NA_HEAD_DIM), F32)
        for hh in range(2):
            qm = jnp.where(in_head[hh], qc, jnp.zeros_like(qc))
            o = softmax_pv([(_dot_nt(qm, kc) * scale, vc)])
            out = jnp.where(in_head[hh], o, out)
        o_c[0] = out.astype(o_c.dtype)


def _na_attention(q_x, k_x, v_x, q_c, k_c, v_c, table, want_ctx):
    bsz, sx, dim = q_x.shape
    sc = k_c.shape[1]
    pw = 2 * NA_HEAD_DIM
    n_pair = dim // pw

    def seq(s):
        return pl.BlockSpec((1, s, pw), lambda p, b: (b, 0, p))

    tspec = pl.BlockSpec((2,) + table.shape[1:], lambda p, b: (p, 0, 0, 0))
    if want_ctx:
        args = (q_x, k_x, v_x, q_c, k_c, v_c, table)
        in_specs = [seq(sx)] * 3 + [seq(sc)] * 3 + [tspec]
        out_specs = [seq(sx), seq(sc)]
        out_shape = [jax.ShapeDtypeStruct((bsz, sx, dim), BF16), jax.ShapeDtypeStruct((bsz, sc, dim), BF16)]
    else:
        args = (q_x, k_x, v_x, k_c, v_c, table)
        in_specs = [seq(sx)] * 3 + [seq(sc)] * 2 + [tspec]
        out_specs = [seq(sx)]
        out_shape = [jax.ShapeDtypeStruct((bsz, sx, dim), BF16)]
    res = pl.pallas_call(
        functools.partial(_na_kernel, sx=sx, sc=sc, want_ctx=want_ctx),
        grid=(n_pair, bsz),
        in_specs=in_specs, out_specs=out_specs, out_shape=out_shape,
        compiler_params=_cparams("arbitrary", "arbitrary"),
        name="na_attention",
    )(*args)
    return res if want_ctx else (res[0], None)


def _top_rows(s, k, exact):
    n, tt = s.shape
    rowid = lax.broadcasted_iota(jnp.int32, (n, tt), 0).astype(F32)
    top_id = lax.broadcasted_iota(jnp.int32, (k, tt), 0)
    work = s
    top = jnp.zeros((k, tt), F32)
    rank = jnp.full((n, tt), float(k), F32)
    for it in range(k):
        m = jnp.max(work, axis=0, keepdims=True)
        sel = work == m
        if exact:
            sel = rowid == jnp.min(jnp.where(sel, rowid, float(n)), axis=0, keepdims=True)
        top = jnp.where(top_id == it, m, top)
        rank = jnp.where(sel, float(it), rank)
        work = jnp.where(sel, -jnp.inf, work)
    n_sel = jnp.sum(jnp.where(rank < float(k), 1.0, 0.0), axis=0, keepdims=True)
    return top, rank, n_sel


def _peer_route_kernel(h_ref, wq_ref, keys_ref, cnt_ref, rk_ref, e0_ref, e1_ref, s_s, top_s, rank_s, pick_s):
    nk, k = PEER_KEYS, PEER_TOPK
    tt = h_ref.shape[1]
    q = _dot(h_ref[0], wq_ref[0]).astype(BF16)
    for z in range(2):
        s_s[z] = _dot_nt(keys_ref[z], q[:, z * nk:(z + 1) * nk])

    def strip(si, carry):
        cols = pl.ds(pl.multiple_of(si * LANES, LANES), LANES)
        s = [s_s[0, :, cols], s_s[1, :, cols]]

        def stage1(exact):
            dev = []
            for z in range(2):
                top, rank, n_sel = _top_rows(s[z], k, exact)
                top_s[z] = top
                rank_s[z] = rank
                dev.append(jnp.max(jnp.abs(n_sel - float(k))))
            return jnp.maximum(dev[0], dev[1])

        tie1 = stage1(False)

        @pl.when(tie1 > 0.5)
        def _():
            stage1(True)

        t0, t1 = top_s[0], top_s[1]
        cand = jnp.concatenate([t0[0:1] + t1] + [t0[a:a + 1] + t1[0:8] for a in range(1, 8)]
                               + [t0[8:16] + t1[0:1]], axis=0)

        def stage2(exact):
            _, rank, n_sel = _top_rows(cand, k, exact)
            pick_s[...] = jnp.where(rank < float(k), 1.0, 0.0)
            return jnp.max(jnp.abs(n_sel - float(k)))

        tie2 = stage2(False)

        @pl.when(tie2 > 0.5)
        def _():
            stage2(True)

        pf = pick_s[...]
        z_sum = jnp.sum(pf * jnp.exp(cand - cand[0:1]), axis=0, keepdims=True)
        cnts = [jnp.sum(pf[0:k], axis=0, keepdims=True)]
        cnts += [jnp.sum(pf[k + 8 * (a - 1):k + 8 * a], axis=0, keepdims=True) for a in range(1, 8)]
        cnts += [pf[k + 56 + a:k + 57 + a] for a in range(8)]
        rank0 = rank_s[0]
        cnt_i = jnp.zeros_like(rank0)
        for a in range(k):
            cnt_i = jnp.where(rank0 == float(a), cnts[a], cnt_i)
        e0 = jnp.exp(s[0] - t0[0:1]) / z_sum
        for c in range(nk // PEER_ICHUNK):
            cnt_ref[0, c, :, cols] = cnt_i[c * PEER_ICHUNK:(c + 1) * PEER_ICHUNK]
            e0_ref[0, c, :, cols] = e0[c * PEER_ICHUNK:(c + 1) * PEER_ICHUNK]
        rk_ref[0, :, cols] = rank_s[1].astype(rk_ref.dtype)
        e1_ref[0, :, cols] = jnp.exp(s[1] - t1[0:1]).astype(e1_ref.dtype)
        return carry

    lax.fori_loop(0, tt // LANES, strip, 0)


def _peer_route(hx, w_q, keys, tt):
    bn, sn, d = hx.shape
    nk, nh, ic = PEER_KEYS, PEER_HEADS, PEER_ICHUNK
    spec_i = pl.BlockSpec((1, nk // ic, ic, tt), lambda b, i, h: (b, 0, h, i))
    spec_j = pl.BlockSpec((1, nk, tt), lambda b, i, h: (b, h, i))
    shape_i = jax.ShapeDtypeStruct((bn, nk // ic, nh * ic, sn), F32)
    shape_j = jax.ShapeDtypeStruct((bn, nh * nk, sn), BF16)
    n_cand = PEER_TOPK + 8 * 8
    return pl.pallas_call(
        _peer_route_kernel,
        grid=(bn, sn // tt, nh),
        in_specs=[pl.BlockSpec((1, tt, d), lambda b, i, h: (b, i, 0)),
                  pl.BlockSpec((1, d, 2 * nk), lambda b, i, h: (h, 0, 0)),
                  pl.BlockSpec((2, nk, keys.shape[2]), lambda b, i, h: (h, 0, 0))],
        out_specs=[spec_i, spec_j, spec_i, spec_j],
        out_shape=[shape_i, shape_j, shape_i, shape_j],
        scratch_shapes=[pltpu.VMEM((2, nk, tt), F32), pltpu.VMEM((2, PEER_TOPK, LANES), F32),
                        pltpu.VMEM((2, nk, LANES), F32), pltpu.VMEM((n_cand, LANES), F32)],
        compiler_params=_cparams("arbitrary", "arbitrary", "arbitrary"),
        name="peer_route",
    )(hx, w_q, keys)


def _peer_dense_kernel(h_ref, cnt_ref, rk_ref, e0_ref, e1_ref, u_ref, vt_ref, x_ref, g_ref, o_ref,
                       acc_ref, act_ref, p_ref, hx_s, rk_s, e1_s):
    nk, ic = PEER_KEYS, PEER_ICHUNK
    ck = pl.program_id(2)
    tt = h_ref.shape[1]
    n_strip, _, sw = act_ref.shape
    il_group = 4

    @pl.when(ck == 0)
    def _():
        acc_ref[...] = jnp.zeros_like(acc_ref)
        hx_s[...] = h_ref[0]
        rk_s[...] = rk_ref[0]
        e1_s[...] = e1_ref[0]

    def activations(s):
        act_ref[s] = _gelu_tanh(_dot_nt(u_ref[...], hx_s[s * sw:(s + 1) * sw, :])).astype(act_ref.dtype)

    def gate_weights(s):
        for sub in range(sw // LANES):
            cols = slice(s * sw + sub * LANES, s * sw + (sub + 1) * LANES)
            lcols = slice(sub * LANES, (sub + 1) * LANES)
            for ig in range(ic // il_group):
                wgt = [None] * il_group
                for h in range(PEER_HEADS):
                    rk = rk_s[h * nk:(h + 1) * nk, cols]
                    e1 = e1_s[h * nk:(h + 1) * nk, cols]
                    cnt8 = cnt_ref[0, 0, h * ic:(h + 1) * ic, cols]
                    e08 = e0_ref[0, 0, h * ic:(h + 1) * ic, cols]
                    for g in range(il_group):
                        il = ig * il_group + g
                        cnt_row = jnp.broadcast_to(cnt8[il:il + 1], (nk, LANES)).astype(rk.dtype)
                        e0_row = jnp.broadcast_to(e08[il:il + 1], (nk, LANES)).astype(e1.dtype)
                        term = jnp.where(rk < cnt_row, e1 * e0_row, jnp.zeros_like(e1))
                        wgt[g] = term if wgt[g] is None else wgt[g] + term
                for g in range(il_group):
                    rows = slice((ig * il_group + g) * nk, (ig * il_group + g + 1) * nk)
                    p_ref[s, rows, lcols] = wgt[g]

    activations(0)
    for s in range(n_strip):
        if s + 1 < n_strip:
            activations(s + 1)
        gate_weights(s)
        acc_ref[:, s * sw:(s + 1) * sw] += _dot(vt_ref[...], p_ref[s] * act_ref[s])

    @pl.when(ck == pl.num_programs(2) - 1)
    def _():
        o_ref[0] = x_ref[0] + g_ref[0] * acc_ref[...].T


PEER_STRIP = 256


def _peer_dense(hx, route, u, v_t, x, gate, tt):
    bn, sn, d = hx.shape
    nk, nh, ic = PEER_KEYS, PEER_HEADS, PEER_ICHUNK
    ne = ic * nk
    n_chunk = u.shape[0] // ne
    spec_i = pl.BlockSpec((1, 1, nh * ic, tt), lambda b, i, c: (b, c, 0, i))
    spec_j = pl.BlockSpec((1, nh * nk, tt), lambda b, i, c: (b, 0, i))
    tok = pl.BlockSpec((1, tt, d), lambda b, i, c: (b, i, 0))
    sw = min(PEER_STRIP, tt)
    return pl.pallas_call(
        _peer_dense_kernel,
        grid=(bn, sn // tt, n_chunk),
        in_specs=[tok, spec_i, spec_j, spec_i, spec_j,
                  pl.BlockSpec((ne, d), lambda b, i, c: (c, 0)),
                  pl.BlockSpec((d, ne), lambda b, i, c: (0, c)),
                  tok,
                  pl.BlockSpec((1, 1, d), lambda b, i, c: (b, 0, 0))],
        out_specs=tok,
        out_shape=jax.ShapeDtypeStruct((bn, sn, d), F32),
        scratch_shapes=[pltpu.VMEM((d, tt), F32), pltpu.VMEM((tt // sw, ne, sw), BF16),
                        pltpu.VMEM((tt // sw, ne, sw), BF16),
                        pltpu.VMEM((tt, d), BF16), pltpu.VMEM((nh * nk, tt), BF16), pltpu.VMEM((nh * nk, tt), BF16)],
        compiler_params=_cparams("arbitrary", "arbitrary", "arbitrary"),
        name="peer_dense",
    )(hx, *route, u, v_t, x, gate)


def _peer(hx, x, gate, w_q, keys, u, v_t, tt=1024):
    tt = min(tt, hx.shape[1])
    route = _peer_route(hx, w_q, keys, tt)
    return _peer_dense(hx, route, u, v_t, x, gate, tt)


def _group_lanes(p, width):
    g = p.reshape(2, SSD_GROUPS, SSD_HPG).transpose(1, 0, 2).reshape(SSD_GROUPS, 2 * SSD_HPG)
    return jnp.pad(g, ((0, 0), (0, width - 2 * SSD_HPG))).reshape(SSD_GROUPS, 1, width)


def _block_diag(w, tile):
    two, nb, bd, _ = w.shape
    per = tile // bd
    w = w.reshape(two, nb // per, per, bd, bd)
    eye = jnp.eye(per, dtype=w.dtype)
    return jnp.einsum("dtpij,pq->dtpiqj", w, eye).reshape(two, nb // per, tile, tile)


def kernel(x, c, ctx, c_ctx, ada_w, ada_b, norm1_g, norm2_g, ev_w_in, ev_conv_w, ev_conv_b, ev_a_log,
           ev_dt_bias, ev_d, ev_ssd_norm_g, ev_lru_conv_w, ev_lru_conv_b, ev_lru_wa, ev_lru_ba, ev_lru_wx,
           ev_lru_bx, ev_lru_lam, ev_w_out, od_w_qkv, od_q_norm_g, od_k_norm_g, od_rpb, od_w_o,
           pe_w_q, pe_keys, pe_u, pe_v):
    bsz, sx, d = x.shape
    sc = ctx.shape[1]
    depth = ada_w.shape[0]

    n_c = bsz + 1
    rows = -(-n_c // SUBLANES) * SUBLANES
    c_all = jnp.concatenate([c, c_ctx[None], jnp.zeros((rows - n_c, d), F32)], axis=0)
    mods = _ada_mods(c_all, ada_w, ada_b).reshape(depth, rows, 6, d)

    ctx = ctx.reshape(1, bsz * sc, d)

    def per_batch(t):
        return t.reshape(bsz, sc, t.shape[-1])

    for layer in range(depth):
        last = layer == depth - 1
        j = layer // 2
        mod_x = mods[layer, :bsz]
        mod_c = mods[layer, bsz:bsz + 1]
        g1 = norm1_g[layer][None]
        g2 = norm2_g[layer][None]
        want_ctx = not last

        if layer % 2 == 0:
            w_in = ev_w_in[j]
            o_dt, o_xl = SSD_XBC, SSD_XBC + 2 * SSD_HEADS
            o_z = o_xl + LRU_WIDTH
            o_gate = o_z + SSD_INNER
            w_dt = w_in[:, o_dt:o_xl].reshape(d, 2, SSD_GROUPS, SSD_HPG).transpose(0, 2, 1, 3)
            w_dt = jnp.pad(w_dt.reshape(d, SSD_GROUPS, 2 * SSD_HPG), ((0, 0), (0, 0), (0, LANES - 2 * SSD_HPG)))
            ws = [w_in[:, :o_dt].astype(BF16), w_dt.reshape(d, SSD_GROUPS * LANES).astype(BF16),
                  w_in[:, o_xl:o_z].astype(BF16), w_in[:, o_z:o_gate].astype(BF16), w_in[:, o_gate:].astype(BF16)]
            dts = [BF16, F32, BF16, BF16, BF16]
            px = _nm_linear(x, g1, mod_x[:, 0:2], ws, dts)
            pc = [per_batch(t) for t in _nm_linear(ctx, g1, mod_c[:, 0:2], ws, dts)]
            y_ssd_x, y_ssd_c = _ssd_mixer(
                px[0], pc[0], px[1], pc[1], px[3], pc[3], ev_conv_w[j], ev_conv_b[j][None],
                _group_lanes(ev_a_log[j], LANES), _group_lanes(ev_dt_bias[j], LANES),
                jnp.repeat(ev_d[j], SSD_HEAD_DIM).reshape(SSD_GROUPS, 1, SSD_GW),
                ev_ssd_norm_g[j].reshape(SSD_GROUPS, 1, SSD_GW))
            y_lru_x, y_lru_c = _lru_mixer(
                px[2], pc[2], px[4], pc[4], ev_lru_conv_w[j], ev_lru_conv_b[j][None],
                _block_diag(ev_lru_wa[j], LRU_TILE).astype(BF16), _block_diag(ev_lru_wx[j], LRU_TILE).astype(BF16),
                ev_lru_ba[j], ev_lru_bx[j], ev_lru_lam[j])
            w_out = ev_w_out[j].astype(BF16)
            w_outs = [w_out[:SSD_INNER], w_out[SSD_INNER:]]
            acts_x = [y_ssd_x, y_lru_x]
            acts_c = [y_ssd_c.reshape(1, bsz * sc, -1), y_lru_c.reshape(1, bsz * sc, -1)]
        else:
            w_qkv = od_w_qkv[j].astype(BF16)
            nd = w_qkv.shape[1] // 3
            ws = [w_qkv[:, :nd], w_qkv[:, nd:2 * nd], w_qkv[:, 2 * nd:]]
            gains = [jnp.tile(od_q_norm_g[j], NA_HEADS)[None], jnp.tile(od_k_norm_g[j], NA_HEADS)[None], None]
            q_x, k_x, v_x = _nm_linear(x, g1, mod_x[:, 0:2], ws, [BF16] * 3, gains)
            q_c, k_c, v_c = [per_batch(t) for t in _nm_linear(ctx, g1, mod_c[:, 0:2], ws, [BF16] * 3, gains)]
            table = _rpb_table(od_rpb[j])
            o_x, o_c = _na_attention(q_x, k_x, v_x, q_c, k_c, v_c, table, want_ctx)
            w_outs = [od_w_o[j].astype(BF16)]
            acts_x = [o_x]
            acts_c = [o_c.reshape(1, bsz * sc, -1)] if want_ctx else None

        w_q = pe_w_q[layer].reshape(d, PEER_HEADS, 2 * PEER_KEYS).transpose(1, 0, 2).astype(BF16)
        keys = pe_keys[layer].reshape(2 * PEER_HEADS, PEER_KEYS, -1).astype(BF16)
        u = pe_u[layer].astype(BF16)
        v_t = pe_v[layer].T.astype(BF16)

        x, hx = _out_linear(acts_x, w_outs, x, mod_x[:, 2:5], g2)
        x = _peer(hx, x, mod_x[:, 5:6], w_q, keys, u, v_t)
        if want_ctx:
            ctx, hc = _out_linear(acts_c, w_outs, ctx, mod_c[:, 2:5], g2)
            ctx = _peer(hc, ctx, mod_c[:, 5:6], w_q, keys, u, v_t)
    return x
```

```python
import functools
import math

import jax
import jax.numpy as jnp
from jax import lax
from jax.experimental import pallas as pl
from jax.experimental.pallas import tpu as pltpu

F32 = jnp.float32
BF16 = jnp.bfloat16
HIGHEST = lax.Precision.HIGHEST

EPS = 1e-6
NEG_INF = -1e30

GRID_W = 64
SSD_HEADS = 16
SSD_HEAD_DIM = 64
SSD_GROUPS = 4
SSD_HPG = SSD_HEADS // SSD_GROUPS
SSD_STATE = 128
SSD_CHUNK = 128
SSD_INNER = SSD_HEADS * SSD_HEAD_DIM
SSD_GW = SSD_INNER // SSD_GROUPS
SSD_XBC = SSD_INNER + 2 * SSD_GROUPS * SSD_STATE
LRU_WIDTH = 1024
LRU_BLOCKS = 16
LRU_BLOCK_DIM = LRU_WIDTH // LRU_BLOCKS
LRU_C = 8.0
LRU_TILE = 256
NA_HEADS = 16
NA_HEAD_DIM = 64
NA_KH = 8
NA_KW = 16
NA_QROWS = 4
NA_KROWS = 12
PEER_HEADS = 8
PEER_KEYS = 128
PEER_TOPK = 16
PEER_ICHUNK = 8

LANES = 128
SUBLANES = 8
VMEM_LIMIT_BYTES = 56 * 1024 * 1024


def _cparams(*sem):
    return pltpu.CompilerParams(dimension_semantics=sem, vmem_limit_bytes=VMEM_LIMIT_BYTES)


def _silu(x):
    return x * (1.0 / (1.0 + jnp.exp(-x)))


def _sigmoid(x):
    return 1.0 / (1.0 + jnp.exp(-x))


def _softplus(x):
    return jnp.maximum(x, 0.0) + jnp.log(1.0 + jnp.exp(-jnp.abs(x)))


def _gelu_tanh(x):
    k0 = -2.0 * math.sqrt(2.0 / math.pi)
    return x / (1.0 + jnp.exp(x * (k0 + (k0 * 0.044715) * (x * x))))


def _rms_mod(x, g, shift, scale):
    ms = jnp.mean(x * x, axis=-1, keepdims=True)
    y = x * lax.rsqrt(ms + EPS) * g
    return y * (1.0 + scale) + shift


def _dot(a, b):
    return jnp.dot(a, b, preferred_element_type=F32)


def _dot_nt(a, b):
    return lax.dot_general(a, b, (((1,), (1,)), ((), ())), preferred_element_type=F32)


def _ada_kernel(c_ref, w_ref, b_ref, o_ref):
    s = _silu(c_ref[...])
    o_ref[0] = jnp.dot(s, w_ref[0], preferred_element_type=F32, precision=HIGHEST) + b_ref[0]


def _ada_mods(c_all, ada_w, ada_b):
    depth, d, n = ada_w.shape
    rows = c_all.shape[0]
    tn = 1536
    return pl.pallas_call(
        _ada_kernel,
        grid=(depth, n // tn),
        in_specs=[pl.BlockSpec((rows, d), lambda l, j: (0, 0)),
                  pl.BlockSpec((1, d, tn), lambda l, j: (l, 0, j)),
                  pl.BlockSpec((1, 1, tn), lambda l, j: (l, 0, j))],
        out_specs=pl.BlockSpec((1, rows, tn), lambda l, j: (l, 0, j)),
        out_shape=jax.ShapeDtypeStruct((depth, rows, n), F32),
        compiler_params=_cparams("arbitrary", "arbitrary"),
        name="ada_mods",
    )(c_all, ada_w, ada_b.reshape(depth, 1, n))


def _head_block_ones(n):
    r = lax.broadcasted_iota(jnp.int32, (n, n), 0) // NA_HEAD_DIM
    c = lax.broadcasted_iota(jnp.int32, (n, n), 1) // NA_HEAD_DIM
    return (r == c).astype(F32)


def _nm_linear_kernel(*refs, n_out, head_norm, tn):
    x_ref, g_ref, mod_ref = refs[:3]
    w_refs = refs[3:3 + n_out]
    hg_refs = refs[3 + n_out:3 + n_out + sum(head_norm)]
    o_refs = refs[3 + n_out + sum(head_norm):]
    h = _rms_mod(x_ref[0], g_ref[...], mod_ref[0, 0:1, :], mod_ref[0, 1:2, :]).astype(BF16)
    hg_i = 0
    for w_ref, o_ref, hn in zip(w_refs, o_refs, head_norm):
        n = w_ref.shape[1]
        for j in range(n // tn):
            y = _dot(h, w_ref[:, j * tn:(j + 1) * tn])
            if hn:
                ss = jnp.dot(y * y, _head_block_ones(tn), preferred_element_type=F32, precision=HIGHEST)
                y = y * lax.rsqrt(ss * (1.0 / NA_HEAD_DIM) + EPS) * hg_refs[hg_i][:, j * tn:(j + 1) * tn]
            o_ref[0, :, j * tn:(j + 1) * tn] = y.astype(o_ref.dtype)
        hg_i += hn


def _nm_linear(x, g, mod, ws, out_dtypes, head_gains=None, tm=512, tn=256):
    bn, sn, d = x.shape
    tm = min(tm, sn)
    n_out = len(ws)
    head_gains = head_gains or [None] * n_out
    head_norm = tuple(hg is not None for hg in head_gains)
    hgs = [hg for hg in head_gains if hg is not None]
    in_specs = [pl.BlockSpec((1, tm, d), lambda b, i: (b, i, 0)),
                pl.BlockSpec((1, d), lambda b, i: (0, 0)),
                pl.BlockSpec((1, 2, d), lambda b, i: (b, 0, 0))]
    in_specs += [pl.BlockSpec(w.shape, lambda b, i: (0, 0)) for w in ws]
    in_specs += [pl.BlockSpec(hg.shape, lambda b, i: (0, 0)) for hg in hgs]
    out_specs = [pl.BlockSpec((1, tm, w.shape[1]), lambda b, i: (b, i, 0)) for w in ws]
    out_shape = [jax.ShapeDtypeStruct((bn, sn, w.shape[1]), dt) for w, dt in zip(ws, out_dtypes)]
    return pl.pallas_call(
        functools.partial(_nm_linear_kernel, n_out=n_out, head_norm=head_norm, tn=tn),
        grid=(bn, sn // tm),
        in_specs=in_specs, out_specs=out_specs, out_shape=out_shape,
        compiler_params=_cparams("arbitrary", "arbitrary"),
        name="nm_linear",
    )(x, g, mod, *ws, *hgs)


def _out_linear_kernel(*refs, n_in):
    a_refs = refs[:n_in]
    w_refs = refs[n_in:2 * n_in]
    x_ref, mod_ref, g_ref, xo_ref, ho_ref = refs[2 * n_in:]
    y = _dot(a_refs[0][0], w_refs[0][...])
    for a_ref, w_ref in zip(a_refs[1:], w_refs[1:]):
        y = y + _dot(a_ref[0], w_ref[...])
    xn = x_ref[0] + mod_ref[0, 0:1, :] * y
    xo_ref[0] = xn
    ho_ref[0] = _rms_mod(xn, g_ref[...], mod_ref[0, 1:2, :], mod_ref[0, 2:3, :]).astype(BF16)


def _out_linear(acts, ws, x, mod, g, tm=512):
    bn, sn, d = x.shape
    tm = min(tm, sn)
    n_in = len(acts)
    in_specs = [pl.BlockSpec((1, tm, a.shape[2]), lambda b, i: (b, i, 0)) for a in acts]
    in_specs += [pl.BlockSpec(w.shape, lambda b, i: (0, 0)) for w in ws]
    in_specs += [pl.BlockSpec((1, tm, d), lambda b, i: (b, i, 0)),
                 pl.BlockSpec((1, 3, d), lambda b, i: (b, 0, 0)),
                 pl.BlockSpec((1, d), lambda b, i: (0, 0))]
    return pl.pallas_call(
        functools.partial(_out_linear_kernel, n_in=n_in),
        grid=(bn, sn // tm),
        in_specs=in_specs,
        out_specs=[pl.BlockSpec((1, tm, d), lambda b, i: (b, i, 0))] * 2,
        out_shape=[jax.ShapeDtypeStruct((bn, sn, d), F32), jax.ShapeDtypeStruct((bn, sn, d), BF16)],
        compiler_params=_cparams("arbitrary", "arbitrary"),
        name="out_linear",
    )(*acts, *ws, x, mod, g)


CONV_HALO = 16


def _conv_chunk(src_ref, s, seg_len, w, bias, rows=SSD_CHUNK):
    ncol = src_ref.shape[2]
    if s > 0:
        prev = src_ref[0, s - CONV_HALO:s, :].astype(F32)
    else:
        prev = jnp.zeros((CONV_HALO, ncol), F32)
    cur = src_ref[0, s:s + rows, :].astype(F32)
    if s + rows < seg_len:
        nxt = src_ref[0, s + rows:s + rows + CONV_HALO, :].astype(F32)
    else:
        nxt = jnp.zeros((CONV_HALO, ncol), F32)
    win = jnp.concatenate([prev, cur, nxt], axis=0)
    taps = w.shape[0]
    acc = bias
    for k in range(taps):
        off = CONV_HALO - taps // 2 + k
        acc = acc + win[off:off + rows, :] * w[k:k + 1, :]
    return acc


def _lane_head_expand(cols, width):
    nh = len(cols)
    hd = width // nh
    rows = cols[0].shape[0]
    lane_head = lax.broadcasted_iota(jnp.int32, (rows, width), 1) // hd
    out = jnp.broadcast_to(cols[nh - 1], (rows, width))
    for h in range(nh - 2, -1, -1):
        out = jnp.where(lane_head == h, jnp.broadcast_to(cols[h], (rows, width)), out)
    return out


def _ssd_kernel(xs_x, bm_x, cm_x, xs_c, bm_c, cm_c, dt_x, dt_c, z_x, z_c,
                cw_xs, cw_b, cw_c, cb_xs, cb_b, cb_c, alog_ref, dtb_ref, dsk_ref, ng_ref,
                y_x, y_c,
                xs_s, bm_s, cm_s, dt_s, y_s, st_s, *, sx, sc):
    q = SSD_CHUNK
    nc_c, nc_x = sc // q, sx // q
    nc = nc_c + nc_x
    gw = xs_s.shape[1]

    for seg_ref3, seg_len, base in (((xs_c, bm_c, cm_c), sc, 0), ((xs_x, bm_x, cm_x), sx, sc)):
        for ci in range(seg_len // q):
            s = ci * q
            for src, dst, w_ref, b_ref in zip(seg_ref3, (xs_s, bm_s, cm_s), (cw_xs, cw_b, cw_c),
                                              (cb_xs, cb_b, cb_c)):
                dst[base + s:base + s + q, :] = _silu(_conv_chunk(src, s, seg_len, w_ref[...], b_ref[...]))
    dt_s[0:sc, :] = _softplus(dt_c[0] + dtb_ref[0])
    dt_s[sc:sc + sx, :] = _softplus(dt_x[0] + dtb_ref[0])

    a_neg = -jnp.exp(alog_ref[0])
    row = lax.broadcasted_iota(jnp.int32, (q, q), 0)
    col = lax.broadcasted_iota(jnp.int32, (q, q), 1)
    tri = ((col <= row).astype(F32), (col >= row).astype(F32))
    keep = (col <= row, col >= row)
    lane_head = lax.broadcasted_iota(jnp.int32, (q, gw), 1) // SSD_HEAD_DIM

    for d in range(2):
        st_s[...] = jnp.zeros_like(st_s)

        def chunk_body(i, carry, d=d):
            if d == 0:
                ci = i
            else:
                ci = jnp.where(i < nc_c, nc_c - 1 - i, nc + nc_c - 1 - i)
            r0 = pl.multiple_of(ci * q, q)
            xs = xs_s[pl.ds(r0, q), :]
            bm = bm_s[pl.ds(r0, q), :]
            cm = cm_s[pl.ds(r0, q), :]
            dt = dt_s[pl.ds(r0, q), :]
            la = dt * a_neg
            cs = jnp.dot(tri[d], la, preferred_element_type=F32, precision=HIGHEST)
            cs_t = cs.T
            cb = _dot_nt(cm.astype(BF16), bm.astype(BF16))
            heads = [d * SSD_HPG + h for h in range(SSD_HPG)]
            dt_mat = _lane_head_expand([dt[:, c:c + 1] for c in heads], gw)
            cs_mat = _lane_head_expand([cs[:, c:c + 1] for c in heads], gw)
            xd = xs * dt_mat
            xd_b = xd.astype(BF16)
            y = jnp.zeros((q, gw), F32)
            for h, c in enumerate(heads):
                diff = cs[:, c:c + 1] - cs_t[c:c + 1, :]
                lmat = jnp.exp(jnp.where(keep[d], diff, NEG_INF))
                y = jnp.where(lane_head == h, _dot((cb * lmat).astype(BF16), xd_b), y)
            st = st_s[...]
            y = y + _dot(cm.astype(BF16), st.astype(BF16)) * jnp.exp(cs_mat)
            end = q - 1 if d == 0 else 0
            cs_end = cs_mat[end:end + 1, :]
            s_new = _dot(bm.T.astype(BF16), (xd * jnp.exp(cs_end - cs_mat)).astype(BF16))
            st_s[...] = st * jnp.exp(cs_end) + s_new
            if d == 0:
                y_s[pl.ds(r0, q), :] = y
            else:
                y_s[pl.ds(r0, q), :] = y_s[pl.ds(r0, q), :] + y
            return carry

        lax.fori_loop(0, nc, chunk_body, 0)

    for ci in range(nc):
        s = ci * q
        if ci < nc_c:
            z = z_c[0, s:s + q, :]
        else:
            z = z_x[0, s - sc:s - sc + q, :]
        y = (y_s[s:s + q, :] + dsk_ref[0] * xs_s[s:s + q, :]) * _silu(z.astype(F32))
        ms = jnp.mean(y * y, axis=-1, keepdims=True)
        out = (y * lax.rsqrt(ms + EPS) * ng_ref[0]).astype(BF16)
        if ci < nc_c:
            y_c[0, s:s + q, :] = out
        else:
            y_x[0, s - sc:s - sc + q, :] = out


def _ssd_mixer(xbc_x, xbc_c, dt_x, dt_c, z_x, z_c, conv_w, conv_b, alog_g, dtb_g, dsk_g, ng_g):
    bsz, sx, _ = xbc_x.shape
    sc = xbc_c.shape[1]
    g, gw, n = SSD_GROUPS, SSD_GW, SSD_STATE
    nb = SSD_INNER // n
    taps = conv_w.shape[0]

    def seq(s, w, off):
        return pl.BlockSpec((1, s, w), lambda b, j, off=off: (b, 0, off + j))

    def par(r, w, off):
        return pl.BlockSpec((r, w), lambda b, j, off=off: (0, off + j))

    def grp(w):
        return pl.BlockSpec((1, 1, w), lambda b, j: (j, 0, 0))

    in_specs = [seq(sx, gw, 0), seq(sx, n, nb), seq(sx, n, nb + g),
                seq(sc, gw, 0), seq(sc, n, nb), seq(sc, n, nb + g),
                seq(sx, LANES, 0), seq(sc, LANES, 0), seq(sx, gw, 0), seq(sc, gw, 0),
                par(taps, gw, 0), par(taps, n, nb), par(taps, n, nb + g),
                par(1, gw, 0), par(1, n, nb), par(1, n, nb + g),
                grp(LANES), grp(LANES), grp(gw), grp(gw)]
    stot = sx + sc
    return pl.pallas_call(
        functools.partial(_ssd_kernel, sx=sx, sc=sc),
        grid=(bsz, g),
        in_specs=in_specs,
        out_specs=[seq(sx, gw, 0), seq(sc, gw, 0)],
        out_shape=[jax.ShapeDtypeStruct((bsz, sx, SSD_INNER), BF16),
                   jax.ShapeDtypeStruct((bsz, sc, SSD_INNER), BF16)],
        scratch_shapes=[pltpu.VMEM((stot, gw), F32), pltpu.VMEM((stot, n), F32), pltpu.VMEM((stot, n), F32),
                        pltpu.VMEM((stot, LANES), F32), pltpu.VMEM((stot, gw), F32), pltpu.VMEM((n, gw), F32)],
        compiler_params=_cparams("arbitrary", "arbitrary"),
        name="ssd_mixer",
    )(xbc_x, xbc_x, xbc_x, xbc_c, xbc_c, xbc_c, dt_x, dt_c, z_x, z_c,
      conv_w, conv_w, conv_w, conv_b, conv_b, conv_b, alog_g, dtb_g, dsk_g, ng_g)


def _lru_kernel(xl_x, xl_c, gt_x, gt_c, cw, cb, wa, wx, ba, bx, lam, y_x, y_c,
                xr_s, a_s, b_s, y_s, *, sx, sc):
    q = SSD_CHUNK
    stot = sx + sc
    w = xr_s.shape[1]
    for src, seg_len, base in ((xl_c, sc, 0), (xl_x, sx, sc)):
        for ci in range(seg_len // q):
            s = ci * q
            xr_s[base + s:base + s + q, :] = _conv_chunk(src, s, seg_len, cw[...], cb[...])

    ng = stot // SUBLANES
    ng_c = sc // SUBLANES
    sub = lax.broadcasted_iota(jnp.int32, (SUBLANES, w), 0)
    rt = q
    for d in range(2):
        nsp = _softplus(-lam[d:d + 1, :])
        for ci in range(stot // rt):
            s = ci * rt
            xr = xr_s[s:s + rt, :]
            xb = xr.astype(BF16)
            r = _sigmoid(_dot(xb, wa[d, 0]) + ba[d:d + 1, :])
            ig = _sigmoid(_dot(xb, wx[d, 0]) + bx[d:d + 1, :])
            a = jnp.exp(-LRU_C * r * nsp)
            a_s[s:s + rt, :] = a
            b_s[s:s + rt, :] = jnp.sqrt(1.0 - a * a) * (ig * xr)

        def group_body(k, carry, d=d):
            if d == 0:
                gi = k
            else:
                gi = jnp.where(k < ng_c, ng_c - 1 - k, ng + ng_c - 1 - k)
            r0 = pl.multiple_of(gi * SUBLANES, SUBLANES)
            a = a_s[pl.ds(r0, SUBLANES), :]
            b = b_s[pl.ds(r0, SUBLANES), :]
            for sh in (1, 2, 4):
                if d == 0:
                    valid = sub >= sh
                    a_sh = pltpu.roll(a, sh, axis=0)
                    b_sh = pltpu.roll(b, sh, axis=0)
                else:
                    valid = sub < SUBLANES - sh
                    a_sh = pltpu.roll(a, SUBLANES - sh, axis=0)
                    b_sh = pltpu.roll(b, SUBLANES - sh, axis=0)
                b = jnp.where(valid, a * b_sh + b, b)
                a = jnp.where(valid, a * a_sh, a)
            h = a * carry + b
            if d == 0:
                y_s[pl.ds(r0, SUBLANES), :] = h
                last = h[SUBLANES - 1:SUBLANES, :]
            else:
                y_s[pl.ds(r0, SUBLANES), :] = y_s[pl.ds(r0, SUBLANES), :] + h
                last = h[0:1, :]
            return jnp.broadcast_to(last, (SUBLANES, w))

        lax.fori_loop(0, ng, group_body, jnp.zeros((SUBLANES, w), F32))

    for ci in range(stot // rt):
        s = ci * rt
        if s < sc:
            gate = gt_c[0, s:s + rt, :]
        else:
            gate = gt_x[0, s - sc:s - sc + rt, :]
        out = (y_s[s:s + rt, :] * _gelu_tanh(gate.astype(F32))).astype(BF16)
        if s < sc:
            y_c[0, s:s + rt, :] = out
        else:
            y_x[0, s - sc:s - sc + rt, :] = out


def _lru_mixer(xl_x, xl_c, gt_x, gt_c, conv_w, conv_b, wa_bd, wx_bd, ba, bx, lam):
    bsz, sx, width = xl_x.shape
    sc = xl_c.shape[1]
    w = LRU_TILE
    taps = conv_w.shape[0]

    def seq(s):
        return pl.BlockSpec((1, s, w), lambda b, j: (b, 0, j))

    def par(r):
        return pl.BlockSpec((r, w), lambda b, j: (0, j))

    wspec = pl.BlockSpec((2, 1, w, w), lambda b, j: (0, j, 0, 0))
    stot = sx + sc
    return pl.pallas_call(
        functools.partial(_lru_kernel, sx=sx, sc=sc),
        grid=(bsz, width // w),
        in_specs=[seq(sx), seq(sc), seq(sx), seq(sc), par(taps), par(1), wspec, wspec, par(2), par(2), par(2)],
        out_specs=[seq(sx), seq(sc)],
        out_shape=[jax.ShapeDtypeStruct((bsz, sx, width), BF16), jax.ShapeDtypeStruct((bsz, sc, width), BF16)],
        scratch_shapes=[pltpu.VMEM((stot, w), F32)] * 4,
        compiler_params=_cparams("arbitrary", "arbitrary"),
        name="lru_mixer",
    )(xl_x, xl_c, gt_x, gt_c, conv_w, conv_b, wa_bd, wx_bd, ba, bx, lam)


def _rpb_table_kernel(rpb_ref, o_ref, *, n_dr, n_dc):
    h = pl.program_id(0)
    w = GRID_W
    qcol = lax.broadcasted_iota(jnp.int32, (w, 2 * w), 0)
    lane = lax.broadcasted_iota(jnp.int32, (w, 2 * w), 1)
    kcol = lane % w
    hi = lane >= w
    rel = kcol - qcol + (NA_KW - 1)
    cstart = jnp.clip(qcol - NA_KW // 2, 0, w - NA_KW)
    in_win = (kcol >= cstart) & (kcol < cstart + NA_KW)
    for d in range(n_dr + 1):
        acc = jnp.full((w, 2 * w), NEG_INF, F32)
        for dc in range(n_dc):
            lo = rpb_ref[(h * n_dr + d - 1) * n_dc + dc] if d >= 1 else NEG_INF
            up = rpb_ref[(h * n_dr + d) * n_dc + dc] if d < n_dr else NEG_INF
            acc = jnp.where(rel == dc, jnp.where(hi, up, lo), acc)
        valid = in_win
        if d == 0:
            valid = valid & hi
        if d == n_dr:
            valid = valid & jnp.logical_not(hi)
        o_ref[0, d] = jnp.where(valid, acc, NEG_INF)


def _rpb_table(rpb):
    nh, n_dr, n_dc = rpb.shape
    return pl.pallas_call(
        functools.partial(_rpb_table_kernel, n_dr=n_dr, n_dc=n_dc),
        grid=(nh,),
        in_specs=[pl.BlockSpec(memory_space=pltpu.SMEM)],
        out_specs=pl.BlockSpec((1, n_dr + 1, GRID_W, 2 * GRID_W), lambda h: (h, 0, 0, 0)),
        out_shape=jax.ShapeDtypeStruct((nh, n_dr + 1, GRID_W, 2 * GRID_W), F32),
        compiler_params=_cparams("arbitrary"),
        name="rpb_table",
    )(rpb.reshape(-1))


def _na_kernel(*refs, sx, sc, want_ctx):
    if want_ctx:
        q_x, k_x, v_x, q_c, k_c, v_c, tab, o_x, o_c = refs
    else:
        q_x, k_x, v_x, k_c, v_c, tab, o_x = refs
    w = GRID_W
    rows = sx // w
    qb = NA_QROWS * w
    kb = NA_KROWS * w
    n_blk = rows // NA_QROWS
    scale = NA_HEAD_DIM ** -0.5
    lane = lax.broadcasted_iota(jnp.int32, (1, 2 * NA_HEAD_DIM), 1)
    in_head = (lane < NA_HEAD_DIM, lane >= NA_HEAD_DIM)
    keyrow = lax.broadcasted_iota(jnp.int32, (1, kb), 1) // w
    kc = k_c[0]
    vc = v_c[0]
    n_tab = tab.shape[1]

    def softmax_pv(parts):
        m = parts[0][0].max(axis=-1, keepdims=True)
        for s, _ in parts[1:]:
            m = jnp.maximum(m, s.max(axis=-1, keepdims=True))
        acc, den = None, None
        for s, v in parts:
            e = jnp.exp(s - m)
            den = e.sum(axis=-1, keepdims=True) if den is None else den + e.sum(axis=-1, keepdims=True)
            pv = _dot(e.astype(BF16), v)
            acc = pv if acc is None else acc + pv
        return acc / den

    def block_body(rb, carry):
        ws = jnp.clip(NA_QROWS * rb - NA_KH // 2, 0, rows - NA_KROWS)
        q0 = pl.multiple_of(rb * qb, qb)
        k0 = pl.multiple_of(ws * w, w)
        qblk = q_x[0, pl.ds(q0, qb), :]
        kwin = k_x[0, pl.ds(k0, kb), :]
        vwin = v_x[0, pl.ds(k0, kb), :]
        out = jnp.zeros((qb, 2 * NA_HEAD_DIM), F32)
        for hh in range(2):
            qm = jnp.where(in_head[hh], qblk, jnp.zeros_like(qblk))
            s_loc = _dot_nt(qm, kwin) * scale
            s_ctx = _dot_nt(qm, kc) * scale
            pieces = []
            for rq in range(NA_QROWS):
                r = NA_QROWS * rb + rq
                rs = jnp.clip(r - NA_KH // 2, 0, rows - NA_KH)
                lo = rs - ws
                valid = (keyrow >= lo) & (keyrow < lo + NA_KH)
                blocks = []
                for ip in range(NA_KROWS // 2):
                    dr_lo = ws + 2 * ip - r + NA_KH - 1
                    blocks.append(tab[hh, jnp.clip(dr_lo + 1, 0, n_tab - 1)])
                bias = jnp.concatenate(blocks, axis=1)
                piece = s_loc[rq * w:(rq + 1) * w, :] + bias
                pieces.append(jnp.where(valid, piece, NEG_INF))
            s_loc = jnp.concatenate(pieces, axis=0)
            o = softmax_pv([(s_loc, vwin), (s_ctx, vc)])
            out = jnp.where(in_head[hh], o, out)
        o_x[0, pl.ds(q0, qb), :] = out.astype(o_x.dtype)
        return carry

    lax.fori_loop(0, n_blk, block_body, 0)

    if want_ctx:
        qc = q_c[0]
        out = jnp.zeros((sc, 2 * NA_HEAD_DIM), F32)
        for hh in range(2):
            qm = jnp.where(in_head[hh], qc, jnp.zeros_like(qc))
            o = softmax_pv([(_dot_nt(qm, kc) * scale, vc)])
            out = jnp.where(in_head[hh], o, out)
        o_c[0] = out.astype(o_c.dtype)


def _na_attention(q_x, k_x, v_x, q_c, k_c, v_c, table, want_ctx):
    bsz, sx, dim = q_x.shape
    sc = k_c.shape[1]
    pw = 2 * NA_HEAD_DIM
    n_pair = dim // pw

    def seq(s):
        return pl.BlockSpec((1, s, pw), lambda p, b: (b, 0, p))

    tspec = pl.BlockSpec((2,) + table.shape[1:], lambda p, b: (p, 0, 0, 0))
    if want_ctx:
        args = (q_x, k_x, v_x, q_c, k_c, v_c, table)
        in_specs = [seq(sx)] * 3 + [seq(sc)] * 3 + [tspec]
        out_specs = [seq(sx), seq(sc)]
        out_shape = [jax.ShapeDtypeStruct((bsz, sx, dim), BF16), jax.ShapeDtypeStruct((bsz, sc, dim), BF16)]
    else:
        args = (q_x, k_x, v_x, k_c, v_c, table)
        in_specs = [seq(sx)] * 3 + [seq(sc)] * 2 + [tspec]
        out_specs = [seq(sx)]
        out_shape = [jax.ShapeDtypeStruct((bsz, sx, dim), BF16)]
    res = pl.pallas_call(
        functools.partial(_na_kernel, sx=sx, sc=sc, want_ctx=want_ctx),
        grid=(n_pair, bsz),
        in_specs=in_specs, out_specs=out_specs, out_shape=out_shape,
        compiler_params=_cparams("arbitrary", "arbitrary"),
        name="na_attention",
    )(*args)
    return res if want_ctx else (res[0], None)


def _top_rows(s, k, exact):
    n, tt = s.shape
    rowid = lax.broadcasted_iota(jnp.int32, (n, tt), 0).astype(F32)
    top_id = lax.broadcasted_iota(jnp.int32, (k, tt), 0)
    work = s
    top = jnp.zeros((k, tt), F32)
    rank = jnp.full((n, tt), float(k), F32)
    for it in range(k):
        m = jnp.max(work, axis=0, keepdims=True)
        sel = work == m
        if exact:
            sel = rowid == jnp.min(jnp.where(sel, rowid, float(n)), axis=0, keepdims=True)
        top = jnp.where(top_id == it, m, top)
        rank = jnp.where(sel, float(it), rank)
        work = jnp.where(sel, -jnp.inf, work)
    n_sel = jnp.sum(jnp.where(rank < float(k), 1.0, 0.0), axis=0, keepdims=True)
    return top, rank, n_sel


def _peer_route_kernel(h_ref, wq_ref, keys_ref, cnt_ref, rk_ref, e0_ref, e1_ref, s_s, top_s, rank_s, pick_s):
    nk, k = PEER_KEYS, PEER_TOPK
    tt = h_ref.shape[1]
    q = _dot(h_ref[0], wq_ref[0]).astype(BF16)
    for z in range(2):
        sz = _dot_nt(keys_ref[z], q[:, z * nk:(z + 1) * nk])
        for t in range(tt // LANES):
            s_s[z, t] = sz[:, t * LANES:(t + 1) * LANES]

    def strip(si, carry):
        cols = pl.ds(pl.multiple_of(si * LANES, LANES), LANES)
        s = [s_s[0, si], s_s[1, si]]

        def stage1(exact):
            dev = []
            for z in range(2):
                top, rank, n_sel = _top_rows(s[z], k, exact)
                top_s[z] = top
                rank_s[z] = rank
                dev.append(jnp.max(jnp.abs(n_sel - float(k))))
            return jnp.maximum(dev[0], dev[1])

        tie1 = stage1(False)

        @pl.when(tie1 > 0.5)
        def _():
            stage1(True)

        t0, t1 = top_s[0], top_s[1]
        cand = jnp.concatenate([t0[0:1] + t1] + [t0[a:a + 1] + t1[0:8] for a in range(1, 8)]
                               + [t0[8:16] + t1[0:1]], axis=0)

        def stage2(exact):
            _, rank, n_sel = _top_rows(cand, k, exact)
            pick_s[...] = jnp.where(rank < float(k), 1.0, 0.0)
            return jnp.max(jnp.abs(n_sel - float(k)))

        tie2 = stage2(False)

        @pl.when(tie2 > 0.5)
        def _():
            stage2(True)

        pf = pick_s[...]
        z_sum = jnp.sum(pf * jnp.exp(cand - cand[0:1]), axis=0, keepdims=True)
        cnts = [jnp.sum(pf[0:k], axis=0, keepdims=True)]
        cnts += [jnp.sum(pf[k + 8 * (a - 1):k + 8 * a], axis=0, keepdims=True) for a in range(1, 8)]
        cnts += [pf[k + 56 + a:k + 57 + a] for a in range(8)]
        rank0 = rank_s[0]
        cnt_i = jnp.zeros_like(rank0)
        for a in range(k):
            cnt_i = jnp.where(rank0 == float(a), cnts[a], cnt_i)
        e0 = jnp.exp(s[0] - t0[0:1]) / z_sum
        for c in range(nk // PEER_ICHUNK):
            cnt_ref[0, c, :, cols] = cnt_i[c * PEER_ICHUNK:(c + 1) * PEER_ICHUNK]
            e0_ref[0, c, :, cols] = e0[c * PEER_ICHUNK:(c + 1) * PEER_ICHUNK]
        rk_ref[0, :, cols] = rank_s[1].astype(rk_ref.dtype)
        e1_ref[0, :, cols] = jnp.exp(s[1] - t1[0:1]).astype(e1_ref.dtype)
        return carry

    lax.fori_loop(0, tt // LANES, strip, 0)


def _peer_route(hx, w_q, keys, tt):
    bn, sn, d = hx.shape
    nk, nh, ic = PEER_KEYS, PEER_HEADS, PEER_ICHUNK
    spec_i = pl.BlockSpec((1, nk // ic, ic, tt), lambda b, i, h: (b, 0, h, i))
    spec_j = pl.BlockSpec((1, nk, tt), lambda b, i, h: (b, h, i))
    shape_i = jax.ShapeDtypeStruct((bn, nk // ic, nh * ic, sn), F32)
    shape_j = jax.ShapeDtypeStruct((bn, nh * nk, sn), BF16)
    n_cand = PEER_TOPK + 8 * 8
    return pl.pallas_call(
        _peer_route_kernel,
        grid=(bn, sn // tt, nh),
        in_specs=[pl.BlockSpec((1, tt, d), lambda b, i, h: (b, i, 0)),
                  pl.BlockSpec((1, d, 2 * nk), lambda b, i, h: (h, 0, 0)),
                  pl.BlockSpec((2, nk, keys.shape[2]), lambda b, i, h: (h, 0, 0))],
        out_specs=[spec_i, spec_j, spec_i, spec_j],
        out_shape=[shape_i, shape_j, shape_i, shape_j],
        scratch_shapes=[pltpu.VMEM((2, tt // LANES, nk, LANES), F32), pltpu.VMEM((2, PEER_TOPK, LANES), F32),
                        pltpu.VMEM((2, nk, LANES), F32), pltpu.VMEM((n_cand, LANES), F32)],
        compiler_params=_cparams("arbitrary", "arbitrary", "arbitrary"),
        name="peer_route",
    )(hx, w_q, keys)


def _peer_dense_kernel(h_ref, cnt_ref, rk_ref, e0_ref, e1_ref, u_ref, vt_ref, x_ref, g_ref, o_ref,
                       acc_ref, act_ref, p_ref, hx_s, rk_s, e1_s, cnt_s, e0_s):
    nk, ic = PEER_KEYS, PEER_ICHUNK
    ck = pl.program_id(2)
    n_slab = act_ref.shape[0]
    sw = min(PEER_STRIP, n_slab * LANES)
    per = sw // LANES
    n_strip = n_slab // per
    il_group, j_group = 4, 4

    @pl.when(ck == 0)
    def _():
        acc_ref[...] = jnp.zeros_like(acc_ref)
        hx_s[...] = h_ref[0]
        for t in range(n_slab):
            rk_s[t] = rk_ref[0, :, t * LANES:(t + 1) * LANES].astype(rk_s.dtype)
            e1_s[t] = e1_ref[0, :, t * LANES:(t + 1) * LANES].astype(e1_s.dtype)

    for t in range(n_slab):
        cnt_s[t] = cnt_ref[0, 0, :, t * LANES:(t + 1) * LANES]
        e0_s[t] = e0_ref[0, 0, :, t * LANES:(t + 1) * LANES]

    def activations(s):
        r0 = pl.multiple_of(s * sw, sw)
        a = _gelu_tanh(_dot_nt(u_ref[...], hx_s[pl.ds(r0, sw), :])).astype(act_ref.dtype)
        for k in range(per):
            act_ref[s * per + k] = a[:, k * LANES:(k + 1) * LANES]

    def gate_weights(t):
        jr = nk // j_group
        for ig in range(ic // il_group):
            for jg in range(j_group):
                wgt = [None] * il_group
                for h in range(PEER_HEADS):
                    cnt8 = cnt_s[t, h * ic:(h + 1) * ic, :]
                    e08 = e0_s[t, h * ic:(h + 1) * ic, :]
                    rk = rk_s[t, h * nk + jg * jr:h * nk + (jg + 1) * jr, :]
                    e1 = e1_s[t, h * nk + jg * jr:h * nk + (jg + 1) * jr, :]
                    for g in range(il_group):
                        il = ig * il_group + g
                        term = jnp.where(rk < cnt8[il:il + 1], e1 * e08[il:il + 1], 0.0)
                        wgt[g] = term if wgt[g] is None else wgt[g] + term
                for g in range(il_group):
                    r0 = (ig * il_group + g) * nk + jg * jr
                    p_ref[t, r0:r0 + jr, :] = (wgt[g] * act_ref[t, r0:r0 + jr, :]).astype(p_ref.dtype)

    def combine(s):
        for k in range(per):
            gate_weights(s * per + k)
        y = _dot(vt_ref[...], jnp.concatenate([p_ref[s * per + k] for k in range(per)], axis=1))
        for k in range(per):
            acc_ref[s * per + k] += y[:, k * LANES:(k + 1) * LANES]

    activations(0)

    def strip(s, carry):
        activations(s + 1)
        combine(s)
        return carry

    lax.fori_loop(0, n_strip - 1, strip, 0)
    combine(n_strip - 1)

    @pl.when(ck == pl.num_programs(2) - 1)
    def _():
        for t in range(n_slab):
            rows = slice(t * LANES, (t + 1) * LANES)
            o_ref[0, rows, :] = x_ref[0, rows, :] + g_ref[0] * acc_ref[t].T


PEER_STRIP = 256


def _peer_dense(hx, route, u, v_t, x, gate, tt):
    bn, sn, d = hx.shape
    nk, nh, ic = PEER_KEYS, PEER_HEADS, PEER_ICHUNK
    ne = ic * nk
    n_chunk = u.shape[0] // ne
    spec_i = pl.BlockSpec((1, 1, nh * ic, tt), lambda b, i, c: (b, c, 0, i))
    spec_j = pl.BlockSpec((1, nh * nk, tt), lambda b, i, c: (b, 0, i))
    tok = pl.BlockSpec((1, tt, d), lambda b, i, c: (b, i, 0))
    n_slab = tt // LANES
    return pl.pallas_call(
        _peer_dense_kernel,
        grid=(bn, sn // tt, n_chunk),
        in_specs=[tok, spec_i, spec_j, spec_i, spec_j,
                  pl.BlockSpec((ne, d), lambda b, i, c: (c, 0)),
                  pl.BlockSpec((d, ne), lambda b, i, c: (0, c)),
                  tok,
                  pl.BlockSpec((1, 1, d), lambda b, i, c: (b, 0, 0))],
        out_specs=tok,
        out_shape=jax.ShapeDtypeStruct((bn, sn, d), F32),
        scratch_shapes=[pltpu.VMEM((n_slab, d, LANES), F32), pltpu.VMEM((n_slab, ne, LANES), BF16),
                        pltpu.VMEM((n_slab, ne, LANES), BF16), pltpu.VMEM((tt, d), BF16),
                        pltpu.VMEM((n_slab, nh * nk, LANES), F32), pltpu.VMEM((n_slab, nh * nk, LANES), F32),
                        pltpu.VMEM((n_slab, nh * ic, LANES), F32), pltpu.VMEM((n_slab, nh * ic, LANES), F32)],
        compiler_params=_cparams("arbitrary", "arbitrary", "arbitrary"),
        name="peer_dense",
    )(hx, *route, u, v_t, x, gate)


def _peer(hx, x, gate, w_q, keys, u, v_t, tt=1024):
    tt = min(tt, hx.shape[1])
    route = _peer_route(hx, w_q, keys, tt)
    return _peer_dense(hx, route, u, v_t, x, gate, tt)


def _group_lanes(p, width):
    g = p.reshape(2, SSD_GROUPS, SSD_HPG).transpose(1, 0, 2).reshape(SSD_GROUPS, 2 * SSD_HPG)
    return jnp.pad(g, ((0, 0), (0, width - 2 * SSD_HPG))).reshape(SSD_GROUPS, 1, width)


def _block_diag(w, tile):
    two, nb, bd, _ = w.shape
    per = tile // bd
    w = w.reshape(two, nb // per, per, bd, bd)
    eye = jnp.eye(per, dtype=w.dtype)
    return jnp.einsum("dtpij,pq->dtpiqj", w, eye).reshape(two, nb // per, tile, tile)


def kernel(x, c, ctx, c_ctx, ada_w, ada_b, norm1_g, norm2_g, ev_w_in, ev_conv_w, ev_conv_b, ev_a_log,
           ev_dt_bias, ev_d, ev_ssd_norm_g, ev_lru_conv_w, ev_lru_conv_b, ev_lru_wa, ev_lru_ba, ev_lru_wx,
           ev_lru_bx, ev_lru_lam, ev_w_out, od_w_qkv, od_q_norm_g, od_k_norm_g, od_rpb, od_w_o,
           pe_w_q, pe_keys, pe_u, pe_v):
    bsz, sx, d = x.shape
    sc = ctx.shape[1]
    depth = ada_w.shape[0]

    n_c = bsz + 1
    rows = -(-n_c // SUBLANES) * SUBLANES
    c_all = jnp.concatenate([c, c_ctx[None], jnp.zeros((rows - n_c, d), F32)], axis=0)
    mods = _ada_mods(c_all, ada_w, ada_b).reshape(depth, rows, 6, d)

    ctx = ctx.reshape(1, bsz * sc, d)

    def per_batch(t):
        return t.reshape(bsz, sc, t.shape[-1])

    for layer in range(depth):
        last = layer == depth - 1
        j = layer // 2
        mod_x = mods[layer, :bsz]
        mod_c = mods[layer, bsz:bsz + 1]
        g1 = norm1_g[layer][None]
        g2 = norm2_g[layer][None]
        want_ctx = not last

        if layer % 2 == 0:
            w_in = ev_w_in[j]
            o_dt, o_xl = SSD_XBC, SSD_XBC + 2 * SSD_HEADS
            o_z = o_xl + LRU_WIDTH
            o_gate = o_z + SSD_INNER
            w_dt = w_in[:, o_dt:o_xl].reshape(d, 2, SSD_GROUPS, SSD_HPG).transpose(0, 2, 1, 3)
            w_dt = jnp.pad(w_dt.reshape(d, SSD_GROUPS, 2 * SSD_HPG), ((0, 0), (0, 0), (0, LANES - 2 * SSD_HPG)))
            ws = [w_in[:, :o_dt].astype(BF16), w_dt.reshape(d, SSD_GROUPS * LANES).astype(BF16),
                  w_in[:, o_xl:o_z].astype(BF16), w_in[:, o_z:o_gate].astype(BF16), w_in[:, o_gate:].astype(BF16)]
            dts = [BF16, F32, BF16, BF16, BF16]
            px = _nm_linear(x, g1, mod_x[:, 0:2], ws, dts)
            pc = [per_batch(t) for t in _nm_linear(ctx, g1, mod_c[:, 0:2], ws, dts)]
            y_ssd_x, y_ssd_c = _ssd_mixer(
                px[0], pc[0], px[1], pc[1], px[3], pc[3], ev_conv_w[j], ev_conv_b[j][None],
                _group_lanes(ev_a_log[j], LANES), _group_lanes(ev_dt_bias[j], LANES),
                jnp.repeat(ev_d[j], SSD_HEAD_DIM).reshape(SSD_GROUPS, 1, SSD_GW),
                ev_ssd_norm_g[j].reshape(SSD_GROUPS, 1, SSD_GW))
            y_lru_x, y_lru_c = _lru_mixer(
                px[2], pc[2], px[4], pc[4], ev_lru_conv_w[j], ev_lru_conv_b[j][None],
                _block_diag(ev_lru_wa[j], LRU_TILE).astype(BF16), _block_diag(ev_lru_wx[j], LRU_TILE).astype(BF16),
                ev_lru_ba[j], ev_lru_bx[j], ev_lru_lam[j])
            w_out = ev_w_out[j].astype(BF16)
            w_outs = [w_out[:SSD_INNER], w_out[SSD_INNER:]]
            acts_x = [y_ssd_x, y_lru_x]
            acts_c = [y_ssd_c.reshape(1, bsz * sc, -1), y_lru_c.reshape(1, bsz * sc, -1)]
        else:
            w_qkv = od_w_qkv[j].astype(BF16)
            nd = w_qkv.shape[1] // 3
            ws = [w_qkv[:, :nd], w_qkv[:, nd:2 * nd], w_qkv[:, 2 * nd:]]
            gains = [jnp.tile(od_q_norm_g[j], NA_HEADS)[None], jnp.tile(od_k_norm_g[j], NA_HEADS)[None], None]
            q_x, k_x, v_x = _nm_linear(x, g1, mod_x[:, 0:2], ws, [BF16] * 3, gains)
            q_c, k_c, v_c = [per_batch(t) for t in _nm_linear(ctx, g1, mod_c[:, 0:2], ws, [BF16] * 3, gains)]
            table = _rpb_table(od_rpb[j])
            o_x, o_c = _na_attention(q_x, k_x, v_x, q_c, k_c, v_c, table, want_ctx)
            w_outs = [od_w_o[j].astype(BF16)]
            acts_x = [o_x]
            acts_c = [o_c.reshape(1, bsz * sc, -1)] if want_ctx else None

        w_q = pe_w_q[layer].reshape(d, PEER_HEADS, 2 * PEER_KEYS).transpose(1, 0, 2).astype(BF16)
        keys = pe_keys[layer].reshape(2 * PEER_HEADS, PEER_KEYS, -1).astype(BF16)
        u = pe_u[layer].astype(BF16)
        v_t = pe_v[layer].T.astype(BF16)

        x, hx = _out_linear(acts_x, w_outs, x, mod_x[:, 2:5], g2)
        x = _peer(hx, x, mod_x[:, 5:6], w_q, keys, u, v_t)
        if want_ctx:
            ctx, hc = _out_linear(acts_c, w_outs, ctx, mod_c[:, 2:5], g2)
            ctx = _peer(hc, ctx, mod_c[:, 5:6], w_q, keys, u, v_t)
    return x
```

```python
import functools
import math

import jax
import jax.numpy as jnp
from jax import lax
from jax.experimental import pallas as pl
from jax.experimental.pallas import tpu as pltpu

F32 = jnp.float32
BF16 = jnp.bfloat16
HIGHEST = lax.Precision.HIGHEST

EPS = 1e-6
NEG_INF = -1e30

GRID_W = 64
SSD_HEADS = 16
SSD_HEAD_DIM = 64
SSD_GROUPS = 4
SSD_HPG = SSD_HEADS // SSD_GROUPS
SSD_STATE = 128
SSD_CHUNK = 128
SSD_INNER = SSD_HEADS * SSD_HEAD_DIM
SSD_GW = SSD_INNER // SSD_GROUPS
SSD_XBC = SSD_INNER + 2 * SSD_GROUPS * SSD_STATE
LRU_WIDTH = 1024
LRU_BLOCKS = 16
LRU_BLOCK_DIM = LRU_WIDTH // LRU_BLOCKS
LRU_C = 8.0
LRU_TILE = 256
NA_HEADS = 16
NA_HEAD_DIM = 64
NA_KH = 8
NA_KW = 16
NA_QROWS = 4
NA_KROWS = 12
PEER_HEADS = 8
PEER_KEYS = 128
PEER_TOPK = 16
PEER_ICHUNK = 8

LANES = 128
SUBLANES = 8
VMEM_LIMIT_BYTES = 56 * 1024 * 1024


def _cparams(*sem):
    return pltpu.CompilerParams(dimension_semantics=sem, vmem_limit_bytes=VMEM_LIMIT_BYTES)


def _silu(x):
    return x * (1.0 / (1.0 + jnp.exp(-x)))


def _sigmoid(x):
    return 1.0 / (1.0 + jnp.exp(-x))


def _softplus(x):
    return jnp.maximum(x, 0.0) + jnp.log(1.0 + jnp.exp(-jnp.abs(x)))


def _gelu_tanh(x):
    k0 = -2.0 * math.sqrt(2.0 / math.pi)
    return x / (1.0 + jnp.exp(x * (k0 + (k0 * 0.044715) * (x * x))))


def _rms_mod(x, g, shift, scale):
    ms = jnp.mean(x * x, axis=-1, keepdims=True)
    y = x * lax.rsqrt(ms + EPS) * g
    return y * (1.0 + scale) + shift


def _dot(a, b):
    return jnp.dot(a, b, preferred_element_type=F32)


def _dot_nt(a, b):
    return lax.dot_general(a, b, (((1,), (1,)), ((), ())), preferred_element_type=F32)


def _ada_kernel(c_ref, w_ref, b_ref, o_ref):
    s = _silu(c_ref[...])
    o_ref[0] = jnp.dot(s, w_ref[0], preferred_element_type=F32, precision=HIGHEST) + b_ref[0]


def _ada_mods(c_all, ada_w, ada_b):
    depth, d, n = ada_w.shape
    rows = c_all.shape[0]
    tn = 1536
    return pl.pallas_call(
        _ada_kernel,
        grid=(depth, n // tn),
        in_specs=[pl.BlockSpec((rows, d), lambda l, j: (0, 0)),
                  pl.BlockSpec((1, d, tn), lambda l, j: (l, 0, j)),
                  pl.BlockSpec((1, 1, tn), lambda l, j: (l, 0, j))],
        out_specs=pl.BlockSpec((1, rows, tn), lambda l, j: (l, 0, j)),
        out_shape=jax.ShapeDtypeStruct((depth, rows, n), F32),
        compiler_params=_cparams("arbitrary", "arbitrary"),
        name="ada_mods",
    )(c_all, ada_w, ada_b.reshape(depth, 1, n))


def _head_block_ones(n):
    r = lax.broadcasted_iota(jnp.int32, (n, n), 0) // NA_HEAD_DIM
    c = lax.broadcasted_iota(jnp.int32, (n, n), 1) // NA_HEAD_DIM
    return (r == c).astype(F32)


def _nm_linear_kernel(*refs, n_out, head_norm, tn):
    x_ref, g_ref, mod_ref = refs[:3]
    w_refs = refs[3:3 + n_out]
    hg_refs = refs[3 + n_out:3 + n_out + sum(head_norm)]
    o_refs = refs[3 + n_out + sum(head_norm):]
    h = _rms_mod(x_ref[0], g_ref[...], mod_ref[0, 0:1, :], mod_ref[0, 1:2, :]).astype(BF16)
    hg_i = 0
    for w_ref, o_ref, hn in zip(w_refs, o_refs, head_norm):
        n = w_ref.shape[1]
        for j in range(n // tn):
            y = _dot(h, w_ref[:, j * tn:(j + 1) * tn])
            if hn:
                ss = jnp.dot(y * y, _head_block_ones(tn), preferred_element_type=F32, precision=HIGHEST)
                y = y * lax.rsqrt(ss * (1.0 / NA_HEAD_DIM) + EPS) * hg_refs[hg_i][:, j * tn:(j + 1) * tn]
            o_ref[0, :, j * tn:(j + 1) * tn] = y.astype(o_ref.dtype)
        hg_i += hn


def _nm_linear(x, g, mod, ws, out_dtypes, head_gains=None, tm=512, tn=256):
    bn, sn, d = x.shape
    tm = min(tm, sn)
    n_out = len(ws)
    head_gains = head_gains or [None] * n_out
    head_norm = tuple(hg is not None for hg in head_gains)
    hgs = [hg for hg in head_gains if hg is not None]
    in_specs = [pl.BlockSpec((1, tm, d), lambda b, i: (b, i, 0)),
                pl.BlockSpec((1, d), lambda b, i: (0, 0)),
                pl.BlockSpec((1, 2, d), lambda b, i: (b, 0, 0))]
    in_specs += [pl.BlockSpec(w.shape, lambda b, i: (0, 0)) for w in ws]
    in_specs += [pl.BlockSpec(hg.shape, lambda b, i: (0, 0)) for hg in hgs]
    out_specs = [pl.BlockSpec((1, tm, w.shape[1]), lambda b, i: (b, i, 0)) for w in ws]
    out_shape = [jax.ShapeDtypeStruct((bn, sn, w.shape[1]), dt) for w, dt in zip(ws, out_dtypes)]
    return pl.pallas_call(
        functools.partial(_nm_linear_kernel, n_out=n_out, head_norm=head_norm, tn=tn),
        grid=(bn, sn // tm),
        in_specs=in_specs, out_specs=out_specs, out_shape=out_shape,
        compiler_params=_cparams("arbitrary", "arbitrary"),
        name="nm_linear",
    )(x, g, mod, *ws, *hgs)


def _out_linear_kernel(*refs, n_in):
    a_refs = refs[:n_in]
    w_refs = refs[n_in:2 * n_in]
    x_ref, mod_ref, g_ref, xo_ref, ho_ref = refs[2 * n_in:]
    y = _dot(a_refs[0][0], w_refs[0][...])
    for a_ref, w_ref in zip(a_refs[1:], w_refs[1:]):
        y = y + _dot(a_ref[0], w_ref[...])
    xn = x_ref[0] + mod_ref[0, 0:1, :] * y
    xo_ref[0] = xn
    ho_ref[0] = _rms_mod(xn, g_ref[...], mod_ref[0, 1:2, :], mod_ref[0, 2:3, :]).astype(BF16)


def _out_linear(acts, ws, x, mod, g, tm=512):
    bn, sn, d = x.shape
    tm = min(tm, sn)
    n_in = len(acts)
    in_specs = [pl.BlockSpec((1, tm, a.shape[2]), lambda b, i: (b, i, 0)) for a in acts]
    in_specs += [pl.BlockSpec(w.shape, lambda b, i: (0, 0)) for w in ws]
    in_specs += [pl.BlockSpec((1, tm, d), lambda b, i: (b, i, 0)),
                 pl.BlockSpec((1, 3, d), lambda b, i: (b, 0, 0)),
                 pl.BlockSpec((1, d), lambda b, i: (0, 0))]
    return pl.pallas_call(
        functools.partial(_out_linear_kernel, n_in=n_in),
        grid=(bn, sn // tm),
        in_specs=in_specs,
        out_specs=[pl.BlockSpec((1, tm, d), lambda b, i: (b, i, 0))] * 2,
        out_shape=[jax.ShapeDtypeStruct((bn, sn, d), F32), jax.ShapeDtypeStruct((bn, sn, d), BF16)],
        compiler_params=_cparams("arbitrary", "arbitrary"),
        name="out_linear",
    )(*acts, *ws, x, mod, g)


CONV_HALO = 16


def _conv_chunk(src_ref, s, seg_len, w, bias, rows=SSD_CHUNK):
    ncol = src_ref.shape[2]
    if s > 0:
        prev = src_ref[0, s - CONV_HALO:s, :].astype(F32)
    else:
        prev = jnp.zeros((CONV_HALO, ncol), F32)
    cur = src_ref[0, s:s + rows, :].astype(F32)
    if s + rows < seg_len:
        nxt = src_ref[0, s + rows:s + rows + CONV_HALO, :].astype(F32)
    else:
        nxt = jnp.zeros((CONV_HALO, ncol), F32)
    win = jnp.concatenate([prev, cur, nxt], axis=0)
    taps = w.shape[0]
    acc = bias
    for k in range(taps):
        off = CONV_HALO - taps // 2 + k
        acc = acc + win[off:off + rows, :] * w[k:k + 1, :]
    return acc


def _lane_head_expand(cols, width):
    nh = len(cols)
    hd = width // nh
    rows = cols[0].shape[0]
    lane_head = lax.broadcasted_iota(jnp.int32, (rows, width), 1) // hd
    out = jnp.broadcast_to(cols[nh - 1], (rows, width))
    for h in range(nh - 2, -1, -1):
        out = jnp.where(lane_head == h, jnp.broadcast_to(cols[h], (rows, width)), out)
    return out


def _ssd_kernel(xs_x, bm_x, cm_x, xs_c, bm_c, cm_c, dt_x, dt_c, z_x, z_c,
                cw_xs, cw_b, cw_c, cb_xs, cb_b, cb_c, alog_ref, dtb_ref, dsk_ref, ng_ref,
                y_x, y_c,
                xs_s, bm_s, cm_s, dt_s, y_s, st_s, cs_s, cst_s, *, sx, sc):
    q = SSD_CHUNK
    nc_c, nc_x = sc // q, sx // q
    nc = nc_c + nc_x
    gw = xs_s.shape[1]

    for seg_ref3, seg_len, base in (((xs_c, bm_c, cm_c), sc, 0), ((xs_x, bm_x, cm_x), sx, sc)):
        for ci in range(seg_len // q):
            s = ci * q
            for src, dst, w_ref, b_ref in zip(seg_ref3, (xs_s, bm_s, cm_s), (cw_xs, cw_b, cw_c),
                                              (cb_xs, cb_b, cb_c)):
                dst[base + s:base + s + q, :] = _silu(_conv_chunk(src, s, seg_len, w_ref[...], b_ref[...]))
    dt_s[0:sc, :] = _softplus(dt_c[0] + dtb_ref[0])
    dt_s[sc:sc + sx, :] = _softplus(dt_x[0] + dtb_ref[0])

    a_neg = -jnp.exp(alog_ref[0])
    row = lax.broadcasted_iota(jnp.int32, (q, q), 0)
    col = lax.broadcasted_iota(jnp.int32, (q, q), 1)
    tri = ((col <= row).astype(F32), (col >= row).astype(F32))
    keep = (col <= row, col >= row)
    lane_head = lax.broadcasted_iota(jnp.int32, (q, gw), 1) // SSD_HEAD_DIM

    st_s[...] = jnp.zeros_like(st_s)
    y_s[...] = jnp.zeros_like(y_s)

    for ci in range(nc):
        la = dt_s[ci * q:(ci + 1) * q, :] * a_neg
        for d in range(2):
            cs = jnp.dot(tri[d], la, preferred_element_type=F32, precision=HIGHEST)
            cs_s[d, ci * q:(ci + 1) * q, :] = cs
            cst_s[d, ci * q:(ci + 1) * q, :] = cs.T

    def chunk_body(i, carry):
        for d in range(2):
            if d == 0:
                ci = i
            else:
                ci = jnp.where(i < nc_c, nc_c - 1 - i, nc + nc_c - 1 - i)
            r0 = pl.multiple_of(ci * q, q)
            xs = xs_s[pl.ds(r0, q), :]
            bm = bm_s[pl.ds(r0, q), :]
            cm = cm_s[pl.ds(r0, q), :]
            dt = dt_s[pl.ds(r0, q), :]
            cs = cs_s[d, pl.ds(r0, q), :]
            cs_t = cst_s[d, pl.ds(r0, q), :]
            cb = _dot_nt(cm.astype(BF16), bm.astype(BF16))
            heads = [d * SSD_HPG + h for h in range(SSD_HPG)]
            dt_mat = _lane_head_expand([dt[:, c:c + 1] for c in heads], gw)
            cs_mat = _lane_head_expand([cs[:, c:c + 1] for c in heads], gw)
            xd = xs * dt_mat
            xd_b = xd.astype(BF16)
            y = jnp.zeros((q, gw), F32)
            for h, c in enumerate(heads):
                diff = cs[:, c:c + 1] - cs_t[c:c + 1, :]
                lmat = jnp.exp(jnp.where(keep[d], diff, NEG_INF))
                y = jnp.where(lane_head == h, _dot((cb * lmat).astype(BF16), xd_b), y)
            st = st_s[d]
            y = y + _dot(cm.astype(BF16), st.astype(BF16)) * jnp.exp(cs_mat)
            end = q - 1 if d == 0 else 0
            cs_end = cs_mat[end:end + 1, :]
            s_new = _dot(bm.T.astype(BF16), (xd * jnp.exp(cs_end - cs_mat)).astype(BF16))
            st_s[d] = st * jnp.exp(cs_end) + s_new
            y_s[pl.ds(r0, q), :] = y_s[pl.ds(r0, q), :] + y
        return carry

    lax.fori_loop(0, nc, chunk_body, 0)

    for ci in range(nc):
        s = ci * q
        if ci < nc_c:
            z = z_c[0, s:s + q, :]
        else:
            z = z_x[0, s - sc:s - sc + q, :]
        y = (y_s[s:s + q, :] + dsk_ref[0] * xs_s[s:s + q, :]) * _silu(z.astype(F32))
        ms = jnp.mean(y * y, axis=-1, keepdims=True)
        out = (y * lax.rsqrt(ms + EPS) * ng_ref[0]).astype(BF16)
        if ci < nc_c:
            y_c[0, s:s + q, :] = out
        else:
            y_x[0, s - sc:s - sc + q, :] = out


def _ssd_mixer(xbc_x, xbc_c, dt_x, dt_c, z_x, z_c, conv_w, conv_b, alog_g, dtb_g, dsk_g, ng_g):
    bsz, sx, _ = xbc_x.shape
    sc = xbc_c.shape[1]
    g, gw, n = SSD_GROUPS, SSD_GW, SSD_STATE
    nb = SSD_INNER // n
    taps = conv_w.shape[0]

    def seq(s, w, off):
        return pl.BlockSpec((1, s, w), lambda b, j, off=off: (b, 0, off + j))

    def par(r, w, off):
        return pl.BlockSpec((r, w), lambda b, j, off=off: (0, off + j))

    def grp(w):
        return pl.BlockSpec((1, 1, w), lambda b, j: (j, 0, 0))

    in_specs = [seq(sx, gw, 0), seq(sx, n, nb), seq(sx, n, nb + g),
                seq(sc, gw, 0), seq(sc, n, nb), seq(sc, n, nb + g),
                seq(sx, LANES, 0), seq(sc, LANES, 0), seq(sx, gw, 0), seq(sc, gw, 0),
                par(taps, gw, 0), par(taps, n, nb), par(taps, n, nb + g),
                par(1, gw, 0), par(1, n, nb), par(1, n, nb + g),
                grp(LANES), grp(LANES), grp(gw), grp(gw)]
    stot = sx + sc
    return pl.pallas_call(
        functools.partial(_ssd_kernel, sx=sx, sc=sc),
        grid=(bsz, g),
        in_specs=in_specs,
        out_specs=[seq(sx, gw, 0), seq(sc, gw, 0)],
        out_shape=[jax.ShapeDtypeStruct((bsz, sx, SSD_INNER), BF16),
                   jax.ShapeDtypeStruct((bsz, sc, SSD_INNER), BF16)],
        scratch_shapes=[pltpu.VMEM((stot, gw), F32), pltpu.VMEM((stot, n), F32), pltpu.VMEM((stot, n), F32),
                        pltpu.VMEM((stot, LANES), F32), pltpu.VMEM((stot, gw), F32), pltpu.VMEM((2, n, gw), F32),
                        pltpu.VMEM((2, stot, LANES), F32), pltpu.VMEM((2, stot, LANES), F32)],
        compiler_params=_cparams("arbitrary", "arbitrary"),
        name="ssd_mixer",
    )(xbc_x, xbc_x, xbc_x, xbc_c, xbc_c, xbc_c, dt_x, dt_c, z_x, z_c,
      conv_w, conv_w, conv_w, conv_b, conv_b, conv_b, alog_g, dtb_g, dsk_g, ng_g)


def _lru_kernel(xl_x, xl_c, gt_x, gt_c, cw, cb, wa, wx, ba, bx, lam, y_x, y_c,
                xr_s, a_s, b_s, y_s, *, sx, sc):
    q = SSD_CHUNK
    stot = sx + sc
    w = xr_s.shape[1]
    for src, seg_len, base in ((xl_c, sc, 0), (xl_x, sx, sc)):
        for ci in range(seg_len // q):
            s = ci * q
            xr_s[base + s:base + s + q, :] = _conv_chunk(src, s, seg_len, cw[...], cb[...])

    ng = stot // SUBLANES
    ng_c = sc // SUBLANES
    sub = lax.broadcasted_iota(jnp.int32, (SUBLANES, w), 0)
    rt = q
    for d in range(2):
        nsp = _softplus(-lam[d:d + 1, :])
        for ci in range(stot // rt):
            s = ci * rt
            xr = xr_s[s:s + rt, :]
            xb = xr.astype(BF16)
            r = _sigmoid(_dot(xb, wa[d, 0]) + ba[d:d + 1, :])
            ig = _sigmoid(_dot(xb, wx[d, 0]) + bx[d:d + 1, :])
            a = jnp.exp(-LRU_C * r * nsp)
            a_s[d, s:s + rt, :] = a
            b_s[d, s:s + rt, :] = jnp.sqrt(1.0 - a * a) * (ig * xr)

    y_s[...] = jnp.zeros_like(y_s)

    def group_body(k, carries):
        new = []
        for d in range(2):
            if d == 0:
                gi = k
            else:
                gi = jnp.where(k < ng_c, ng_c - 1 - k, ng + ng_c - 1 - k)
            r0 = pl.multiple_of(gi * SUBLANES, SUBLANES)
            a = a_s[d, pl.ds(r0, SUBLANES), :]
            b = b_s[d, pl.ds(r0, SUBLANES), :]
            for sh in (1, 2, 4):
                if d == 0:
                    valid = sub >= sh
                    a_sh = pltpu.roll(a, sh, axis=0)
                    b_sh = pltpu.roll(b, sh, axis=0)
                else:
                    valid = sub < SUBLANES - sh
                    a_sh = pltpu.roll(a, SUBLANES - sh, axis=0)
                    b_sh = pltpu.roll(b, SUBLANES - sh, axis=0)
                b = jnp.where(valid, a * b_sh + b, b)
                a = jnp.where(valid, a * a_sh, a)
            h = a * carries[d] + b
            y_s[pl.ds(r0, SUBLANES), :] = y_s[pl.ds(r0, SUBLANES), :] + h
            last = h[SUBLANES - 1:SUBLANES, :] if d == 0 else h[0:1, :]
            new.append(jnp.broadcast_to(last, (SUBLANES, w)))
        return tuple(new)

    zero = jnp.zeros((SUBLANES, w), F32)
    lax.fori_loop(0, ng, group_body, (zero, zero))

    for ci in range(stot // rt):
        s = ci * rt
        if s < sc:
            gate = gt_c[0, s:s + rt, :]
        else:
            gate = gt_x[0, s - sc:s - sc + rt, :]
        out = (y_s[s:s + rt, :] * _gelu_tanh(gate.astype(F32))).astype(BF16)
        if s < sc:
            y_c[0, s:s + rt, :] = out
        else:
            y_x[0, s - sc:s - sc + rt, :] = out


def _lru_mixer(xl_x, xl_c, gt_x, gt_c, conv_w, conv_b, wa_bd, wx_bd, ba, bx, lam):
    bsz, sx, width = xl_x.shape
    sc = xl_c.shape[1]
    w = LRU_TILE
    taps = conv_w.shape[0]

    def seq(s):
        return pl.BlockSpec((1, s, w), lambda b, j: (b, 0, j))

    def par(r):
        return pl.BlockSpec((r, w), lambda b, j: (0, j))

    wspec = pl.BlockSpec((2, 1, w, w), lambda b, j: (0, j, 0, 0))
    stot = sx + sc
    return pl.pallas_call(
        functools.partial(_lru_kernel, sx=sx, sc=sc),
        grid=(bsz, width // w),
        in_specs=[seq(sx), seq(sc), seq(sx), seq(sc), par(taps), par(1), wspec, wspec, par(2), par(2), par(2)],
        out_specs=[seq(sx), seq(sc)],
        out_shape=[jax.ShapeDtypeStruct((bsz, sx, width), BF16), jax.ShapeDtypeStruct((bsz, sc, width), BF16)],
        scratch_shapes=[pltpu.VMEM((stot, w), F32), pltpu.VMEM((2, stot, w), F32), pltpu.VMEM((2, stot, w), F32),
                        pltpu.VMEM((stot, w), F32)],
        compiler_params=_cparams("arbitrary", "arbitrary"),
        name="lru_mixer",
    )(xl_x, xl_c, gt_x, gt_c, conv_w, conv_b, wa_bd, wx_bd, ba, bx, lam)


def _rpb_table_kernel(rpb_ref, o_ref, *, n_dr, n_dc):
    h = pl.program_id(0)
    w = GRID_W
    qcol = lax.broadcasted_iota(jnp.int32, (w, 2 * w), 0)
    lane = lax.broadcasted_iota(jnp.int32, (w, 2 * w), 1)
    kcol = lane % w
    hi = lane >= w
    rel = kcol - qcol + (NA_KW - 1)
    cstart = jnp.clip(qcol - NA_KW // 2, 0, w - NA_KW)
    in_win = (kcol >= cstart) & (kcol < cstart + NA_KW)
    for d in range(n_dr + 1):
        acc = jnp.full((w, 2 * w), NEG_INF, F32)
        for dc in range(n_dc):
            lo = rpb_ref[(h * n_dr + d - 1) * n_dc + dc] if d >= 1 else NEG_INF
            up = rpb_ref[(h * n_dr + d) * n_dc + dc] if d < n_dr else NEG_INF
            acc = jnp.where(rel == dc, jnp.where(hi, up, lo), acc)
        valid = in_win
        if d == 0:
            valid = valid & hi
        if d == n_dr:
            valid = valid & jnp.logical_not(hi)
        o_ref[0, d] = jnp.where(valid, acc, NEG_INF)


def _rpb_table(rpb):
    nh, n_dr, n_dc = rpb.shape
    return pl.pallas_call(
        functools.partial(_rpb_table_kernel, n_dr=n_dr, n_dc=n_dc),
        grid=(nh,),
        in_specs=[pl.BlockSpec(memory_space=pltpu.SMEM)],
        out_specs=pl.BlockSpec((1, n_dr + 1, GRID_W, 2 * GRID_W), lambda h: (h, 0, 0, 0)),
        out_shape=jax.ShapeDtypeStruct((nh, n_dr + 1, GRID_W, 2 * GRID_W), F32),
        compiler_params=_cparams("arbitrary"),
        name="rpb_table",
    )(rpb.reshape(-1))


def _na_kernel(*refs, sx, sc, want_ctx):
    if want_ctx:
        q_x, k_x, v_x, q_c, k_c, v_c, tab, o_x, o_c = refs
    else:
        q_x, k_x, v_x, k_c, v_c, tab, o_x = refs
    w = GRID_W
    rows = sx // w
    qb = NA_QROWS * w
    kb = NA_KROWS * w
    n_blk = rows // NA_QROWS
    scale = NA_HEAD_DIM ** -0.5
    lane = lax.broadcasted_iota(jnp.int32, (1, 2 * NA_HEAD_DIM), 1)
    in_head = (lane < NA_HEAD_DIM, lane >= NA_HEAD_DIM)
    keyrow = lax.broadcasted_iota(jnp.int32, (1, kb), 1) // w
    kc = k_c[0]
    vc = v_c[0]
    n_tab = tab.shape[1]

    def softmax_pv(parts):
        m = parts[0][0].max(axis=-1, keepdims=True)
        for s, _ in parts[1:]:
            m = jnp.maximum(m, s.max(axis=-1, keepdims=True))
        acc, den = None, None
        for s, v in parts:
            e = jnp.exp(s - m)
            den = e.sum(axis=-1, keepdims=True) if den is None else den + e.sum(axis=-1, keepdims=True)
            pv = _dot(e.astype(BF16), v)
            acc = pv if acc is None else acc + pv
        return acc / den

    def block_body(rb, carry):
        ws = jnp.clip(NA_QROWS * rb - NA_KH // 2, 0, rows - NA_KROWS)
        q0 = pl.multiple_of(rb * qb, qb)
        k0 = pl.multiple_of(ws * w, w)
        qblk = q_x[0, pl.ds(q0, qb), :]
        kwin = k_x[0, pl.ds(k0, kb), :]
        vwin = v_x[0, pl.ds(k0, kb), :]
        out = jnp.zeros((qb, 2 * NA_HEAD_DIM), F32)
        for hh in range(2):
            qm = jnp.where(in_head[hh], qblk, jnp.zeros_like(qblk))
            s_loc = _dot_nt(qm, kwin) * scale
            s_ctx = _dot_nt(qm, kc) * scale
            pieces = []
            for rq in range(NA_QROWS):
                r = NA_QROWS * rb + rq
                rs = jnp.clip(r - NA_KH // 2, 0, rows - NA_KH)
                lo = rs - ws
                valid = (keyrow >= lo) & (keyrow < lo + NA_KH)
                blocks = []
                for ip in range(NA_KROWS // 2):
                    dr_lo = ws + 2 * ip - r + NA_KH - 1
                    blocks.append(tab[hh, jnp.clip(dr_lo + 1, 0, n_tab - 1)])
                bias = jnp.concatenate(blocks, axis=1)
                piece = s_loc[rq * w:(rq + 1) * w, :] + bias
                pieces.append(jnp.where(valid, piece, NEG_INF))
            s_loc = jnp.concatenate(pieces, axis=0)
            o = softmax_pv([(s_loc, vwin), (s_ctx, vc)])
            out = jnp.where(in_head[hh], o, out)
        o_x[0, pl.ds(q0, qb), :] = out.astype(o_x.dtype)
        return carry

    lax.fori_loop(0, n_blk, block_body, 0)

    if want_ctx:
        qc = q_c[0]
        out = jnp.zeros((sc, 2 * NA_HEAD_DIM), F32)
        for hh in range(2):
            qm = jnp.where(in_head[hh], qc, jnp.zeros_like(qc))
            o = softmax_pv([(_dot_nt(qm, kc) * scale, vc)])
            out = jnp.where(in_head[hh], o, out)
        o_c[0] = out.astype(o_c.dtype)


def _na_attention(q_x, k_x, v_x, q_c, k_c, v_c, table, want_ctx):
    bsz, sx, dim = q_x.shape
    sc = k_c.shape[1]
    pw = 2 * NA_HEAD_DIM
    n_pair = dim // pw

    def seq(s):
        return pl.BlockSpec((1, s, pw), lambda p, b: (b, 0, p))

    tspec = pl.BlockSpec((2,) + table.shape[1:], lambda p, b: (p, 0, 0, 0))
    if want_ctx:
        args = (q_x, k_x, v_x, q_c, k_c, v_c, table)
        in_specs = [seq(sx)] * 3 + [seq(sc)] * 3 + [tspec]
        out_specs = [seq(sx), seq(sc)]
        out_shape = [jax.ShapeDtypeStruct((bsz, sx, dim), BF16), jax.ShapeDtypeStruct((bsz, sc, dim), BF16)]
    else:
        args = (q_x, k_x, v_x, k_c, v_c, table)
        in_specs = [seq(sx)] * 3 + [seq(sc)] * 2 + [tspec]
        out_specs = [seq(sx)]
        out_shape = [jax.ShapeDtypeStruct((bsz, sx, dim), BF16)]
    res = pl.pallas_call(
        functools.partial(_na_kernel, sx=sx, sc=sc, want_ctx=want_ctx),
        grid=(n_pair, bsz),
        in_specs=in_specs, out_specs=out_specs, out_shape=out_shape,
        compiler_params=_cparams("arbitrary", "arbitrary"),
        name="na_attention",
    )(*args)
    return res if want_ctx else (res[0], None)


def _top_rows(s, k, exact, want_rank=True):
    n, tt = s.shape
    top_id = lax.broadcasted_iota(jnp.int32, (k, tt), 0)
    if exact:
        rowid = lax.broadcasted_iota(jnp.int32, (n, tt), 0).astype(F32)
    work = s
    top = jnp.zeros((k, tt), F32)
    rank = jnp.full((n, tt), float(k), F32) if want_rank else None
    for it in range(k):
        m = jnp.max(work, axis=0, keepdims=True)
        sel = work == m
        if exact:
            sel = rowid == jnp.min(jnp.where(sel, rowid, float(n)), axis=0, keepdims=True)
        top = jnp.where(top_id == it, m, top)
        if want_rank:
            rank = jnp.where(sel, float(it), rank)
        work = jnp.where(sel, -jnp.inf, work)
    picked = work == -jnp.inf
    n_sel = jnp.sum(jnp.where(picked, 1.0, 0.0), axis=0, keepdims=True)
    return top, rank, picked, n_sel


def _peer_route_kernel(h_ref, wq_ref, keys_ref, cnt_ref, rk_ref, e0_ref, e1_ref, s_s, top_s, rank_s, cnti_s, z_s):
    nk, k = PEER_KEYS, PEER_TOPK
    tt = h_ref.shape[1]
    sw = top_s.shape[2]
    q = _dot(h_ref[0], wq_ref[0]).astype(BF16)
    for z in range(2):
        sz = _dot_nt(keys_ref[z], q[:, z * nk:(z + 1) * nk])
        for t in range(tt // sw):
            s_s[z, t] = sz[:, t * sw:(t + 1) * sw]

    def pair_stage(t0, t1, exact):
        cand = jnp.concatenate([t0[0:1] + t1] + [t0[a:a + 1] + t1[0:8] for a in range(1, 8)]
                               + [t0[8:16] + t1[0:1]], axis=0)
        _, _, picked, n_sel = _top_rows(cand, k, exact, want_rank=False)
        pf = jnp.where(picked, 1.0, 0.0)
        z_sum = jnp.sum(pf * jnp.exp(cand - cand[0:1]), axis=0, keepdims=True)
        cnts = [jnp.sum(pf[0:k], axis=0, keepdims=True)]
        cnts += [jnp.sum(pf[k + 8 * (a - 1):k + 8 * a], axis=0, keepdims=True) for a in range(1, 8)]
        cnts += [pf[k + 56 + a:k + 57 + a] for a in range(8)]
        return cnts, z_sum, n_sel

    def strip(si, carry):
        cols = pl.ds(pl.multiple_of(si * sw, sw), sw)
        s0, s1 = s_s[0, si], s_s[1, si]

        t0, _, _, n0 = _top_rows(s0, k, False, want_rank=False)
        t1, rank1, _, n1 = _top_rows(s1, k, False)
        cnts, z_sum, n2 = pair_stage(t0, t1, False)
        cnt_i = jnp.zeros_like(s0)
        for a in range(k):
            cnt_i = jnp.where(s0 == t0[a:a + 1], cnts[a], cnt_i)
        top_s[0], top_s[1] = t0, t1
        rank_s[...] = rank1
        cnti_s[...] = cnt_i
        z_s[...] = z_sum
        ties = jnp.max(jnp.abs(n0 - float(k)) + jnp.abs(n1 - float(k)) + jnp.abs(n2 - float(k)))

        @pl.when(ties > 0.5)
        def _():
            t0, rank0, _, _ = _top_rows(s0, k, True)
            t1, rank1, _, _ = _top_rows(s1, k, True)
            cnts, z_sum, _ = pair_stage(t0, t1, True)
            cnt_i = jnp.zeros_like(s0)
            for a in range(k):
                cnt_i = jnp.where(rank0 == float(a), cnts[a], cnt_i)
            top_s[0], top_s[1] = t0, t1
            rank_s[...] = rank1
            cnti_s[...] = cnt_i
            z_s[...] = z_sum

        cnt_i = cnti_s[...]
        e0 = jnp.exp(s0 - top_s[0, 0:1]) / z_s[...]
        for c in range(nk // PEER_ICHUNK):
            cnt_ref[0, c, :, cols] = cnt_i[c * PEER_ICHUNK:(c + 1) * PEER_ICHUNK]
            e0_ref[0, c, :, cols] = e0[c * PEER_ICHUNK:(c + 1) * PEER_ICHUNK]
        rk_ref[0, :, cols] = rank_s[...].astype(rk_ref.dtype)
        e1_ref[0, :, cols] = jnp.exp(s1 - top_s[1, 0:1]).astype(e1_ref.dtype)
        return carry

    lax.fori_loop(0, tt // sw, strip, 0)


ROUTE_STRIP = 256


def _peer_route(hx, w_q, keys, tt):
    bn, sn, d = hx.shape
    nk, nh, ic = PEER_KEYS, PEER_HEADS, PEER_ICHUNK
    spec_i = pl.BlockSpec((1, nk // ic, ic, tt), lambda b, i, h: (b, 0, h, i))
    spec_j = pl.BlockSpec((1, nk, tt), lambda b, i, h: (b, h, i))
    shape_i = jax.ShapeDtypeStruct((bn, nk // ic, nh * ic, sn), F32)
    shape_j = jax.ShapeDtypeStruct((bn, nh * nk, sn), BF16)
    sw = min(ROUTE_STRIP, tt)
    return pl.pallas_call(
        _peer_route_kernel,
        grid=(bn, sn // tt, nh),
        in_specs=[pl.BlockSpec((1, tt, d), lambda b, i, h: (b, i, 0)),
                  pl.BlockSpec((1, d, 2 * nk), lambda b, i, h: (h, 0, 0)),
                  pl.BlockSpec((2, nk, keys.shape[2]), lambda b, i, h: (h, 0, 0))],
        out_specs=[spec_i, spec_j, spec_i, spec_j],
        out_shape=[shape_i, shape_j, shape_i, shape_j],
        scratch_shapes=[pltpu.VMEM((2, tt // sw, nk, sw), F32), pltpu.VMEM((2, PEER_TOPK, sw), F32),
                        pltpu.VMEM((nk, sw), F32), pltpu.VMEM((nk, sw), F32), pltpu.VMEM((1, sw), F32)],
        compiler_params=_cparams("arbitrary", "arbitrary", "arbitrary"),
        name="peer_route",
    )(hx, w_q, keys)


def _peer_dense_kernel(h_ref, cnt_ref, rk_ref, e0_ref, e1_ref, u_ref, vt_ref, x_ref, g_ref, o_ref,
                       acc_ref, act_ref, p_ref, hx_s, rk_s, e1_s, cnt_s, e0_s):
    nk, ic = PEER_KEYS, PEER_ICHUNK
    ck = pl.program_id(2)
    n_slab = act_ref.shape[0]
    sw = min(PEER_STRIP, n_slab * LANES)
    per = sw // LANES
    n_strip = n_slab // per
    il_group, j_group = 4, 4

    @pl.when(ck == 0)
    def _():
        acc_ref[...] = jnp.zeros_like(acc_ref)
        hx_s[...] = h_ref[0]
        for t in range(n_slab):
            rk_s[t] = rk_ref[0, :, t * LANES:(t + 1) * LANES].astype(rk_s.dtype)
            e1_s[t] = e1_ref[0, :, t * LANES:(t + 1) * LANES].astype(e1_s.dtype)

    for t in range(n_slab):
        cnt_s[t] = cnt_ref[0, 0, :, t * LANES:(t + 1) * LANES]
        e0_s[t] = e0_ref[0, 0, :, t * LANES:(t + 1) * LANES]

    def activations(s):
        r0 = pl.multiple_of(s * sw, sw)
        a = _gelu_tanh(_dot_nt(u_ref[...], hx_s[pl.ds(r0, sw), :])).astype(act_ref.dtype)
        for k in range(per):
            act_ref[s * per + k] = a[:, k * LANES:(k + 1) * LANES]

    def gate_weights(t):
        jr = nk // j_group
        for ig in range(ic // il_group):
            for jg in range(j_group):
                wgt = [None] * il_group
                for h in range(PEER_HEADS):
                    cnt8 = cnt_s[t, h * ic:(h + 1) * ic, :]
                    e08 = e0_s[t, h * ic:(h + 1) * ic, :]
                    rk = rk_s[t, h * nk + jg * jr:h * nk + (jg + 1) * jr, :]
                    e1 = e1_s[t, h * nk + jg * jr:h * nk + (jg + 1) * jr, :]
                    for g in range(il_group):
                        il = ig * il_group + g
                        term = jnp.where(rk < cnt8[il:il + 1], e1 * e08[il:il + 1], 0.0)
                        wgt[g] = term if wgt[g] is None else wgt[g] + term
                for g in range(il_group):
                    r0 = (ig * il_group + g) * nk + jg * jr
                    p_ref[t, r0:r0 + jr, :] = (wgt[g] * act_ref[t, r0:r0 + jr, :]).astype(p_ref.dtype)

    def combine(s):
        for k in range(per):
            gate_weights(s * per + k)
        y = _dot(vt_ref[...], jnp.concatenate([p_ref[s * per + k] for k in range(per)], axis=1))
        for k in range(per):
            acc_ref[s * per + k] += y[:, k * LANES:(k + 1) * LANES]

    activations(0)

    def strip(s, carry):
        activations(s + 1)
        combine(s)
        return carry

    lax.fori_loop(0, n_strip - 1, strip, 0)
    combine(n_strip - 1)

    @pl.when(ck == pl.num_programs(2) - 1)
    def _():
        for t in range(n_slab):
            rows = slice(t * LANES, (t + 1) * LANES)
            o_ref[0, rows, :] = x_ref[0, rows, :] + g_ref[0] * acc_ref[t].T


PEER_STRIP = 256


def _peer_dense(hx, route, u, v_t, x, gate, tt):
    bn, sn, d = hx.shape
    nk, nh, ic = PEER_KEYS, PEER_HEADS, PEER_ICHUNK
    ne = ic * nk
    n_chunk = u.shape[0] // ne
    spec_i = pl.BlockSpec((1, 1, nh * ic, tt), lambda b, i, c: (b, c, 0, i))
    spec_j = pl.BlockSpec((1, nh * nk, tt), lambda b, i, c: (b, 0, i))
    tok = pl.BlockSpec((1, tt, d), lambda b, i, c: (b, i, 0))
    n_slab = tt // LANES
    return pl.pallas_call(
        _peer_dense_kernel,
        grid=(bn, sn // tt, n_chunk),
        in_specs=[tok, spec_i, spec_j, spec_i, spec_j,
                  pl.BlockSpec((ne, d), lambda b, i, c: (c, 0)),
                  pl.BlockSpec((d, ne), lambda b, i, c: (0, c)),
                  tok,
                  pl.BlockSpec((1, 1, d), lambda b, i, c: (b, 0, 0))],
        out_specs=tok,
        out_shape=jax.ShapeDtypeStruct((bn, sn, d), F32),
        scratch_shapes=[pltpu.VMEM((n_slab, d, LANES), F32), pltpu.VMEM((n_slab, ne, LANES), BF16),
                        pltpu.VMEM((n_slab, ne, LANES), BF16), pltpu.VMEM((tt, d), BF16),
                        pltpu.VMEM((n_slab, nh * nk, LANES), F32), pltpu.VMEM((n_slab, nh * nk, LANES), F32),
                        pltpu.VMEM((n_slab, nh * ic, LANES), F32), pltpu.VMEM((n_slab, nh * ic, LANES), F32)],
        compiler_params=_cparams("arbitrary", "arbitrary", "arbitrary"),
        name="peer_dense",
    )(hx, *route, u, v_t, x, gate)


def _peer(hx, x, gate, w_q, keys, u, v_t, tt=1024):
    tt = min(tt, hx.shape[1])
    route = _peer_route(hx, w_q, keys, tt)
    return _peer_dense(hx, route, u, v_t, x, gate, tt)


def _group_lanes(p, width):
    g = p.reshape(2, SSD_GROUPS, SSD_HPG).transpose(1, 0, 2).reshape(SSD_GROUPS, 2 * SSD_HPG)
    return jnp.pad(g, ((0, 0), (0, width - 2 * SSD_HPG))).reshape(SSD_GROUPS, 1, width)


def _block_diag(w, tile):
    two, nb, bd, _ = w.shape
    per = tile // bd
    w = w.reshape(two, nb // per, per, bd, bd)
    eye = jnp.eye(per, dtype=w.dtype)
    return jnp.einsum("dtpij,pq->dtpiqj", w, eye).reshape(two, nb // per, tile, tile)


def kernel(x, c, ctx, c_ctx, ada_w, ada_b, norm1_g, norm2_g, ev_w_in, ev_conv_w, ev_conv_b, ev_a_log,
           ev_dt_bias, ev_d, ev_ssd_norm_g, ev_lru_conv_w, ev_lru_conv_b, ev_lru_wa, ev_lru_ba, ev_lru_wx,
           ev_lru_bx, ev_lru_lam, ev_w_out, od_w_qkv, od_q_norm_g, od_k_norm_g, od_rpb, od_w_o,
           pe_w_q, pe_keys, pe_u, pe_v):
    bsz, sx, d = x.shape
    sc = ctx.shape[1]
    depth = ada_w.shape[0]

    n_c = bsz + 1
    rows = -(-n_c // SUBLANES) * SUBLANES
    c_all = jnp.concatenate([c, c_ctx[None], jnp.zeros((rows - n_c, d), F32)], axis=0)
    mods = _ada_mods(c_all, ada_w, ada_b).reshape(depth, rows, 6, d)

    ctx = ctx.reshape(1, bsz * sc, d)

    def per_batch(t):
        return t.reshape(bsz, sc, t.shape[-1])

    for layer in range(depth):
        last = layer == depth - 1
        j = layer // 2
        mod_x = mods[layer, :bsz]
        mod_c = mods[layer, bsz:bsz + 1]
        g1 = norm1_g[layer][None]
        g2 = norm2_g[layer][None]
        want_ctx = not last

        if layer % 2 == 0:
            w_in = ev_w_in[j]
            o_dt, o_xl = SSD_XBC, SSD_XBC + 2 * SSD_HEADS
            o_z = o_xl + LRU_WIDTH
            o_gate = o_z + SSD_INNER
            w_dt = w_in[:, o_dt:o_xl].reshape(d, 2, SSD_GROUPS, SSD_HPG).transpose(0, 2, 1, 3)
            w_dt = jnp.pad(w_dt.reshape(d, SSD_GROUPS, 2 * SSD_HPG), ((0, 0), (0, 0), (0, LANES - 2 * SSD_HPG)))
            ws = [w_in[:, :o_dt].astype(BF16), w_dt.reshape(d, SSD_GROUPS * LANES).astype(BF16),
                  w_in[:, o_xl:o_z].astype(BF16), w_in[:, o_z:o_gate].astype(BF16), w_in[:, o_gate:].astype(BF16)]
            dts = [BF16, F32, BF16, BF16, BF16]
            px = _nm_linear(x, g1, mod_x[:, 0:2], ws, dts)
            pc = [per_batch(t) for t in _nm_linear(ctx, g1, mod_c[:, 0:2], ws, dts)]
            y_ssd_x, y_ssd_c = _ssd_mixer(
                px[0], pc[0], px[1], pc[1], px[3], pc[3], ev_conv_w[j], ev_conv_b[j][None],
                _group_lanes(ev_a_log[j], LANES), _group_lanes(ev_dt_bias[j], LANES),
                jnp.repeat(ev_d[j], SSD_HEAD_DIM).reshape(SSD_GROUPS, 1, SSD_GW),
                ev_ssd_norm_g[j].reshape(SSD_GROUPS, 1, SSD_GW))
            y_lru_x, y_lru_c = _lru_mixer(
                px[2], pc[2], px[4], pc[4], ev_lru_conv_w[j], ev_lru_conv_b[j][None],
                _block_diag(ev_lru_wa[j], LRU_TILE).astype(BF16), _block_diag(ev_lru_wx[j], LRU_TILE).astype(BF16),
                ev_lru_ba[j], ev_lru_bx[j], ev_lru_lam[j])
            w_out = ev_w_out[j].astype(BF16)
            w_outs = [w_out[:SSD_INNER], w_out[SSD_INNER:]]
            acts_x = [y_ssd_x, y_lru_x]
            acts_c = [y_ssd_c.reshape(1, bsz * sc, -1), y_lru_c.reshape(1, bsz * sc, -1)]
        else:
            w_qkv = od_w_qkv[j].astype(BF16)
            nd = w_qkv.shape[1] // 3
            ws = [w_qkv[:, :nd], w_qkv[:, nd:2 * nd], w_qkv[:, 2 * nd:]]
            gains = [jnp.tile(od_q_norm_g[j], NA_HEADS)[None], jnp.tile(od_k_norm_g[j], NA_HEADS)[None], None]
            q_x, k_x, v_x = _nm_linear(x, g1, mod_x[:, 0:2], ws, [BF16] * 3, gains)
            q_c, k_c, v_c = [per_batch(t) for t in _nm_linear(ctx, g1, mod_c[:, 0:2], ws, [BF16] * 3, gains)]
            table = _rpb_table(od_rpb[j])
            o_x, o_c = _na_attention(q_x, k_x, v_x, q_c, k_c, v_c, table, want_ctx)
            w_outs = [od_w_o[j].astype(BF16)]
            acts_x = [o_x]
            acts_c = [o_c.reshape(1, bsz * sc, -1)] if want_ctx else None

        w_q = pe_w_q[layer].reshape(d, PEER_HEADS, 2 * PEER_KEYS).transpose(1, 0, 2).astype(BF16)
        keys = pe_keys[layer].reshape(2 * PEER_HEADS, PEER_KEYS, -1).astype(BF16)
        u = pe_u[layer].astype(BF16)
        v_t = pe_v[layer].T.astype(BF16)

        x, hx = _out_linear(acts_x, w_outs, x, mod_x[:, 2:5], g2)
        x = _peer(hx, x, mod_x[:, 5:6], w_q, keys, u, v_t)
        if want_ctx:
            ctx, hc = _out_linear(acts_c, w_outs, ctx, mod_c[:, 2:5], g2)
            ctx = _peer(hc, ctx, mod_c[:, 5:6], w_q, keys, u, v_t)
    return x
```

```python
import functools
import math

import jax
import jax.numpy as jnp
from jax import lax
from jax.experimental import pallas as pl
from jax.experimental.pallas import tpu as pltpu

F32 = jnp.float32
BF16 = jnp.bfloat16
HIGHEST = lax.Precision.HIGHEST

EPS = 1e-6
NEG_INF = -1e30

GRID_W = 64
SSD_HEADS = 16
SSD_HEAD_DIM = 64
SSD_GROUPS = 4
SSD_HPG = SSD_HEADS // SSD_GROUPS
SSD_STATE = 128
SSD_CHUNK = 128
SSD_INNER = SSD_HEADS * SSD_HEAD_DIM
SSD_GW = SSD_INNER // SSD_GROUPS
SSD_XBC = SSD_INNER + 2 * SSD_GROUPS * SSD_STATE
LRU_WIDTH = 1024
LRU_BLOCKS = 16
LRU_BLOCK_DIM = LRU_WIDTH // LRU_BLOCKS
LRU_C = 8.0
LRU_TILE = 256
NA_HEADS = 16
NA_HEAD_DIM = 64
NA_KH = 8
NA_KW = 16
NA_QROWS = 4
NA_KROWS = 12
PEER_HEADS = 8
PEER_KEYS = 128
PEER_TOPK = 16
PEER_ICHUNK = 8

LANES = 128
SUBLANES = 8
VMEM_LIMIT_BYTES = 56 * 1024 * 1024


def _cparams(*sem):
    return pltpu.CompilerParams(dimension_semantics=sem, vmem_limit_bytes=VMEM_LIMIT_BYTES)


def _silu(x):
    return x * (1.0 / (1.0 + jnp.exp(-x)))


def _sigmoid(x):
    return 1.0 / (1.0 + jnp.exp(-x))


def _softplus(x):
    return jnp.maximum(x, 0.0) + jnp.log(1.0 + jnp.exp(-jnp.abs(x)))


def _gelu_tanh(x):
    k0 = -2.0 * math.sqrt(2.0 / math.pi)
    return x / (1.0 + jnp.exp(x * (k0 + (k0 * 0.044715) * (x * x))))


def _rms_mod(x, g, shift, scale):
    ms = jnp.mean(x * x, axis=-1, keepdims=True)
    y = x * lax.rsqrt(ms + EPS) * g
    return y * (1.0 + scale) + shift


def _dot(a, b):
    return jnp.dot(a, b, preferred_element_type=F32)


def _dot_nt(a, b):
    return lax.dot_general(a, b, (((1,), (1,)), ((), ())), preferred_element_type=F32)


def _ada_kernel(c_ref, w_ref, b_ref, o_ref):
    s = _silu(c_ref[...])
    o_ref[0] = jnp.dot(s, w_ref[0], preferred_element_type=F32, precision=HIGHEST) + b_ref[0]


def _ada_mods(c_all, ada_w, ada_b):
    depth, d, n = ada_w.shape
    rows = c_all.shape[0]
    tn = 1536
    return pl.pallas_call(
        _ada_kernel,
        grid=(depth, n // tn),
        in_specs=[pl.BlockSpec((rows, d), lambda l, j: (0, 0)),
                  pl.BlockSpec((1, d, tn), lambda l, j: (l, 0, j)),
                  pl.BlockSpec((1, 1, tn), lambda l, j: (l, 0, j))],
        out_specs=pl.BlockSpec((1, rows, tn), lambda l, j: (l, 0, j)),
        out_shape=jax.ShapeDtypeStruct((depth, rows, n), F32),
        compiler_params=_cparams("arbitrary", "arbitrary"),
        name="ada_mods",
    )(c_all, ada_w, ada_b.reshape(depth, 1, n))


def _head_block_ones(n):
    r = lax.broadcasted_iota(jnp.int32, (n, n), 0) // NA_HEAD_DIM
    c = lax.broadcasted_iota(jnp.int32, (n, n), 1) // NA_HEAD_DIM
    return (r == c).astype(F32)


def _nm_linear_kernel(*refs, n_out, head_norm, tn):
    x_ref, g_ref, mod_ref = refs[:3]
    w_refs = refs[3:3 + n_out]
    hg_refs = refs[3 + n_out:3 + n_out + sum(head_norm)]
    o_refs = refs[3 + n_out + sum(head_norm):]
    h = _rms_mod(x_ref[0], g_ref[...], mod_ref[0, 0:1, :], mod_ref[0, 1:2, :]).astype(BF16)
    hg_i = 0
    for w_ref, o_ref, hn in zip(w_refs, o_refs, head_norm):
        n = w_ref.shape[1]
        for j in range(n // tn):
            y = _dot(h, w_ref[:, j * tn:(j + 1) * tn])
            if hn:
                ss = jnp.dot(y * y, _head_block_ones(tn), preferred_element_type=F32, precision=HIGHEST)
                y = y * lax.rsqrt(ss * (1.0 / NA_HEAD_DIM) + EPS) * hg_refs[hg_i][:, j * tn:(j + 1) * tn]
            o_ref[0, :, j * tn:(j + 1) * tn] = y.astype(o_ref.dtype)
        hg_i += hn


def _nm_linear(x, g, mod, ws, out_dtypes, head_gains=None, tm=512, tn=256):
    bn, sn, d = x.shape
    tm = min(tm, sn)
    n_out = len(ws)
    head_gains = head_gains or [None] * n_out
    head_norm = tuple(hg is not None for hg in head_gains)
    hgs = [hg for hg in head_gains if hg is not None]
    in_specs = [pl.BlockSpec((1, tm, d), lambda b, i: (b, i, 0)),
                pl.BlockSpec((1, d), lambda b, i: (0, 0)),
                pl.BlockSpec((1, 2, d), lambda b, i: (b, 0, 0))]
    in_specs += [pl.BlockSpec(w.shape, lambda b, i: (0, 0)) for w in ws]
    in_specs += [pl.BlockSpec(hg.shape, lambda b, i: (0, 0)) for hg in hgs]
    out_specs = [pl.BlockSpec((1, tm, w.shape[1]), lambda b, i: (b, i, 0)) for w in ws]
    out_shape = [jax.ShapeDtypeStruct((bn, sn, w.shape[1]), dt) for w, dt in zip(ws, out_dtypes)]
    return pl.pallas_call(
        functools.partial(_nm_linear_kernel, n_out=n_out, head_norm=head_norm, tn=tn),
        grid=(bn, sn // tm),
        in_specs=in_specs, out_specs=out_specs, out_shape=out_shape,
        compiler_params=_cparams("arbitrary", "arbitrary"),
        name="nm_linear",
    )(x, g, mod, *ws, *hgs)


def _out_linear_kernel(*refs, n_in):
    a_refs = refs[:n_in]
    w_refs = refs[n_in:2 * n_in]
    x_ref, mod_ref, g_ref, xo_ref, ho_ref = refs[2 * n_in:]
    y = _dot(a_refs[0][0], w_refs[0][...])
    for a_ref, w_ref in zip(a_refs[1:], w_refs[1:]):
        y = y + _dot(a_ref[0], w_ref[...])
    xn = x_ref[0] + mod_ref[0, 0:1, :] * y
    xo_ref[0] = xn
    ho_ref[0] = _rms_mod(xn, g_ref[...], mod_ref[0, 1:2, :], mod_ref[0, 2:3, :]).astype(BF16)


def _out_linear(acts, ws, x, mod, g, tm=512):
    bn, sn, d = x.shape
    tm = min(tm, sn)
    n_in = len(acts)
    in_specs = [pl.BlockSpec((1, tm, a.shape[2]), lambda b, i: (b, i, 0)) for a in acts]
    in_specs += [pl.BlockSpec(w.shape, lambda b, i: (0, 0)) for w in ws]
    in_specs += [pl.BlockSpec((1, tm, d), lambda b, i: (b, i, 0)),
                 pl.BlockSpec((1, 3, d), lambda b, i: (b, 0, 0)),
                 pl.BlockSpec((1, d), lambda b, i: (0, 0))]
    return pl.pallas_call(
        functools.partial(_out_linear_kernel, n_in=n_in),
        grid=(bn, sn // tm),
        in_specs=in_specs,
        out_specs=[pl.BlockSpec((1, tm, d), lambda b, i: (b, i, 0))] * 2,
        out_shape=[jax.ShapeDtypeStruct((bn, sn, d), F32), jax.ShapeDtypeStruct((bn, sn, d), BF16)],
        compiler_params=_cparams("arbitrary", "arbitrary"),
        name="out_linear",
    )(*acts, *ws, x, mod, g)


CONV_HALO = 16


def _conv_chunk(src_ref, s, seg_len, w, bias, rows=SSD_CHUNK):
    ncol = src_ref.shape[2]
    if s > 0:
        prev = src_ref[0, s - CONV_HALO:s, :].astype(F32)
    else:
        prev = jnp.zeros((CONV_HALO, ncol), F32)
    cur = src_ref[0, s:s + rows, :].astype(F32)
    if s + rows < seg_len:
        nxt = src_ref[0, s + rows:s + rows + CONV_HALO, :].astype(F32)
    else:
        nxt = jnp.zeros((CONV_HALO, ncol), F32)
    win = jnp.concatenate([prev, cur, nxt], axis=0)
    taps = w.shape[0]
    acc = bias
    for k in range(taps):
        off = CONV_HALO - taps // 2 + k
        acc = acc + win[off:off + rows, :] * w[k:k + 1, :]
    return acc


def _lane_head_expand(cols, width):
    nh = len(cols)
    hd = width // nh
    rows = cols[0].shape[0]
    lane_head = lax.broadcasted_iota(jnp.int32, (rows, width), 1) // hd
    out = jnp.broadcast_to(cols[nh - 1], (rows, width))
    for h in range(nh - 2, -1, -1):
        out = jnp.where(lane_head == h, jnp.broadcast_to(cols[h], (rows, width)), out)
    return out


def _ssd_kernel(xs_x, bm_x, cm_x, xs_c, bm_c, cm_c, dt_x, dt_c, z_x, z_c,
                cw_xs, cw_b, cw_c, cb_xs, cb_b, cb_c, alog_ref, dtb_ref, dsk_ref, ng_ref,
                y_x, y_c,
                xs_s, bm_s, cm_s, dt_s, y_s, st_s, cs_s, cst_s, *, sx, sc):
    q = SSD_CHUNK
    nc_c, nc_x = sc // q, sx // q
    nc = nc_c + nc_x
    gw = xs_s.shape[1]

    for seg_ref3, seg_len, base in (((xs_c, bm_c, cm_c), sc, 0), ((xs_x, bm_x, cm_x), sx, sc)):
        for ci in range(seg_len // q):
            s = ci * q
            for src, dst, w_ref, b_ref in zip(seg_ref3, (xs_s, bm_s, cm_s), (cw_xs, cw_b, cw_c),
                                              (cb_xs, cb_b, cb_c)):
                dst[base + s:base + s + q, :] = _silu(_conv_chunk(src, s, seg_len, w_ref[...], b_ref[...]))
    dt_s[0:sc, :] = _softplus(dt_c[0] + dtb_ref[0])
    dt_s[sc:sc + sx, :] = _softplus(dt_x[0] + dtb_ref[0])

    a_neg = -jnp.exp(alog_ref[0])
    row = lax.broadcasted_iota(jnp.int32, (q, q), 0)
    col = lax.broadcasted_iota(jnp.int32, (q, q), 1)
    tri = ((col <= row).astype(F32), (col >= row).astype(F32))
    keep = (col <= row, col >= row)
    lane_head = lax.broadcasted_iota(jnp.int32, (q, gw), 1) // SSD_HEAD_DIM

    st_s[...] = jnp.zeros_like(st_s)
    y_s[...] = jnp.zeros_like(y_s)

    for ci in range(nc):
        la = dt_s[ci * q:(ci + 1) * q, :] * a_neg
        for d in range(2):
            cs = jnp.dot(tri[d], la, preferred_element_type=F32, precision=HIGHEST)
            cs_s[d, ci * q:(ci + 1) * q, :] = cs
            cst_s[d, ci * q:(ci + 1) * q, :] = cs.T

    def chunk_body(i, carry):
        for d in range(2):
            if d == 0:
                ci = i
            else:
                ci = jnp.where(i < nc_c, nc_c - 1 - i, nc + nc_c - 1 - i)
            r0 = pl.multiple_of(ci * q, q)
            xs = xs_s[pl.ds(r0, q), :]
            bm = bm_s[pl.ds(r0, q), :]
            cm = cm_s[pl.ds(r0, q), :]
            dt = dt_s[pl.ds(r0, q), :]
            cs = cs_s[d, pl.ds(r0, q), :]
            cs_t = cst_s[d, pl.ds(r0, q), :]
            cb = _dot_nt(cm.astype(BF16), bm.astype(BF16))
            heads = [d * SSD_HPG + h for h in range(SSD_HPG)]
            dt_mat = _lane_head_expand([dt[:, c:c + 1] for c in heads], gw)
            cs_mat = _lane_head_expand([cs[:, c:c + 1] for c in heads], gw)
            xd = xs * dt_mat
            xd_b = xd.astype(BF16)
            y = jnp.zeros((q, gw), F32)
            for h, c in enumerate(heads):
                diff = cs[:, c:c + 1] - cs_t[c:c + 1, :]
                lmat = jnp.exp(jnp.where(keep[d], diff, NEG_INF))
                y = jnp.where(lane_head == h, _dot((cb * lmat).astype(BF16), xd_b), y)
            st = st_s[d]
            y = y + _dot(cm.astype(BF16), st.astype(BF16)) * jnp.exp(cs_mat)
            end = q - 1 if d == 0 else 0
            cs_end = cs_mat[end:end + 1, :]
            s_new = _dot(bm.T.astype(BF16), (xd * jnp.exp(cs_end - cs_mat)).astype(BF16))
            st_s[d] = st * jnp.exp(cs_end) + s_new
            y_s[pl.ds(r0, q), :] = y_s[pl.ds(r0, q), :] + y
        return carry

    lax.fori_loop(0, nc, chunk_body, 0)

    for ci in range(nc):
        s = ci * q
        if ci < nc_c:
            z = z_c[0, s:s + q, :]
        else:
            z = z_x[0, s - sc:s - sc + q, :]
        y = (y_s[s:s + q, :] + dsk_ref[0] * xs_s[s:s + q, :]) * _silu(z.astype(F32))
        ms = jnp.mean(y * y, axis=-1, keepdims=True)
        out = (y * lax.rsqrt(ms + EPS) * ng_ref[0]).astype(BF16)
        if ci < nc_c:
            y_c[0, s:s + q, :] = out
        else:
            y_x[0, s - sc:s - sc + q, :] = out


def _ssd_mixer(xbc_x, xbc_c, dt_x, dt_c, z_x, z_c, conv_w, conv_b, alog_g, dtb_g, dsk_g, ng_g):
    bsz, sx, _ = xbc_x.shape
    sc = xbc_c.shape[1]
    g, gw, n = SSD_GROUPS, SSD_GW, SSD_STATE
    nb = SSD_INNER // n
    taps = conv_w.shape[0]

    def seq(s, w, off):
        return pl.BlockSpec((1, s, w), lambda b, j, off=off: (b, 0, off + j))

    def par(r, w, off):
        return pl.BlockSpec((r, w), lambda b, j, off=off: (0, off + j))

    def grp(w):
        return pl.BlockSpec((1, 1, w), lambda b, j: (j, 0, 0))

    in_specs = [seq(sx, gw, 0), seq(sx, n, nb), seq(sx, n, nb + g),
                seq(sc, gw, 0), seq(sc, n, nb), seq(sc, n, nb + g),
                seq(sx, LANES, 0), seq(sc, LANES, 0), seq(sx, gw, 0), seq(sc, gw, 0),
                par(taps, gw, 0), par(taps, n, nb), par(taps, n, nb + g),
                par(1, gw, 0), par(1, n, nb), par(1, n, nb + g),
                grp(LANES), grp(LANES), grp(gw), grp(gw)]
    stot = sx + sc
    return pl.pallas_call(
        functools.partial(_ssd_kernel, sx=sx, sc=sc),
        grid=(bsz, g),
        in_specs=in_specs,
        out_specs=[seq(sx, gw, 0), seq(sc, gw, 0)],
        out_shape=[jax.ShapeDtypeStruct((bsz, sx, SSD_INNER), BF16),
                   jax.ShapeDtypeStruct((bsz, sc, SSD_INNER), BF16)],
        scratch_shapes=[pltpu.VMEM((stot, gw), F32), pltpu.VMEM((stot, n), F32), pltpu.VMEM((stot, n), F32),
                        pltpu.VMEM((stot, LANES), F32), pltpu.VMEM((stot, gw), F32), pltpu.VMEM((2, n, gw), F32),
                        pltpu.VMEM((2, stot, LANES), F32), pltpu.VMEM((2, stot, LANES), F32)],
        compiler_params=_cparams("arbitrary", "arbitrary"),
        name="ssd_mixer",
    )(xbc_x, xbc_x, xbc_x, xbc_c, xbc_c, xbc_c, dt_x, dt_c, z_x, z_c,
      conv_w, conv_w, conv_w, conv_b, conv_b, conv_b, alog_g, dtb_g, dsk_g, ng_g)


def _lru_kernel(xl_x, xl_c, gt_x, gt_c, cw, cb, wa, wx, ba, bx, lam, y_x, y_c,
                xr_s, a_s, b_s, y_s, *, sx, sc):
    q = SSD_CHUNK
    stot = sx + sc
    w = xr_s.shape[1]
    for src, seg_len, base in ((xl_c, sc, 0), (xl_x, sx, sc)):
        for ci in range(seg_len // q):
            s = ci * q
            xr_s[base + s:base + s + q, :] = _conv_chunk(src, s, seg_len, cw[...], cb[...])

    ng = stot // SUBLANES
    ng_c = sc // SUBLANES
    sub = lax.broadcasted_iota(jnp.int32, (SUBLANES, w), 0)
    rt = q
    for d in range(2):
        nsp = _softplus(-lam[d:d + 1, :])
        for ci in range(stot // rt):
            s = ci * rt
            xr = xr_s[s:s + rt, :]
            xb = xr.astype(BF16)
            r = _sigmoid(_dot(xb, wa[d, 0]) + ba[d:d + 1, :])
            ig = _sigmoid(_dot(xb, wx[d, 0]) + bx[d:d + 1, :])
            a = jnp.exp(-LRU_C * r * nsp)
            a_s[d, s:s + rt, :] = a
            b_s[d, s:s + rt, :] = jnp.sqrt(1.0 - a * a) * (ig * xr)

    y_s[...] = jnp.zeros_like(y_s)

    def group_body(k, carries):
        new = []
        for d in range(2):
            if d == 0:
                gi = k
            else:
                gi = jnp.where(k < ng_c, ng_c - 1 - k, ng + ng_c - 1 - k)
            r0 = pl.multiple_of(gi * SUBLANES, SUBLANES)
            a = a_s[d, pl.ds(r0, SUBLANES), :]
            b = b_s[d, pl.ds(r0, SUBLANES), :]
            for sh in (1, 2, 4):
                if d == 0:
                    valid = sub >= sh
                    a_sh = pltpu.roll(a, sh, axis=0)
                    b_sh = pltpu.roll(b, sh, axis=0)
                else:
                    valid = sub < SUBLANES - sh
                    a_sh = pltpu.roll(a, SUBLANES - sh, axis=0)
                    b_sh = pltpu.roll(b, SUBLANES - sh, axis=0)
                b = jnp.where(valid, a * b_sh + b, b)
                a = jnp.where(valid, a * a_sh, a)
            h = a * carries[d] + b
            y_s[pl.ds(r0, SUBLANES), :] = y_s[pl.ds(r0, SUBLANES), :] + h
            last = h[SUBLANES - 1:SUBLANES, :] if d == 0 else h[0:1, :]
            new.append(jnp.broadcast_to(last, (SUBLANES, w)))
        return tuple(new)

    zero = jnp.zeros((SUBLANES, w), F32)
    lax.fori_loop(0, ng, group_body, (zero, zero))

    for ci in range(stot // rt):
        s = ci * rt
        if s < sc:
            gate = gt_c[0, s:s + rt, :]
        else:
            gate = gt_x[0, s - sc:s - sc + rt, :]
        out = (y_s[s:s + rt, :] * _gelu_tanh(gate.astype(F32))).astype(BF16)
        if s < sc:
            y_c[0, s:s + rt, :] = out
        else:
            y_x[0, s - sc:s - sc + rt, :] = out


def _lru_mixer(xl_x, xl_c, gt_x, gt_c, conv_w, conv_b, wa_bd, wx_bd, ba, bx, lam):
    bsz, sx, width = xl_x.shape
    sc = xl_c.shape[1]
    w = LRU_TILE
    taps = conv_w.shape[0]

    def seq(s):
        return pl.BlockSpec((1, s, w), lambda b, j: (b, 0, j))

    def par(r):
        return pl.BlockSpec((r, w), lambda b, j: (0, j))

    wspec = pl.BlockSpec((2, 1, w, w), lambda b, j: (0, j, 0, 0))
    stot = sx + sc
    return pl.pallas_call(
        functools.partial(_lru_kernel, sx=sx, sc=sc),
        grid=(bsz, width // w),
        in_specs=[seq(sx), seq(sc), seq(sx), seq(sc), par(taps), par(1), wspec, wspec, par(2), par(2), par(2)],
        out_specs=[seq(sx), seq(sc)],
        out_shape=[jax.ShapeDtypeStruct((bsz, sx, width), BF16), jax.ShapeDtypeStruct((bsz, sc, width), BF16)],
        scratch_shapes=[pltpu.VMEM((stot, w), F32), pltpu.VMEM((2, stot, w), F32), pltpu.VMEM((2, stot, w), F32),
                        pltpu.VMEM((stot, w), F32)],
        compiler_params=_cparams("arbitrary", "arbitrary"),
        name="lru_mixer",
    )(xl_x, xl_c, gt_x, gt_c, conv_w, conv_b, wa_bd, wx_bd, ba, bx, lam)


def _rpb_table_kernel(rpb_ref, o_ref, *, n_dr, n_dc):
    h = pl.program_id(0)
    w = GRID_W
    qcol = lax.broadcasted_iota(jnp.int32, (w, 2 * w), 0)
    lane = lax.broadcasted_iota(jnp.int32, (w, 2 * w), 1)
    kcol = lane % w
    hi = lane >= w
    rel = kcol - qcol + (NA_KW - 1)
    cstart = jnp.clip(qcol - NA_KW // 2, 0, w - NA_KW)
    in_win = (kcol >= cstart) & (kcol < cstart + NA_KW)
    for d in range(n_dr + 1):
        acc = jnp.full((w, 2 * w), NEG_INF, F32)
        for dc in range(n_dc):
            lo = rpb_ref[(h * n_dr + d - 1) * n_dc + dc] if d >= 1 else NEG_INF
            up = rpb_ref[(h * n_dr + d) * n_dc + dc] if d < n_dr else NEG_INF
            acc = jnp.where(rel == dc, jnp.where(hi, up, lo), acc)
        valid = in_win
        if d == 0:
            valid = valid & hi
        if d == n_dr:
            valid = valid & jnp.logical_not(hi)
        o_ref[0, d] = jnp.where(valid, acc, NEG_INF)


def _rpb_table(rpb):
    nh, n_dr, n_dc = rpb.shape
    return pl.pallas_call(
        functools.partial(_rpb_table_kernel, n_dr=n_dr, n_dc=n_dc),
        grid=(nh,),
        in_specs=[pl.BlockSpec(memory_space=pltpu.SMEM)],
        out_specs=pl.BlockSpec((1, n_dr + 1, GRID_W, 2 * GRID_W), lambda h: (h, 0, 0, 0)),
        out_shape=jax.ShapeDtypeStruct((nh, n_dr + 1, GRID_W, 2 * GRID_W), F32),
        compiler_params=_cparams("arbitrary"),
        name="rpb_table",
    )(rpb.reshape(-1))


def _na_kernel(*refs, sx, sc, want_ctx):
    if want_ctx:
        q_x, k_x, v_x, q_c, k_c, v_c, tab, o_x, o_c = refs
    else:
        q_x, k_x, v_x, k_c, v_c, tab, o_x = refs
    w = GRID_W
    rows = sx // w
    qb = NA_QROWS * w
    kb = NA_KROWS * w
    n_blk = rows // NA_QROWS
    scale = NA_HEAD_DIM ** -0.5
    lane = lax.broadcasted_iota(jnp.int32, (1, 2 * NA_HEAD_DIM), 1)
    in_head = (lane < NA_HEAD_DIM, lane >= NA_HEAD_DIM)
    keyrow = lax.broadcasted_iota(jnp.int32, (1, kb), 1) // w
    kc = k_c[0]
    vc = v_c[0]
    n_tab = tab.shape[1]

    def softmax_pv(parts):
        m = parts[0][0].max(axis=-1, keepdims=True)
        for s, _ in parts[1:]:
            m = jnp.maximum(m, s.max(axis=-1, keepdims=True))
        acc, den = None, None
        for s, v in parts:
            e = jnp.exp(s - m)
            den = e.sum(axis=-1, keepdims=True) if den is None else den + e.sum(axis=-1, keepdims=True)
            pv = _dot(e.astype(BF16), v)
            acc = pv if acc is None else acc + pv
        return acc / den

    def block_body(rb, carry):
        ws = jnp.clip(NA_QROWS * rb - NA_KH // 2, 0, rows - NA_KROWS)
        q0 = pl.multiple_of(rb * qb, qb)
        k0 = pl.multiple_of(ws * w, w)
        qblk = q_x[0, pl.ds(q0, qb), :]
        kwin = k_x[0, pl.ds(k0, kb), :]
        vwin = v_x[0, pl.ds(k0, kb), :]
        out = jnp.zeros((qb, 2 * NA_HEAD_DIM), F32)
        for hh in range(2):
            qm = jnp.where(in_head[hh], qblk, jnp.zeros_like(qblk))
            s_loc = _dot_nt(qm, kwin) * scale
            s_ctx = _dot_nt(qm, kc) * scale
            pieces = []
            for rq in range(NA_QROWS):
                r = NA_QROWS * rb + rq
                rs = jnp.clip(r - NA_KH // 2, 0, rows - NA_KH)
                lo = rs - ws
                valid = (keyrow >= lo) & (keyrow < lo + NA_KH)
                blocks = []
                for ip in range(NA_KROWS // 2):
                    dr_lo = ws + 2 * ip - r + NA_KH - 1
                    blocks.append(tab[hh, jnp.clip(dr_lo + 1, 0, n_tab - 1)])
                bias = jnp.concatenate(blocks, axis=1)
                piece = s_loc[rq * w:(rq + 1) * w, :] + bias
                pieces.append(jnp.where(valid, piece, NEG_INF))
            s_loc = jnp.concatenate(pieces, axis=0)
            o = softmax_pv([(s_loc, vwin), (s_ctx, vc)])
            out = jnp.where(in_head[hh], o, out)
        o_x[0, pl.ds(q0, qb), :] = out.astype(o_x.dtype)
        return carry

    lax.fori_loop(0, n_blk, block_body, 0)

    if want_ctx:
        qc = q_c[0]
        out = jnp.zeros((sc, 2 * NA_HEAD_DIM), F32)
        for hh in range(2):
            qm = jnp.where(in_head[hh], qc, jnp.zeros_like(qc))
            o = softmax_pv([(_dot_nt(qm, kc) * scale, vc)])
            out = jnp.where(in_head[hh], o, out)
        o_c[0] = out.astype(o_c.dtype)


def _na_attention(q_x, k_x, v_x, q_c, k_c, v_c, table, want_ctx):
    bsz, sx, dim = q_x.shape
    sc = k_c.shape[1]
    pw = 2 * NA_HEAD_DIM
    n_pair = dim // pw

    def seq(s):
        return pl.BlockSpec((1, s, pw), lambda p, b: (b, 0, p))

    tspec = pl.BlockSpec((2,) + table.shape[1:], lambda p, b: (p, 0, 0, 0))
    if want_ctx:
        args = (q_x, k_x, v_x, q_c, k_c, v_c, table)
        in_specs = [seq(sx)] * 3 + [seq(sc)] * 3 + [tspec]
        out_specs = [seq(sx), seq(sc)]
        out_shape = [jax.ShapeDtypeStruct((bsz, sx, dim), BF16), jax.ShapeDtypeStruct((bsz, sc, dim), BF16)]
    else:
        args = (q_x, k_x, v_x, k_c, v_c, table)
        in_specs = [seq(sx)] * 3 + [seq(sc)] * 2 + [tspec]
        out_specs = [seq(sx)]
        out_shape = [jax.ShapeDtypeStruct((bsz, sx, dim), BF16)]
    res = pl.pallas_call(
        functools.partial(_na_kernel, sx=sx, sc=sc, want_ctx=want_ctx),
        grid=(n_pair, bsz),
        in_specs=in_specs, out_specs=out_specs, out_shape=out_shape,
        compiler_params=_cparams("arbitrary", "arbitrary"),
        name="na_attention",
    )(*args)
    return res if want_ctx else (res[0], None)


def _top_rows(s, k, exact, want_rank=True):
    n, tt = s.shape
    top_id = lax.broadcasted_iota(jnp.int32, (k, tt), 0)
    if exact:
        rowid = lax.broadcasted_iota(jnp.int32, (n, tt), 0).astype(F32)
    work = s
    top = jnp.zeros((k, tt), F32)
    rank = jnp.full((n, tt), float(k), F32) if want_rank else None
    for it in range(k):
        m = jnp.max(work, axis=0, keepdims=True)
        sel = work == m
        if exact:
            sel = rowid == jnp.min(jnp.where(sel, rowid, float(n)), axis=0, keepdims=True)
        top = jnp.where(top_id == it, m, top)
        if want_rank:
            rank = jnp.where(sel, float(it), rank)
        work = jnp.where(sel, -jnp.inf, work)
    picked = work == -jnp.inf
    n_sel = jnp.sum(jnp.where(picked, 1.0, 0.0), axis=0, keepdims=True)
    return top, rank, picked, n_sel


def _peer_route_kernel(h_ref, wq_ref, keys_ref, cnt_ref, rk_ref, e0_ref, e1_ref, s_s, top_s, rank_s, cnti_s, z_s):
    nk, k = PEER_KEYS, PEER_TOPK
    tt = h_ref.shape[1]
    sw = top_s.shape[2]
    q = _dot(h_ref[0], wq_ref[0]).astype(BF16)
    for z in range(2):
        sz = _dot_nt(keys_ref[z], q[:, z * nk:(z + 1) * nk])
        for t in range(tt // sw):
            s_s[z, t] = sz[:, t * sw:(t + 1) * sw]

    def pair_stage(t0, t1, exact):
        cand = jnp.concatenate([t0[0:1] + t1] + [t0[a:a + 1] + t1[0:8] for a in range(1, 8)]
                               + [t0[8:16] + t1[0:1]], axis=0)
        _, _, picked, n_sel = _top_rows(cand, k, exact, want_rank=False)
        pf = jnp.where(picked, 1.0, 0.0)
        z_sum = jnp.sum(pf * jnp.exp(cand - cand[0:1]), axis=0, keepdims=True)
        cnts = [jnp.sum(pf[0:k], axis=0, keepdims=True)]
        cnts += [jnp.sum(pf[k + 8 * (a - 1):k + 8 * a], axis=0, keepdims=True) for a in range(1, 8)]
        cnts += [pf[k + 56 + a:k + 57 + a] for a in range(8)]
        return cnts, z_sum, n_sel

    def strip(si, carry):
        cols = pl.ds(pl.multiple_of(si * sw, sw), sw)
        s0, s1 = s_s[0, si], s_s[1, si]

        t0, _, _, n0 = _top_rows(s0, k, False, want_rank=False)
        t1, rank1, _, n1 = _top_rows(s1, k, False)
        cnts, z_sum, n2 = pair_stage(t0, t1, False)
        cnt_i = jnp.zeros_like(s0)
        for a in range(k):
            cnt_i = jnp.where(s0 == t0[a:a + 1], cnts[a], cnt_i)
        top_s[0], top_s[1] = t0, t1
        rank_s[...] = rank1
        cnti_s[...] = cnt_i
        z_s[...] = z_sum
        ties = jnp.max(jnp.abs(n0 - float(k)) + jnp.abs(n1 - float(k)) + jnp.abs(n2 - float(k)))

        @pl.when(ties > 0.5)
        def _():
            t0, rank0, _, _ = _top_rows(s0, k, True)
            t1, rank1, _, _ = _top_rows(s1, k, True)
            cnts, z_sum, _ = pair_stage(t0, t1, True)
            cnt_i = jnp.zeros_like(s0)
            for a in range(k):
                cnt_i = jnp.where(rank0 == float(a), cnts[a], cnt_i)
            top_s[0], top_s[1] = t0, t1
            rank_s[...] = rank1
            cnti_s[...] = cnt_i
            z_s[...] = z_sum

        cnt_i = cnti_s[...]
        e0 = jnp.exp(s0 - top_s[0, 0:1]) / z_s[...]
        for c in range(nk // PEER_ICHUNK):
            cnt_ref[0, c, :, cols] = cnt_i[c * PEER_ICHUNK:(c + 1) * PEER_ICHUNK]
            e0_ref[0, c, :, cols] = e0[c * PEER_ICHUNK:(c + 1) * PEER_ICHUNK]
        rk_ref[0, :, cols] = rank_s[...].astype(rk_ref.dtype)
        e1_ref[0, :, cols] = jnp.exp(s1 - top_s[1, 0:1]).astype(e1_ref.dtype)
        return carry

    lax.fori_loop(0, tt // sw, strip, 0)


ROUTE_STRIP = 256


def _peer_route(hx, w_q, keys, tt):
    bn, sn, d = hx.shape
    nk, nh, ic = PEER_KEYS, PEER_HEADS, PEER_ICHUNK
    spec_i = pl.BlockSpec((1, nk // ic, ic, tt), lambda b, i, h: (b, 0, h, i))
    spec_j = pl.BlockSpec((1, nk, tt), lambda b, i, h: (b, h, i))
    shape_i = jax.ShapeDtypeStruct((bn, nk // ic, nh * ic, sn), F32)
    shape_j = jax.ShapeDtypeStruct((bn, nh * nk, sn), BF16)
    sw = min(ROUTE_STRIP, tt)
    return pl.pallas_call(
        _peer_route_kernel,
        grid=(bn, sn // tt, nh),
        in_specs=[pl.BlockSpec((1, tt, d), lambda b, i, h: (b, i, 0)),
                  pl.BlockSpec((1, d, 2 * nk), lambda b, i, h: (h, 0, 0)),
                  pl.BlockSpec((2, nk, keys.shape[2]), lambda b, i, h: (h, 0, 0))],
        out_specs=[spec_i, spec_j, spec_i, spec_j],
        out_shape=[shape_i, shape_j, shape_i, shape_j],
        scratch_shapes=[pltpu.VMEM((2, tt // sw, nk, sw), F32), pltpu.VMEM((2, PEER_TOPK, sw), F32),
                        pltpu.VMEM((nk, sw), F32), pltpu.VMEM((nk, sw), F32), pltpu.VMEM((1, sw), F32)],
        compiler_params=_cparams("arbitrary", "arbitrary", "arbitrary"),
        name="peer_route",
    )(hx, w_q, keys)


def _peer_dense_kernel(h_ref, cnt_ref, rk_ref, e0_ref, e1_ref, u_ref, vt_ref, x_ref, g_ref, o_ref,
                       acc_ref, act_ref, p_ref, hx_s, rk_s, e1_s, cnt_s, e0_s):
    nk, ic = PEER_KEYS, PEER_ICHUNK
    ck = pl.program_id(2)
    n_slab = act_ref.shape[0]
    sw = min(PEER_STRIP, n_slab * LANES)
    per = sw // LANES
    n_strip = n_slab // per
    il_group, j_group = 4, 2
    pk = _rows_per_word(BF16)

    @pl.when(ck == 0)
    def _():
        acc_ref[...] = jnp.zeros_like(acc_ref)
        hx_s[...] = h_ref[0]
        for t in range(n_slab):
            rk_s[t] = rk_ref[0, :, t * LANES:(t + 1) * LANES].astype(rk_s.dtype)
            e1_s[t] = e1_ref[0, :, t * LANES:(t + 1) * LANES].astype(e1_s.dtype)

    for t in range(n_slab):
        cnt_s[t] = cnt_ref[0, 0, :, t * LANES:(t + 1) * LANES]
        e0_s[t] = e0_ref[0, 0, :, t * LANES:(t + 1) * LANES]

    def activations(s):
        r0 = pl.multiple_of(s * sw, sw)
        a = _gelu_tanh(_dot_nt(_from_words(u_ref[...], BF16), hx_s[pl.ds(r0, sw), :])).astype(BF16)
        for k in range(per):
            act_ref[s * per + k] = _to_words(a[:, k * LANES:(k + 1) * LANES])

    def gate_weights(t):
        jr = nk // j_group
        for ig in range(ic // il_group):
            for jg in range(j_group):
                wgt = [None] * il_group
                for h in range(PEER_HEADS):
                    cnt8 = cnt_s[t, h * ic:(h + 1) * ic, :]
                    e08 = e0_s[t, h * ic:(h + 1) * ic, :]
                    rk = rk_s[t, h * nk + jg * jr:h * nk + (jg + 1) * jr, :].astype(BF16)
                    e1 = e1_s[t, h * nk + jg * jr:h * nk + (jg + 1) * jr, :].astype(BF16)
                    for g in range(il_group):
                        il = ig * il_group + g
                        cnt_row = jnp.broadcast_to(cnt8[il:il + 1], (jr, LANES)).astype(BF16)
                        e0_row = jnp.broadcast_to(e08[il:il + 1], (jr, LANES)).astype(BF16)
                        term = jnp.where(rk < cnt_row, e1 * e0_row, jnp.zeros_like(e1))
                        wgt[g] = term if wgt[g] is None else wgt[g] + term
                for g in range(il_group):
                    r0 = ((ig * il_group + g) * nk + jg * jr) // pk
                    act = _from_words(act_ref[t, r0:r0 + jr // pk, :], BF16)
                    p_ref[t, r0:r0 + jr // pk, :] = _to_words(wgt[g] * act)

    def combine(s):
        for k in range(per):
            gate_weights(s * per + k)
        p = _from_words(jnp.concatenate([p_ref[s * per + k] for k in range(per)], axis=1), BF16)
        y = _dot(_from_words(vt_ref[...], BF16), p)
        for k in range(per):
            acc_ref[s * per + k] += y[:, k * LANES:(k + 1) * LANES]

    activations(0)

    def strip(s, carry):
        activations(s + 1)
        combine(s)
        return carry

    lax.fori_loop(0, n_strip - 1, strip, 0)
    combine(n_strip - 1)

    @pl.when(ck == pl.num_programs(2) - 1)
    def _():
        for t in range(n_slab):
            rows = slice(t * LANES, (t + 1) * LANES)
            o_ref[0, rows, :] = x_ref[0, rows, :] + g_ref[0] * acc_ref[t].T


PEER_STRIP = 256


def _rows_per_word(dt):
    return 4 // jnp.dtype(dt).itemsize


def _to_words(x):
    return pltpu.bitcast(x, jnp.uint32) if x.dtype.itemsize == 2 else x


def _from_words(x, dt):
    return pltpu.bitcast(x, dt) if jnp.dtype(dt).itemsize == 2 else x


def _pack_rows(x):
    if x.dtype.itemsize != 2:
        return x
    *lead, m, n = x.shape
    return lax.bitcast_convert_type(jnp.swapaxes(x.reshape(*lead, m // 2, 2, n), -1, -2), jnp.uint32)


def _peer_dense(hx, route, u, v_t, x, gate, tt):
    bn, sn, d = hx.shape
    nk, nh, ic = PEER_KEYS, PEER_HEADS, PEER_ICHUNK
    ne = ic * nk
    pk = _rows_per_word(BF16)
    wdt = jnp.uint32 if pk == 2 else BF16
    n_chunk = u.shape[0] * pk // ne
    spec_i = pl.BlockSpec((1, 1, nh * ic, tt), lambda b, i, c: (b, c, 0, i))
    spec_j = pl.BlockSpec((1, nh * nk, tt), lambda b, i, c: (b, 0, i))
    tok = pl.BlockSpec((1, tt, d), lambda b, i, c: (b, i, 0))
    n_slab = tt // LANES
    return pl.pallas_call(
        _peer_dense_kernel,
        grid=(bn, sn // tt, n_chunk),
        in_specs=[tok, spec_i, spec_j, spec_i, spec_j,
                  pl.BlockSpec((ne // pk, d), lambda b, i, c: (c, 0)),
                  pl.BlockSpec((d // pk, ne), lambda b, i, c: (0, c)),
                  pl.BlockSpec((1, tt, d), lambda b, i, c: (b, i, 0), pipeline_mode=pl.Buffered(1)),
                  pl.BlockSpec((1, 1, d), lambda b, i, c: (b, 0, 0))],
        out_specs=tok,
        out_shape=jax.ShapeDtypeStruct((bn, sn, d), F32),
        scratch_shapes=[pltpu.VMEM((n_slab, d, LANES), F32), pltpu.VMEM((n_slab, ne // pk, LANES), wdt),
                        pltpu.VMEM((n_slab, ne // pk, LANES), wdt), pltpu.VMEM((tt, d), BF16),
                        pltpu.VMEM((n_slab, nh * nk, LANES), F32), pltpu.VMEM((n_slab, nh * nk, LANES), F32),
                        pltpu.VMEM((n_slab, nh * ic, LANES), F32), pltpu.VMEM((n_slab, nh * ic, LANES), F32)],
        compiler_params=_cparams("arbitrary", "arbitrary", "arbitrary"),
        name="peer_dense",
    )(hx, *route, u, v_t, x, gate)


def _peer(hx, x, gate, w_q, keys, u, v_t, tt=1024):
    tt = min(tt, hx.shape[1])
    route = _peer_route(hx, w_q, keys, tt)
    return _peer_dense(hx, route, u, v_t, x, gate, tt)


def _group_lanes(p, width):
    g = p.reshape(2, SSD_GROUPS, SSD_HPG).transpose(1, 0, 2).reshape(SSD_GROUPS, 2 * SSD_HPG)
    return jnp.pad(g, ((0, 0), (0, width - 2 * SSD_HPG))).reshape(SSD_GROUPS, 1, width)


def _block_diag(w, tile):
    two, nb, bd, _ = w.shape
    per = tile // bd
    w = w.reshape(two, nb // per, per, bd, bd)
    eye = jnp.eye(per, dtype=w.dtype)
    return jnp.einsum("dtpij,pq->dtpiqj", w, eye).reshape(two, nb // per, tile, tile)


def kernel(x, c, ctx, c_ctx, ada_w, ada_b, norm1_g, norm2_g, ev_w_in, ev_conv_w, ev_conv_b, ev_a_log,
           ev_dt_bias, ev_d, ev_ssd_norm_g, ev_lru_conv_w, ev_lru_conv_b, ev_lru_wa, ev_lru_ba, ev_lru_wx,
           ev_lru_bx, ev_lru_lam, ev_w_out, od_w_qkv, od_q_norm_g, od_k_norm_g, od_rpb, od_w_o,
           pe_w_q, pe_keys, pe_u, pe_v):
    bsz, sx, d = x.shape
    sc = ctx.shape[1]
    depth = ada_w.shape[0]

    n_c = bsz + 1
    rows = -(-n_c // SUBLANES) * SUBLANES
    c_all = jnp.concatenate([c, c_ctx[None], jnp.zeros((rows - n_c, d), F32)], axis=0)
    mods = _ada_mods(c_all, ada_w, ada_b).reshape(depth, rows, 6, d)

    ctx = ctx.reshape(1, bsz * sc, d)

    def per_batch(t):
        return t.reshape(bsz, sc, t.shape[-1])

    for layer in range(depth):
        last = layer == depth - 1
        j = layer // 2
        mod_x = mods[layer, :bsz]
        mod_c = mods[layer, bsz:bsz + 1]
        g1 = norm1_g[layer][None]
        g2 = norm2_g[layer][None]
        want_ctx = not last

        if layer % 2 == 0:
            w_in = ev_w_in[j]
            o_dt, o_xl = SSD_XBC, SSD_XBC + 2 * SSD_HEADS
            o_z = o_xl + LRU_WIDTH
            o_gate = o_z + SSD_INNER
            w_dt = w_in[:, o_dt:o_xl].reshape(d, 2, SSD_GROUPS, SSD_HPG).transpose(0, 2, 1, 3)
            w_dt = jnp.pad(w_dt.reshape(d, SSD_GROUPS, 2 * SSD_HPG), ((0, 0), (0, 0), (0, LANES - 2 * SSD_HPG)))
            ws = [w_in[:, :o_dt].astype(BF16), w_dt.reshape(d, SSD_GROUPS * LANES).astype(BF16),
                  w_in[:, o_xl:o_z].astype(BF16), w_in[:, o_z:o_gate].astype(BF16), w_in[:, o_gate:].astype(BF16)]
            dts = [BF16, F32, BF16, BF16, BF16]
            px = _nm_linear(x, g1, mod_x[:, 0:2], ws, dts)
            pc = [per_batch(t) for t in _nm_linear(ctx, g1, mod_c[:, 0:2], ws, dts)]
            y_ssd_x, y_ssd_c = _ssd_mixer(
                px[0], pc[0], px[1], pc[1], px[3], pc[3], ev_conv_w[j], ev_conv_b[j][None],
                _group_lanes(ev_a_log[j], LANES), _group_lanes(ev_dt_bias[j], LANES),
                jnp.repeat(ev_d[j], SSD_HEAD_DIM).reshape(SSD_GROUPS, 1, SSD_GW),
                ev_ssd_norm_g[j].reshape(SSD_GROUPS, 1, SSD_GW))
            y_lru_x, y_lru_c = _lru_mixer(
                px[2], pc[2], px[4], pc[4], ev_lru_conv_w[j], ev_lru_conv_b[j][None],
                _block_diag(ev_lru_wa[j], LRU_TILE).astype(BF16), _block_diag(ev_lru_wx[j], LRU_TILE).astype(BF16),
                ev_lru_ba[j], ev_lru_bx[j], ev_lru_lam[j])
            w_out = ev_w_out[j].astype(BF16)
            w_outs = [w_out[:SSD_INNER], w_out[SSD_INNER:]]
            acts_x = [y_ssd_x, y_lru_x]
            acts_c = [y_ssd_c.reshape(1, bsz * sc, -1), y_lru_c.reshape(1, bsz * sc, -1)]
        else:
            w_qkv = od_w_qkv[j].astype(BF16)
            nd = w_qkv.shape[1] // 3
            ws = [w_qkv[:, :nd], w_qkv[:, nd:2 * nd], w_qkv[:, 2 * nd:]]
            gains = [jnp.tile(od_q_norm_g[j], NA_HEADS)[None], jnp.tile(od_k_norm_g[j], NA_HEADS)[None], None]
            q_x, k_x, v_x = _nm_linear(x, g1, mod_x[:, 0:2], ws, [BF16] * 3, gains)
            q_c, k_c, v_c = [per_batch(t) for t in _nm_linear(ctx, g1, mod_c[:, 0:2], ws, [BF16] * 3, gains)]
            table = _rpb_table(od_rpb[j])
            o_x, o_c = _na_attention(q_x, k_x, v_x, q_c, k_c, v_c, table, want_ctx)
            w_outs = [od_w_o[j].astype(BF16)]
            acts_x = [o_x]
            acts_c = [o_c.reshape(1, bsz * sc, -1)] if want_ctx else None

        w_q = pe_w_q[layer].reshape(d, PEER_HEADS, 2 * PEER_KEYS).transpose(1, 0, 2).astype(BF16)
        keys = pe_keys[layer].reshape(2 * PEER_HEADS, PEER_KEYS, -1).astype(BF16)
        u = _pack_rows(pe_u[layer].astype(BF16))
        v_t = _pack_rows(pe_v[layer].T.astype(BF16))

        x, hx = _out_linear(acts_x, w_outs, x, mod_x[:, 2:5], g2)
        x = _peer(hx, x, mod_x[:, 5:6], w_q, keys, u, v_t)
        if want_ctx:
            ctx, hc = _out_linear(acts_c, w_outs, ctx, mod_c[:, 2:5], g2)
            ctx = _peer(hc, ctx, mod_c[:, 5:6], w_q, keys, u, v_t)
    return x
```

```python
import functools
import math

import jax
import jax.numpy as jnp
from jax import lax
from jax.experimental import pallas as pl
from jax.experimental.pallas import tpu as pltpu

F32 = jnp.float32
BF16 = jnp.bfloat16
HIGHEST = lax.Precision.HIGHEST

EPS = 1e-6
NEG_INF = -1e30

GRID_W = 64
SSD_HEADS = 16
SSD_HEAD_DIM = 64
SSD_GROUPS = 4
SSD_HPG = SSD_HEADS // SSD_GROUPS
SSD_STATE = 128
SSD_CHUNK = 128
SSD_INNER = SSD_HEADS * SSD_HEAD_DIM
SSD_GW = SSD_INNER // SSD_GROUPS
SSD_XBC = SSD_INNER + 2 * SSD_GROUPS * SSD_STATE
LRU_WIDTH = 1024
LRU_BLOCKS = 16
LRU_BLOCK_DIM = LRU_WIDTH // LRU_BLOCKS
LRU_C = 8.0
LRU_TILE = 256
NA_HEADS = 16
NA_HEAD_DIM = 64
NA_KH = 8
NA_KW = 16
NA_QROWS = 4
NA_KROWS = 12
PEER_HEADS = 8
PEER_KEYS = 128
PEER_TOPK = 16
PEER_ICHUNK = 8

LANES = 128
SUBLANES = 8
VMEM_LIMIT_BYTES = 56 * 1024 * 1024


def _cparams(*sem):
    return pltpu.CompilerParams(dimension_semantics=sem, vmem_limit_bytes=VMEM_LIMIT_BYTES)


def _silu(x):
    return x * (1.0 / (1.0 + jnp.exp(-x)))


def _sigmoid(x):
    return 1.0 / (1.0 + jnp.exp(-x))


def _softplus(x):
    return jnp.maximum(x, 0.0) + jnp.log(1.0 + jnp.exp(-jnp.abs(x)))


def _gelu_tanh(x):
    k0 = -2.0 * math.sqrt(2.0 / math.pi)
    return x / (1.0 + jnp.exp(x * (k0 + (k0 * 0.044715) * (x * x))))


def _rms_mod(x, g, shift, scale):
    ms = jnp.mean(x * x, axis=-1, keepdims=True)
    y = x * lax.rsqrt(ms + EPS) * g
    return y * (1.0 + scale) + shift


def _dot(a, b):
    return jnp.dot(a, b, preferred_element_type=F32)


def _dot_nt(a, b):
    return lax.dot_general(a, b, (((1,), (1,)), ((), ())), preferred_element_type=F32)


def _ada_kernel(c_ref, w_ref, b_ref, o_ref):
    s = _silu(c_ref[...])
    o_ref[0] = jnp.dot(s, w_ref[0], preferred_element_type=F32, precision=HIGHEST) + b_ref[0]


def _ada_mods(c_all, ada_w, ada_b):
    depth, d, n = ada_w.shape
    rows = c_all.shape[0]
    tn = 1536
    return pl.pallas_call(
        _ada_kernel,
        grid=(depth, n // tn),
        in_specs=[pl.BlockSpec((rows, d), lambda l, j: (0, 0)),
                  pl.BlockSpec((1, d, tn), lambda l, j: (l, 0, j)),
                  pl.BlockSpec((1, 1, tn), lambda l, j: (l, 0, j))],
        out_specs=pl.BlockSpec((1, rows, tn), lambda l, j: (l, 0, j)),
        out_shape=jax.ShapeDtypeStruct((depth, rows, n), F32),
        compiler_params=_cparams("arbitrary", "arbitrary"),
        name="ada_mods",
    )(c_all, ada_w, ada_b.reshape(depth, 1, n))


def _head_block_ones(n):
    r = lax.broadcasted_iota(jnp.int32, (n, n), 0) // NA_HEAD_DIM
    c = lax.broadcasted_iota(jnp.int32, (n, n), 1) // NA_HEAD_DIM
    return (r == c).astype(F32)


def _nm_linear_kernel(*refs, n_out, head_norm, tn):
    x_ref, g_ref, mod_ref = refs[:3]
    w_refs = refs[3:3 + n_out]
    hg_refs = refs[3 + n_out:3 + n_out + sum(head_norm)]
    o_refs = refs[3 + n_out + sum(head_norm):]
    h = _rms_mod(x_ref[0], g_ref[...], mod_ref[0, 0:1, :], mod_ref[0, 1:2, :]).astype(BF16)
    hg_i = 0
    for w_ref, o_ref, hn in zip(w_refs, o_refs, head_norm):
        n = w_ref.shape[1]
        for j in range(n // tn):
            y = _dot(h, w_ref[:, j * tn:(j + 1) * tn])
            if hn:
                ss = jnp.dot(y * y, _head_block_ones(tn), preferred_element_type=F32, precision=HIGHEST)
                y = y * lax.rsqrt(ss * (1.0 / NA_HEAD_DIM) + EPS) * hg_refs[hg_i][:, j * tn:(j + 1) * tn]
            o_ref[0, :, j * tn:(j + 1) * tn] = y.astype(o_ref.dtype)
        hg_i += hn


def _nm_linear(x, g, mod, ws, out_dtypes, head_gains=None, tm=512, tn=256):
    bn, sn, d = x.shape
    tm = min(tm, sn)
    n_out = len(ws)
    head_gains = head_gains or [None] * n_out
    head_norm = tuple(hg is not None for hg in head_gains)
    hgs = [hg for hg in head_gains if hg is not None]
    in_specs = [pl.BlockSpec((1, tm, d), lambda b, i: (b, i, 0)),
                pl.BlockSpec((1, d), lambda b, i: (0, 0)),
                pl.BlockSpec((1, 2, d), lambda b, i: (b, 0, 0))]
    in_specs += [pl.BlockSpec(w.shape, lambda b, i: (0, 0)) for w in ws]
    in_specs += [pl.BlockSpec(hg.shape, lambda b, i: (0, 0)) for hg in hgs]
    out_specs = [pl.BlockSpec((1, tm, w.shape[1]), lambda b, i: (b, i, 0)) for w in ws]
    out_shape = [jax.ShapeDtypeStruct((bn, sn, w.shape[1]), dt) for w, dt in zip(ws, out_dtypes)]
    return pl.pallas_call(
        functools.partial(_nm_linear_kernel, n_out=n_out, head_norm=head_norm, tn=tn),
        grid=(bn, sn // tm),
        in_specs=in_specs, out_specs=out_specs, out_shape=out_shape,
        compiler_params=_cparams("arbitrary", "arbitrary"),
        name="nm_linear",
    )(x, g, mod, *ws, *hgs)


def _out_linear_kernel(*refs, n_in):
    a_refs = refs[:n_in]
    w_refs = refs[n_in:2 * n_in]
    x_ref, mod_ref, g_ref, xo_ref, ho_ref = refs[2 * n_in:]
    y = _dot(a_refs[0][0], w_refs[0][...])
    for a_ref, w_ref in zip(a_refs[1:], w_refs[1:]):
        y = y + _dot(a_ref[0], w_ref[...])
    xn = x_ref[0] + mod_ref[0, 0:1, :] * y
    xo_ref[0] = xn
    ho_ref[0] = _rms_mod(xn, g_ref[...], mod_ref[0, 1:2, :], mod_ref[0, 2:3, :]).astype(BF16)


def _out_linear(acts, ws, x, mod, g, tm=512):
    bn, sn, d = x.shape
    tm = min(tm, sn)
    n_in = len(acts)
    in_specs = [pl.BlockSpec((1, tm, a.shape[2]), lambda b, i: (b, i, 0)) for a in acts]
    in_specs += [pl.BlockSpec(w.shape, lambda b, i: (0, 0)) for w in ws]
    in_specs += [pl.BlockSpec((1, tm, d), lambda b, i: (b, i, 0)),
                 pl.BlockSpec((1, 3, d), lambda b, i: (b, 0, 0)),
                 pl.BlockSpec((1, d), lambda b, i: (0, 0))]
    return pl.pallas_call(
        functools.partial(_out_linear_kernel, n_in=n_in),
        grid=(bn, sn // tm),
        in_specs=in_specs,
        out_specs=[pl.BlockSpec((1, tm, d), lambda b, i: (b, i, 0))] * 2,
        out_shape=[jax.ShapeDtypeStruct((bn, sn, d), F32), jax.ShapeDtypeStruct((bn, sn, d), BF16)],
        compiler_params=_cparams("arbitrary", "arbitrary"),
        name="out_linear",
    )(*acts, *ws, x, mod, g)


CONV_HALO = 16


def _conv_chunk(src_ref, s, seg_len, w, bias, rows=SSD_CHUNK):
    ncol = src_ref.shape[2]
    if s > 0:
        prev = src_ref[0, s - CONV_HALO:s, :].astype(F32)
    else:
        prev = jnp.zeros((CONV_HALO, ncol), F32)
    cur = src_ref[0, s:s + rows, :].astype(F32)
    if s + rows < seg_len:
        nxt = src_ref[0, s + rows:s + rows + CONV_HALO, :].astype(F32)
    else:
        nxt = jnp.zeros((CONV_HALO, ncol), F32)
    win = jnp.concatenate([prev, cur, nxt], axis=0)
    taps = w.shape[0]
    acc = bias
    for k in range(taps):
        off = CONV_HALO - taps // 2 + k
        acc = acc + win[off:off + rows, :] * w[k:k + 1, :]
    return acc


def _lane_head_expand(cols, width):
    nh = len(cols)
    hd = width // nh
    rows = cols[0].shape[0]
    lane_head = lax.broadcasted_iota(jnp.int32, (rows, width), 1) // hd
    out = jnp.broadcast_to(cols[nh - 1], (rows, width))
    for h in range(nh - 2, -1, -1):
        out = jnp.where(lane_head == h, jnp.broadcast_to(cols[h], (rows, width)), out)
    return out


def _ssd_kernel(xs_x, bm_x, cm_x, xs_c, bm_c, cm_c, dt_x, dt_c, z_x, z_c,
                cw_xs, cw_b, cw_c, cb_xs, cb_b, cb_c, alog_ref, dtb_ref, dsk_ref, ng_ref,
                y_x, y_c,
                xs_s, bm_s, cm_s, dt_s, y_s, st_s, cs_s, cst_s, *, sx, sc):
    q = SSD_CHUNK
    nc_c, nc_x = sc // q, sx // q
    nc = nc_c + nc_x
    gw = xs_s.shape[1]

    for seg_ref3, seg_len, base in (((xs_c, bm_c, cm_c), sc, 0), ((xs_x, bm_x, cm_x), sx, sc)):
        for ci in range(seg_len // q):
            s = ci * q
            for src, dst, w_ref, b_ref in zip(seg_ref3, (xs_s, bm_s, cm_s), (cw_xs, cw_b, cw_c),
                                              (cb_xs, cb_b, cb_c)):
                dst[base + s:base + s + q, :] = _silu(_conv_chunk(src, s, seg_len, w_ref[...], b_ref[...]))
    dt_s[0:sc, :] = _softplus(dt_c[0] + dtb_ref[0])
    dt_s[sc:sc + sx, :] = _softplus(dt_x[0] + dtb_ref[0])

    a_neg = -jnp.exp(alog_ref[0])
    row = lax.broadcasted_iota(jnp.int32, (q, q), 0)
    col = lax.broadcasted_iota(jnp.int32, (q, q), 1)
    tri = ((col <= row).astype(F32), (col >= row).astype(F32))
    keep = (col <= row, col >= row)
    lane_head = lax.broadcasted_iota(jnp.int32, (q, gw), 1) // SSD_HEAD_DIM

    st_s[...] = jnp.zeros_like(st_s)
    y_s[...] = jnp.zeros_like(y_s)

    for ci in range(nc):
        la = dt_s[ci * q:(ci + 1) * q, :] * a_neg
        for d in range(2):
            cs = jnp.dot(tri[d], la, preferred_element_type=F32, precision=HIGHEST)
            cs_s[d, ci * q:(ci + 1) * q, :] = cs
            cst_s[d, ci * q:(ci + 1) * q, :] = cs.T

    def chunk_body(i, carry):
        for d in range(2):
            if d == 0:
                ci = i
            else:
                ci = jnp.where(i < nc_c, nc_c - 1 - i, nc + nc_c - 1 - i)
            r0 = pl.multiple_of(ci * q, q)
            xs = xs_s[pl.ds(r0, q), :]
            bm = bm_s[pl.ds(r0, q), :]
            cm = cm_s[pl.ds(r0, q), :]
            dt = dt_s[pl.ds(r0, q), :]
            cs = cs_s[d, pl.ds(r0, q), :]
            cs_t = cst_s[d, pl.ds(r0, q), :]
            cb = _dot_nt(cm.astype(BF16), bm.astype(BF16))
            heads = [d * SSD_HPG + h for h in range(SSD_HPG)]
            dt_mat = _lane_head_expand([dt[:, c:c + 1] for c in heads], gw)
            cs_mat = _lane_head_expand([cs[:, c:c + 1] for c in heads], gw)
            xd = xs * dt_mat
            xd_b = xd.astype(BF16)
            y = jnp.zeros((q, gw), F32)
            for h, c in enumerate(heads):
                diff = cs[:, c:c + 1] - cs_t[c:c + 1, :]
                lmat = jnp.exp(jnp.where(keep[d], diff, NEG_INF))
                y = jnp.where(lane_head == h, _dot((cb * lmat).astype(BF16), xd_b), y)
            st = st_s[d]
            y = y + _dot(cm.astype(BF16), st.astype(BF16)) * jnp.exp(cs_mat)
            end = q - 1 if d == 0 else 0
            cs_end = cs_mat[end:end + 1, :]
            s_new = _dot(bm.T.astype(BF16), (xd * jnp.exp(cs_end - cs_mat)).astype(BF16))
            st_s[d] = st * jnp.exp(cs_end) + s_new
            y_s[pl.ds(r0, q), :] = y_s[pl.ds(r0, q), :] + y
        return carry

    lax.fori_loop(0, nc, chunk_body, 0)

    for ci in range(nc):
        s = ci * q
        if ci < nc_c:
            z = z_c[0, s:s + q, :]
        else:
            z = z_x[0, s - sc:s - sc + q, :]
        y = (y_s[s:s + q, :] + dsk_ref[0] * xs_s[s:s + q, :]) * _silu(z.astype(F32))
        ms = jnp.mean(y * y, axis=-1, keepdims=True)
        out = (y * lax.rsqrt(ms + EPS) * ng_ref[0]).astype(BF16)
        if ci < nc_c:
            y_c[0, s:s + q, :] = out
        else:
            y_x[0, s - sc:s - sc + q, :] = out


def _ssd_mixer(xbc_x, xbc_c, dt_x, dt_c, z_x, z_c, conv_w, conv_b, alog_g, dtb_g, dsk_g, ng_g):
    bsz, sx, _ = xbc_x.shape
    sc = xbc_c.shape[1]
    g, gw, n = SSD_GROUPS, SSD_GW, SSD_STATE
    nb = SSD_INNER // n
    taps = conv_w.shape[0]

    def seq(s, w, off):
        return pl.BlockSpec((1, s, w), lambda b, j, off=off: (b, 0, off + j))

    def par(r, w, off):
        return pl.BlockSpec((r, w), lambda b, j, off=off: (0, off + j))

    def grp(w):
        return pl.BlockSpec((1, 1, w), lambda b, j: (j, 0, 0))

    in_specs = [seq(sx, gw, 0), seq(sx, n, nb), seq(sx, n, nb + g),
                seq(sc, gw, 0), seq(sc, n, nb), seq(sc, n, nb + g),
                seq(sx, LANES, 0), seq(sc, LANES, 0), seq(sx, gw, 0), seq(sc, gw, 0),
                par(taps, gw, 0), par(taps, n, nb), par(taps, n, nb + g),
                par(1, gw, 0), par(1, n, nb), par(1, n, nb + g),
                grp(LANES), grp(LANES), grp(gw), grp(gw)]
    stot = sx + sc
    return pl.pallas_call(
        functools.partial(_ssd_kernel, sx=sx, sc=sc),
        grid=(bsz, g),
        in_specs=in_specs,
        out_specs=[seq(sx, gw, 0), seq(sc, gw, 0)],
        out_shape=[jax.ShapeDtypeStruct((bsz, sx, SSD_INNER), BF16),
                   jax.ShapeDtypeStruct((bsz, sc, SSD_INNER), BF16)],
        scratch_shapes=[pltpu.VMEM((stot, gw), F32), pltpu.VMEM((stot, n), F32), pltpu.VMEM((stot, n), F32),
                        pltpu.VMEM((stot, LANES), F32), pltpu.VMEM((stot, gw), F32), pltpu.VMEM((2, n, gw), F32),
                        pltpu.VMEM((2, stot, LANES), F32), pltpu.VMEM((2, stot, LANES), F32)],
        compiler_params=_cparams("arbitrary", "arbitrary"),
        name="ssd_mixer",
    )(xbc_x, xbc_x, xbc_x, xbc_c, xbc_c, xbc_c, dt_x, dt_c, z_x, z_c,
      conv_w, conv_w, conv_w, conv_b, conv_b, conv_b, alog_g, dtb_g, dsk_g, ng_g)


def _lru_kernel(xl_x, xl_c, gt_x, gt_c, cw, cb, wa, wx, ba, bx, lam, y_x, y_c,
                xr_s, a_s, b_s, y_s, *, sx, sc):
    q = SSD_CHUNK
    stot = sx + sc
    w = xr_s.shape[1]
    for src, seg_len, base in ((xl_c, sc, 0), (xl_x, sx, sc)):
        for ci in range(seg_len // q):
            s = ci * q
            xr_s[base + s:base + s + q, :] = _conv_chunk(src, s, seg_len, cw[...], cb[...])

    ng = stot // SUBLANES
    ng_c = sc // SUBLANES
    sub = lax.broadcasted_iota(jnp.int32, (SUBLANES, w), 0)
    rt = q
    for d in range(2):
        nsp = _softplus(-lam[d:d + 1, :])
        for ci in range(stot // rt):
            s = ci * rt
            xr = xr_s[s:s + rt, :]
            xb = xr.astype(BF16)
            r = _sigmoid(_dot(xb, wa[d, 0]) + ba[d:d + 1, :])
            ig = _sigmoid(_dot(xb, wx[d, 0]) + bx[d:d + 1, :])
            a = jnp.exp(-LRU_C * r * nsp)
            a_s[d, s:s + rt, :] = a
            b_s[d, s:s + rt, :] = jnp.sqrt(1.0 - a * a) * (ig * xr)

    y_s[...] = jnp.zeros_like(y_s)

    def group_body(k, carries):
        new = []
        for d in range(2):
            if d == 0:
                gi = k
            else:
                gi = jnp.where(k < ng_c, ng_c - 1 - k, ng + ng_c - 1 - k)
            r0 = pl.multiple_of(gi * SUBLANES, SUBLANES)
            a = a_s[d, pl.ds(r0, SUBLANES), :]
            b = b_s[d, pl.ds(r0, SUBLANES), :]
            for sh in (1, 2, 4):
                if d == 0:
                    valid = sub >= sh
                    a_sh = pltpu.roll(a, sh, axis=0)
                    b_sh = pltpu.roll(b, sh, axis=0)
                else:
                    valid = sub < SUBLANES - sh
                    a_sh = pltpu.roll(a, SUBLANES - sh, axis=0)
                    b_sh = pltpu.roll(b, SUBLANES - sh, axis=0)
                b = jnp.where(valid, a * b_sh + b, b)
                a = jnp.where(valid, a * a_sh, a)
            h = a * carries[d] + b
            y_s[pl.ds(r0, SUBLANES), :] = y_s[pl.ds(r0, SUBLANES), :] + h
            last = h[SUBLANES - 1:SUBLANES, :] if d == 0 else h[0:1, :]
            new.append(jnp.broadcast_to(last, (SUBLANES, w)))
        return tuple(new)

    zero = jnp.zeros((SUBLANES, w), F32)
    lax.fori_loop(0, ng, group_body, (zero, zero))

    for ci in range(stot // rt):
        s = ci * rt
        if s < sc:
            gate = gt_c[0, s:s + rt, :]
        else:
            gate = gt_x[0, s - sc:s - sc + rt, :]
        out = (y_s[s:s + rt, :] * _gelu_tanh(gate.astype(F32))).astype(BF16)
        if s < sc:
            y_c[0, s:s + rt, :] = out
        else:
            y_x[0, s - sc:s - sc + rt, :] = out


def _lru_mixer(xl_x, xl_c, gt_x, gt_c, conv_w, conv_b, wa_bd, wx_bd, ba, bx, lam):
    bsz, sx, width = xl_x.shape
    sc = xl_c.shape[1]
    w = LRU_TILE
    taps = conv_w.shape[0]

    def seq(s):
        return pl.BlockSpec((1, s, w), lambda b, j: (b, 0, j))

    def par(r):
        return pl.BlockSpec((r, w), lambda b, j: (0, j))

    wspec = pl.BlockSpec((2, 1, w, w), lambda b, j: (0, j, 0, 0))
    stot = sx + sc
    return pl.pallas_call(
        functools.partial(_lru_kernel, sx=sx, sc=sc),
        grid=(bsz, width // w),
        in_specs=[seq(sx), seq(sc), seq(sx), seq(sc), par(taps), par(1), wspec, wspec, par(2), par(2), par(2)],
        out_specs=[seq(sx), seq(sc)],
        out_shape=[jax.ShapeDtypeStruct((bsz, sx, width), BF16), jax.ShapeDtypeStruct((bsz, sc, width), BF16)],
        scratch_shapes=[pltpu.VMEM((stot, w), F32), pltpu.VMEM((2, stot, w), F32), pltpu.VMEM((2, stot, w), F32),
                        pltpu.VMEM((stot, w), F32)],
        compiler_params=_cparams("arbitrary", "arbitrary"),
        name="lru_mixer",
    )(xl_x, xl_c, gt_x, gt_c, conv_w, conv_b, wa_bd, wx_bd, ba, bx, lam)


def _rpb_table_kernel(rpb_ref, o_ref, *, n_dr, n_dc):
    h = pl.program_id(0)
    w = GRID_W
    qcol = lax.broadcasted_iota(jnp.int32, (w, 2 * w), 0)
    lane = lax.broadcasted_iota(jnp.int32, (w, 2 * w), 1)
    kcol = lane % w
    hi = lane >= w
    rel = kcol - qcol + (NA_KW - 1)
    cstart = jnp.clip(qcol - NA_KW // 2, 0, w - NA_KW)
    in_win = (kcol >= cstart) & (kcol < cstart + NA_KW)
    for d in range(n_dr + 1):
        acc = jnp.full((w, 2 * w), NEG_INF, F32)
        for dc in range(n_dc):
            lo = rpb_ref[(h * n_dr + d - 1) * n_dc + dc] if d >= 1 else NEG_INF
            up = rpb_ref[(h * n_dr + d) * n_dc + dc] if d < n_dr else NEG_INF
            acc = jnp.where(rel == dc, jnp.where(hi, up, lo), acc)
        valid = in_win
        if d == 0:
            valid = valid & hi
        if d == n_dr:
            valid = valid & jnp.logical_not(hi)
        o_ref[0, d] = jnp.where(valid, acc, NEG_INF)


def _rpb_table(rpb):
    nh, n_dr, n_dc = rpb.shape
    return pl.pallas_call(
        functools.partial(_rpb_table_kernel, n_dr=n_dr, n_dc=n_dc),
        grid=(nh,),
        in_specs=[pl.BlockSpec(memory_space=pltpu.SMEM)],
        out_specs=pl.BlockSpec((1, n_dr + 1, GRID_W, 2 * GRID_W), lambda h: (h, 0, 0, 0)),
        out_shape=jax.ShapeDtypeStruct((nh, n_dr + 1, GRID_W, 2 * GRID_W), F32),
        compiler_params=_cparams("arbitrary"),
        name="rpb_table",
    )(rpb.reshape(-1))


def _na_kernel(*refs, sx, sc, want_ctx):
    if want_ctx:
        q_x, k_x, v_x, q_c, k_c, v_c, tab, o_x, o_c = refs
    else:
        q_x, k_x, v_x, k_c, v_c, tab, o_x = refs
    w = GRID_W
    rows = sx // w
    qb = NA_QROWS * w
    kb = NA_KROWS * w
    n_blk = rows // NA_QROWS
    scale = NA_HEAD_DIM ** -0.5
    lane = lax.broadcasted_iota(jnp.int32, (1, 2 * NA_HEAD_DIM), 1)
    in_head = (lane < NA_HEAD_DIM, lane >= NA_HEAD_DIM)
    keyrow = lax.broadcasted_iota(jnp.int32, (1, kb), 1) // w
    kc = k_c[0]
    vc = v_c[0]
    n_tab = tab.shape[1]

    def softmax_pv(parts):
        m = parts[0][0].max(axis=-1, keepdims=True)
        for s, _ in parts[1:]:
            m = jnp.maximum(m, s.max(axis=-1, keepdims=True))
        acc, den = None, None
        for s, v in parts:
            e = jnp.exp(s - m)
            den = e.sum(axis=-1, keepdims=True) if den is None else den + e.sum(axis=-1, keepdims=True)
            pv = _dot(e.astype(BF16), v)
            acc = pv if acc is None else acc + pv
        return acc / den

    def block_body(rb, carry):
        ws = jnp.clip(NA_QROWS * rb - NA_KH // 2, 0, rows - NA_KROWS)
        q0 = pl.multiple_of(rb * qb, qb)
        k0 = pl.multiple_of(ws * w, w)
        qblk = q_x[0, pl.ds(q0, qb), :]
        kwin = k_x[0, pl.ds(k0, kb), :]
        vwin = v_x[0, pl.ds(k0, kb), :]
        out = jnp.zeros((qb, 2 * NA_HEAD_DIM), F32)
        for hh in range(2):
            qm = jnp.where(in_head[hh], qblk, jnp.zeros_like(qblk))
            s_loc = _dot_nt(qm, kwin) * scale
            s_ctx = _dot_nt(qm, kc) * scale
            pieces = []
            for rq in range(NA_QROWS):
                r = NA_QROWS * rb + rq
                rs = jnp.clip(r - NA_KH // 2, 0, rows - NA_KH)
                lo = rs - ws
                valid = (keyrow >= lo) & (keyrow < lo + NA_KH)
                blocks = []
                for ip in range(NA_KROWS // 2):
                    dr_lo = ws + 2 * ip - r + NA_KH - 1
                    blocks.append(tab[hh, jnp.clip(dr_lo + 1, 0, n_tab - 1)])
                bias = jnp.concatenate(blocks, axis=1)
                piece = s_loc[rq * w:(rq + 1) * w, :] + bias
                pieces.append(jnp.where(valid, piece, NEG_INF))
            s_loc = jnp.concatenate(pieces, axis=0)
            o = softmax_pv([(s_loc, vwin), (s_ctx, vc)])
            out = jnp.where(in_head[hh], o, out)
        o_x[0, pl.ds(q0, qb), :] = out.astype(o_x.dtype)
        return carry

    lax.fori_loop(0, n_blk, block_body, 0)

    if want_ctx:
        qc = q_c[0]
        out = jnp.zeros((sc, 2 * NA_HEAD_DIM), F32)
        for hh in range(2):
            qm = jnp.where(in_head[hh], qc, jnp.zeros_like(qc))
            o = softmax_pv([(_dot_nt(qm, kc) * scale, vc)])
            out = jnp.where(in_head[hh], o, out)
        o_c[0] = out.astype(o_c.dtype)


def _na_attention(q_x, k_x, v_x, q_c, k_c, v_c, table, want_ctx):
    bsz, sx, dim = q_x.shape
    sc = k_c.shape[1]
    pw = 2 * NA_HEAD_DIM
    n_pair = dim // pw

    def seq(s):
        return pl.BlockSpec((1, s, pw), lambda p, b: (b, 0, p))

    tspec = pl.BlockSpec((2,) + table.shape[1:], lambda p, b: (p, 0, 0, 0))
    if want_ctx:
        args = (q_x, k_x, v_x, q_c, k_c, v_c, table)
        in_specs = [seq(sx)] * 3 + [seq(sc)] * 3 + [tspec]
        out_specs = [seq(sx), seq(sc)]
        out_shape = [jax.ShapeDtypeStruct((bsz, sx, dim), BF16), jax.ShapeDtypeStruct((bsz, sc, dim), BF16)]
    else:
        args = (q_x, k_x, v_x, k_c, v_c, table)
        in_specs = [seq(sx)] * 3 + [seq(sc)] * 2 + [tspec]
        out_specs = [seq(sx)]
        out_shape = [jax.ShapeDtypeStruct((bsz, sx, dim), BF16)]
    res = pl.pallas_call(
        functools.partial(_na_kernel, sx=sx, sc=sc, want_ctx=want_ctx),
        grid=(n_pair, bsz),
        in_specs=in_specs, out_specs=out_specs, out_shape=out_shape,
        compiler_params=_cparams("arbitrary", "arbitrary"),
        name="na_attention",
    )(*args)
    return res if want_ctx else (res[0], None)


def _top_rows(s, k, exact, want_rank=True):
    n, tt = s.shape
    top_id = lax.broadcasted_iota(jnp.int32, (k, tt), 0)
    if exact:
        rowid = lax.broadcasted_iota(jnp.int32, (n, tt), 0).astype(F32)
    work = s
    top = jnp.zeros((k, tt), F32)
    rank = jnp.full((n, tt), float(k), F32) if want_rank else None
    for it in range(k):
        m = jnp.max(work, axis=0, keepdims=True)
        sel = work == m
        if exact:
            sel = rowid == jnp.min(jnp.where(sel, rowid, float(n)), axis=0, keepdims=True)
        top = jnp.where(top_id == it, m, top)
        if want_rank:
            rank = jnp.where(sel, float(it), rank)
        work = jnp.where(sel, -jnp.inf, work)
    picked = work == -jnp.inf
    n_sel = jnp.sum(jnp.where(picked, 1.0, 0.0), axis=0, keepdims=True)
    return top, rank, picked, n_sel


def _peer_route_kernel(h_ref, wq_ref, keys_ref, cnt_ref, rk_ref, e0_ref, e1_ref, s_s, top_s, rank_s, cnti_s, z_s):
    nk, k = PEER_KEYS, PEER_TOPK
    tt = h_ref.shape[1]
    sw = top_s.shape[2]
    q = _dot(h_ref[0], wq_ref[0]).astype(BF16)
    for z in range(2):
        sz = _dot_nt(keys_ref[z], q[:, z * nk:(z + 1) * nk])
        for t in range(tt // sw):
            s_s[z, t] = sz[:, t * sw:(t + 1) * sw]

    def pair_stage(t0, t1, exact):
        cand = jnp.concatenate([t0[0:1] + t1] + [t0[a:a + 1] + t1[0:8] for a in range(1, 8)]
                               + [t0[8:16] + t1[0:1]], axis=0)
        _, _, picked, n_sel = _top_rows(cand, k, exact, want_rank=False)
        pf = jnp.where(picked, 1.0, 0.0)
        z_sum = jnp.sum(pf * jnp.exp(cand - cand[0:1]), axis=0, keepdims=True)
        cnts = [jnp.sum(pf[0:k], axis=0, keepdims=True)]
        cnts += [jnp.sum(pf[k + 8 * (a - 1):k + 8 * a], axis=0, keepdims=True) for a in range(1, 8)]
        cnts += [pf[k + 56 + a:k + 57 + a] for a in range(8)]
        return cnts, z_sum, n_sel

    def strip(si, carry):
        cols = pl.ds(pl.multiple_of(si * sw, sw), sw)
        s0, s1 = s_s[0, si], s_s[1, si]

        t0, _, _, n0 = _top_rows(s0, k, False, want_rank=False)
        t1, rank1, _, n1 = _top_rows(s1, k, False)
        cnts, z_sum, n2 = pair_stage(t0, t1, False)
        cnt_i = jnp.zeros_like(s0)
        for a in range(k):
            cnt_i = jnp.where(s0 == t0[a:a + 1], cnts[a], cnt_i)
        top_s[0], top_s[1] = t0, t1
        rank_s[...] = rank1
        cnti_s[...] = cnt_i
        z_s[...] = z_sum
        ties = jnp.max(jnp.abs(n0 - float(k)) + jnp.abs(n1 - float(k)) + jnp.abs(n2 - float(k)))

        @pl.when(ties > 0.5)
        def _():
            t0, rank0, _, _ = _top_rows(s0, k, True)
            t1, rank1, _, _ = _top_rows(s1, k, True)
            cnts, z_sum, _ = pair_stage(t0, t1, True)
            cnt_i = jnp.zeros_like(s0)
            for a in range(k):
                cnt_i = jnp.where(rank0 == float(a), cnts[a], cnt_i)
            top_s[0], top_s[1] = t0, t1
            rank_s[...] = rank1
            cnti_s[...] = cnt_i
            z_s[...] = z_sum

        cnt_i = cnti_s[...]
        e0 = jnp.exp(s0 - top_s[0, 0:1]) / z_s[...]
        for c in range(nk // PEER_ICHUNK):
            cnt_ref[0, c, :, cols] = cnt_i[c * PEER_ICHUNK:(c + 1) * PEER_ICHUNK]
            e0_ref[0, c, :, cols] = e0[c * PEER_ICHUNK:(c + 1) * PEER_ICHUNK]
        rk_ref[0, :, cols] = rank_s[...].astype(rk_ref.dtype)
        e1_ref[0, :, cols] = jnp.exp(s1 - top_s[1, 0:1]).astype(e1_ref.dtype)
        return carry

    lax.fori_loop(0, tt // sw, strip, 0)


ROUTE_STRIP = 256


def _peer_route(hx, w_q, keys, tt):
    bn, sn, d = hx.shape
    nk, nh, ic = PEER_KEYS, PEER_HEADS, PEER_ICHUNK
    spec_i = pl.BlockSpec((1, nk // ic, ic, tt), lambda b, i, h: (b, 0, h, i))
    spec_j = pl.BlockSpec((1, nk, tt), lambda b, i, h: (b, h, i))
    shape_i = jax.ShapeDtypeStruct((bn, nk // ic, nh * ic, sn), F32)
    shape_j = jax.ShapeDtypeStruct((bn, nh * nk, sn), BF16)
    sw = min(ROUTE_STRIP, tt)
    return pl.pallas_call(
        _peer_route_kernel,
        grid=(bn, sn // tt, nh),
        in_specs=[pl.BlockSpec((1, tt, d), lambda b, i, h: (b, i, 0)),
                  pl.BlockSpec((1, d, 2 * nk), lambda b, i, h: (h, 0, 0)),
                  pl.BlockSpec((2, nk, keys.shape[2]), lambda b, i, h: (h, 0, 0))],
        out_specs=[spec_i, spec_j, spec_i, spec_j],
        out_shape=[shape_i, shape_j, shape_i, shape_j],
        scratch_shapes=[pltpu.VMEM((2, tt // sw, nk, sw), F32), pltpu.VMEM((2, PEER_TOPK, sw), F32),
                        pltpu.VMEM((nk, sw), F32), pltpu.VMEM((nk, sw), F32), pltpu.VMEM((1, sw), F32)],
        compiler_params=_cparams("arbitrary", "arbitrary", "arbitrary"),
        name="peer_route",
    )(hx, w_q, keys)


def _peer_dense_kernel(h_ref, cnt_ref, rk_ref, e0_ref, e1_ref, u_ref, vt_ref, x_ref, g_ref, o_ref,
                       acc_ref, act_ref, p_ref, hx_s, rk_s, e1_s, cnt_s, e0_s):
    nk, ic = PEER_KEYS, PEER_ICHUNK
    ck = pl.program_id(2)
    n_slab = act_ref.shape[0]
    sw = min(PEER_STRIP, n_slab * LANES)
    per = sw // LANES
    n_strip = n_slab // per
    il_group, j_group = 4, 2
    pk = _rows_per_word(BF16)

    @pl.when(ck == 0)
    def _():
        acc_ref[...] = jnp.zeros_like(acc_ref)
        hx_s[...] = h_ref[0]
        for t in range(n_slab):
            rk_s[t] = _to_words(rk_ref[0, :, t * LANES:(t + 1) * LANES])
            e1_s[t] = _to_words(e1_ref[0, :, t * LANES:(t + 1) * LANES])

    for t in range(n_slab):
        cnt_s[t] = cnt_ref[0, 0, :, t * LANES:(t + 1) * LANES]
        e0_s[t] = e0_ref[0, 0, :, t * LANES:(t + 1) * LANES]

    def activations(s):
        r0 = pl.multiple_of(s * sw, sw)
        a = _gelu_tanh(_dot_nt(_from_words(u_ref[...], BF16), hx_s[pl.ds(r0, sw), :])).astype(BF16)
        for k in range(per):
            act_ref[s * per + k] = _to_words(a[:, k * LANES:(k + 1) * LANES])

    def gate_weights(t):
        jr = nk // j_group
        for ig in range(ic // il_group):
            for jg in range(j_group):
                wgt = [None] * il_group
                for h in range(PEER_HEADS):
                    cnt8 = cnt_s[t, h * ic:(h + 1) * ic, :]
                    e08 = e0_s[t, h * ic:(h + 1) * ic, :]
                    j0 = (h * nk + jg * jr) // pk
                    rk = _from_words(rk_s[t, j0:j0 + jr // pk, :], BF16)
                    e1 = _from_words(e1_s[t, j0:j0 + jr // pk, :], BF16)
                    for g in range(il_group):
                        il = ig * il_group + g
                        cnt_row = jnp.broadcast_to(cnt8[il:il + 1], (jr, LANES)).astype(BF16)
                        e0_row = jnp.broadcast_to(e08[il:il + 1], (jr, LANES)).astype(BF16)
                        term = jnp.where(rk < cnt_row, e1 * e0_row, jnp.zeros_like(e1))
                        wgt[g] = term if wgt[g] is None else wgt[g] + term
                for g in range(il_group):
                    r0 = ((ig * il_group + g) * nk + jg * jr) // pk
                    act = _from_words(act_ref[t, r0:r0 + jr // pk, :], BF16)
                    p_ref[t, r0:r0 + jr // pk, :] = _to_words(wgt[g] * act)

    def combine(s):
        for k in range(per):
            gate_weights(s * per + k)
        p = _from_words(jnp.concatenate([p_ref[s * per + k] for k in range(per)], axis=1), BF16)
        y = _dot(_from_words(vt_ref[...], BF16), p)
        for k in range(per):
            acc_ref[s * per + k] += y[:, k * LANES:(k + 1) * LANES]

    activations(0)

    def strip(s, carry):
        activations(s + 1)
        combine(s)
        return carry

    lax.fori_loop(0, n_strip - 1, strip, 0)
    combine(n_strip - 1)

    @pl.when(ck == pl.num_programs(2) - 1)
    def _():
        for t in range(n_slab):
            rows = slice(t * LANES, (t + 1) * LANES)
            o_ref[0, rows, :] = x_ref[0, rows, :] + g_ref[0] * acc_ref[t].T


PEER_STRIP = 256


def _rows_per_word(dt):
    return 4 // jnp.dtype(dt).itemsize


def _to_words(x):
    return pltpu.bitcast(x, jnp.uint32) if x.dtype.itemsize == 2 else x


def _from_words(x, dt):
    return pltpu.bitcast(x, dt) if jnp.dtype(dt).itemsize == 2 else x


def _pack_rows(x):
    if x.dtype.itemsize != 2:
        return x
    *lead, m, n = x.shape
    return lax.bitcast_convert_type(jnp.swapaxes(x.reshape(*lead, m // 2, 2, n), -1, -2), jnp.uint32)


def _pack_weight_kernel(w_ref, o_ref, *, transpose):
    w = w_ref[...].T if transpose else w_ref[...]
    o_ref[...] = _to_words(w.astype(BF16))


def _pack_weight(w, transpose):
    rows, cols = w.shape
    pk = _rows_per_word(BF16)
    blk = 1024
    if transpose:
        out_shape, out_spec = (cols // pk, rows), pl.BlockSpec((cols // pk, blk), lambda i: (0, i))
    else:
        out_shape, out_spec = (rows // pk, cols), pl.BlockSpec((blk // pk, cols), lambda i: (i, 0))
    return pl.pallas_call(
        functools.partial(_pack_weight_kernel, transpose=transpose),
        grid=(rows // blk,),
        in_specs=[pl.BlockSpec((blk, cols), lambda i: (i, 0))],
        out_specs=out_spec,
        out_shape=jax.ShapeDtypeStruct(out_shape, jnp.uint32 if pk == 2 else BF16),
        compiler_params=_cparams("arbitrary"),
        name="pack_weight",
    )(w)


def _peer_dense(hx, route, u, v_t, x, gate, tt):
    bn, sn, d = hx.shape
    nk, nh, ic = PEER_KEYS, PEER_HEADS, PEER_ICHUNK
    ne = ic * nk
    pk = _rows_per_word(BF16)
    wdt = jnp.uint32 if pk == 2 else BF16
    n_chunk = u.shape[0] * pk // ne
    spec_i = pl.BlockSpec((1, 1, nh * ic, tt), lambda b, i, c: (b, c, 0, i))
    spec_j = pl.BlockSpec((1, nh * nk, tt), lambda b, i, c: (b, 0, i))
    tok = pl.BlockSpec((1, tt, d), lambda b, i, c: (b, i, 0))
    n_slab = tt // LANES
    return pl.pallas_call(
        _peer_dense_kernel,
        grid=(bn, sn // tt, n_chunk),
        in_specs=[tok, spec_i, spec_j, spec_i, spec_j,
                  pl.BlockSpec((ne // pk, d), lambda b, i, c: (c, 0)),
                  pl.BlockSpec((d // pk, ne), lambda b, i, c: (0, c)),
                  pl.BlockSpec((1, tt, d), lambda b, i, c: (b, i, 0), pipeline_mode=pl.Buffered(1)),
                  pl.BlockSpec((1, 1, d), lambda b, i, c: (b, 0, 0))],
        out_specs=tok,
        out_shape=jax.ShapeDtypeStruct((bn, sn, d), F32),
        scratch_shapes=[pltpu.VMEM((n_slab, d, LANES), F32), pltpu.VMEM((n_slab, ne // pk, LANES), wdt),
                        pltpu.VMEM((n_slab, ne // pk, LANES), wdt), pltpu.VMEM((tt, d), BF16),
                        pltpu.VMEM((n_slab, nh * nk // pk, LANES), wdt),
                        pltpu.VMEM((n_slab, nh * nk // pk, LANES), wdt),
                        pltpu.VMEM((n_slab, nh * ic, LANES), F32), pltpu.VMEM((n_slab, nh * ic, LANES), F32)],
        compiler_params=_cparams("arbitrary", "arbitrary", "arbitrary"),
        name="peer_dense",
    )(hx, *route, u, v_t, x, gate)


def _peer(hx, x, gate, w_q, keys, u, v_t, tt=1024):
    tt = min(tt, hx.shape[1])
    route = _peer_route(hx, w_q, keys, tt)
    return _peer_dense(hx, route, u, v_t, x, gate, tt)


def _group_lanes(p, width):
    g = p.reshape(2, SSD_GROUPS, SSD_HPG).transpose(1, 0, 2).reshape(SSD_GROUPS, 2 * SSD_HPG)
    return jnp.pad(g, ((0, 0), (0, width - 2 * SSD_HPG))).reshape(SSD_GROUPS, 1, width)


def _block_diag(w, tile):
    two, nb, bd, _ = w.shape
    per = tile // bd
    w = w.reshape(two, nb // per, per, bd, bd)
    eye = jnp.eye(per, dtype=w.dtype)
    return jnp.einsum("dtpij,pq->dtpiqj", w, eye).reshape(two, nb // per, tile, tile)


def kernel(x, c, ctx, c_ctx, ada_w, ada_b, norm1_g, norm2_g, ev_w_in, ev_conv_w, ev_conv_b, ev_a_log,
           ev_dt_bias, ev_d, ev_ssd_norm_g, ev_lru_conv_w, ev_lru_conv_b, ev_lru_wa, ev_lru_ba, ev_lru_wx,
           ev_lru_bx, ev_lru_lam, ev_w_out, od_w_qkv, od_q_norm_g, od_k_norm_g, od_rpb, od_w_o,
           pe_w_q, pe_keys, pe_u, pe_v):
    bsz, sx, d = x.shape
    sc = ctx.shape[1]
    depth = ada_w.shape[0]

    n_c = bsz + 1
    rows = -(-n_c // SUBLANES) * SUBLANES
    c_all = jnp.concatenate([c, c_ctx[None], jnp.zeros((rows - n_c, d), F32)], axis=0)
    mods = _ada_mods(c_all, ada_w, ada_b).reshape(depth, rows, 6, d)

    ctx = ctx.reshape(1, bsz * sc, d)

    def per_batch(t):
        return t.reshape(bsz, sc, t.shape[-1])

    for layer in range(depth):
        last = layer == depth - 1
        j = layer // 2
        mod_x = mods[layer, :bsz]
        mod_c = mods[layer, bsz:bsz + 1]
        g1 = norm1_g[layer][None]
        g2 = norm2_g[layer][None]
        want_ctx = not last

        if layer % 2 == 0:
            w_in = ev_w_in[j]
            o_dt, o_xl = SSD_XBC, SSD_XBC + 2 * SSD_HEADS
            o_z = o_xl + LRU_WIDTH
            o_gate = o_z + SSD_INNER
            w_dt = w_in[:, o_dt:o_xl].reshape(d, 2, SSD_GROUPS, SSD_HPG).transpose(0, 2, 1, 3)
            w_dt = jnp.pad(w_dt.reshape(d, SSD_GROUPS, 2 * SSD_HPG), ((0, 0), (0, 0), (0, LANES - 2 * SSD_HPG)))
            ws = [w_in[:, :o_dt].astype(BF16), w_dt.reshape(d, SSD_GROUPS * LANES).astype(BF16),
                  w_in[:, o_xl:o_z].astype(BF16), w_in[:, o_z:o_gate].astype(BF16), w_in[:, o_gate:].astype(BF16)]
            dts = [BF16, F32, BF16, BF16, BF16]
            px = _nm_linear(x, g1, mod_x[:, 0:2], ws, dts)
            pc = [per_batch(t) for t in _nm_linear(ctx, g1, mod_c[:, 0:2], ws, dts)]
            y_ssd_x, y_ssd_c = _ssd_mixer(
                px[0], pc[0], px[1], pc[1], px[3], pc[3], ev_conv_w[j], ev_conv_b[j][None],
                _group_lanes(ev_a_log[j], LANES), _group_lanes(ev_dt_bias[j], LANES),
                jnp.repeat(ev_d[j], SSD_HEAD_DIM).reshape(SSD_GROUPS, 1, SSD_GW),
                ev_ssd_norm_g[j].reshape(SSD_GROUPS, 1, SSD_GW))
            y_lru_x, y_lru_c = _lru_mixer(
                px[2], pc[2], px[4], pc[4], ev_lru_conv_w[j], ev_lru_conv_b[j][None],
                _block_diag(ev_lru_wa[j], LRU_TILE).astype(BF16), _block_diag(ev_lru_wx[j], LRU_TILE).astype(BF16),
                ev_lru_ba[j], ev_lru_bx[j], ev_lru_lam[j])
            w_out = ev_w_out[j].astype(BF16)
            w_outs = [w_out[:SSD_INNER], w_out[SSD_INNER:]]
            acts_x = [y_ssd_x, y_lru_x]
            acts_c = [y_ssd_c.reshape(1, bsz * sc, -1), y_lru_c.reshape(1, bsz * sc, -1)]
        else:
            w_qkv = od_w_qkv[j].astype(BF16)
            nd = w_qkv.shape[1] // 3
            ws = [w_qkv[:, :nd], w_qkv[:, nd:2 * nd], w_qkv[:, 2 * nd:]]
            gains = [jnp.tile(od_q_norm_g[j], NA_HEADS)[None], jnp.tile(od_k_norm_g[j], NA_HEADS)[None], None]
            q_x, k_x, v_x = _nm_linear(x, g1, mod_x[:, 0:2], ws, [BF16] * 3, gains)
            q_c, k_c, v_c = [per_batch(t) for t in _nm_linear(ctx, g1, mod_c[:, 0:2], ws, [BF16] * 3, gains)]
            table = _rpb_table(od_rpb[j])
            o_x, o_c = _na_attention(q_x, k_x, v_x, q_c, k_c, v_c, table, want_ctx)
            w_outs = [od_w_o[j].astype(BF16)]
            acts_x = [o_x]
            acts_c = [o_c.reshape(1, bsz * sc, -1)] if want_ctx else None

        w_q = pe_w_q[layer].reshape(d, PEER_HEADS, 2 * PEER_KEYS).transpose(1, 0, 2).astype(BF16)
        keys = pe_keys[layer].reshape(2 * PEER_HEADS, PEER_KEYS, -1).astype(BF16)
        u = _pack_weight(pe_u[layer], transpose=False)
        v_t = _pack_weight(pe_v[layer], transpose=True)

        x, hx = _out_linear(acts_x, w_outs, x, mod_x[:, 2:5], g2)
        x = _peer(hx, x, mod_x[:, 5:6], w_q, keys, u, v_t)
        if want_ctx:
            ctx, hc = _out_linear(acts_c, w_outs, ctx, mod_c[:, 2:5], g2)
            ctx = _peer(hc, ctx, mod_c[:, 5:6], w_q, keys, u, v_t)
    return x
```

```python
import functools
import math

import jax
import jax.numpy as jnp
from jax import lax
from jax.experimental import pallas as pl
from jax.experimental.pallas import tpu as pltpu

F32 = jnp.float32
BF16 = jnp.bfloat16
HIGHEST = lax.Precision.HIGHEST

EPS = 1e-6
NEG_INF = -1e30

GRID_W = 64
SSD_HEADS = 16
SSD_HEAD_DIM = 64
SSD_GROUPS = 4
SSD_HPG = SSD_HEADS // SSD_GROUPS
SSD_STATE = 128
SSD_CHUNK = 128
SSD_INNER = SSD_HEADS * SSD_HEAD_DIM
SSD_GW = SSD_INNER // SSD_GROUPS
SSD_XBC = SSD_INNER + 2 * SSD_GROUPS * SSD_STATE
LRU_WIDTH = 1024
LRU_BLOCKS = 16
LRU_BLOCK_DIM = LRU_WIDTH // LRU_BLOCKS
LRU_C = 8.0
LRU_TILE = 256
NA_HEADS = 16
NA_HEAD_DIM = 64
NA_KH = 8
NA_KW = 16
NA_QROWS = 4
NA_KROWS = 12
PEER_HEADS = 8
PEER_KEYS = 128
PEER_TOPK = 16
PEER_ICHUNK = 8

LANES = 128
SUBLANES = 8
VMEM_LIMIT_BYTES = 56 * 1024 * 1024


def _cparams(*sem):
    return pltpu.CompilerParams(dimension_semantics=sem, vmem_limit_bytes=VMEM_LIMIT_BYTES)


def _silu(x):
    return x * (1.0 / (1.0 + jnp.exp(-x)))


def _sigmoid(x):
    return 1.0 / (1.0 + jnp.exp(-x))


def _softplus(x):
    return jnp.maximum(x, 0.0) + jnp.log(1.0 + jnp.exp(-jnp.abs(x)))


def _gelu_tanh(x):
    k0 = -2.0 * math.sqrt(2.0 / math.pi)
    return x / (1.0 + jnp.exp(x * (k0 + (k0 * 0.044715) * (x * x))))


def _rms_mod(x, g, shift, scale):
    ms = jnp.mean(x * x, axis=-1, keepdims=True)
    y = x * lax.rsqrt(ms + EPS) * g
    return y * (1.0 + scale) + shift


def _dot(a, b):
    return jnp.dot(a, b, preferred_element_type=F32)


def _dot_nt(a, b):
    return lax.dot_general(a, b, (((1,), (1,)), ((), ())), preferred_element_type=F32)


def _ada_kernel(c_ref, w_ref, b_ref, o_ref):
    s = _silu(c_ref[...])
    o_ref[0] = jnp.dot(s, w_ref[0], preferred_element_type=F32, precision=HIGHEST) + b_ref[0]


def _ada_mods(c_all, ada_w, ada_b):
    depth, d, n = ada_w.shape
    rows = c_all.shape[0]
    tn = 1536
    return pl.pallas_call(
        _ada_kernel,
        grid=(depth, n // tn),
        in_specs=[pl.BlockSpec((rows, d), lambda l, j: (0, 0)),
                  pl.BlockSpec((1, d, tn), lambda l, j: (l, 0, j)),
                  pl.BlockSpec((1, 1, tn), lambda l, j: (l, 0, j))],
        out_specs=pl.BlockSpec((1, rows, tn), lambda l, j: (l, 0, j)),
        out_shape=jax.ShapeDtypeStruct((depth, rows, n), F32),
        compiler_params=_cparams("arbitrary", "arbitrary"),
        name="ada_mods",
    )(c_all, ada_w, ada_b.reshape(depth, 1, n))


def _head_block_ones(n):
    r = lax.broadcasted_iota(jnp.int32, (n, n), 0) // NA_HEAD_DIM
    c = lax.broadcasted_iota(jnp.int32, (n, n), 1) // NA_HEAD_DIM
    return (r == c).astype(F32)


def _nm_linear_kernel(*refs, n_out, head_norm, tn):
    x_ref, g_ref, mod_ref = refs[:3]
    w_refs = refs[3:3 + n_out]
    hg_refs = refs[3 + n_out:3 + n_out + sum(head_norm)]
    o_refs = refs[3 + n_out + sum(head_norm):]
    h = _rms_mod(x_ref[0], g_ref[...], mod_ref[0, 0:1, :], mod_ref[0, 1:2, :]).astype(BF16)
    hg_i = 0
    for w_ref, o_ref, hn in zip(w_refs, o_refs, head_norm):
        n = w_ref.shape[1]
        for j in range(n // tn):
            y = _dot(h, w_ref[:, j * tn:(j + 1) * tn])
            if hn:
                ss = jnp.dot(y * y, _head_block_ones(tn), preferred_element_type=F32, precision=HIGHEST)
                y = y * lax.rsqrt(ss * (1.0 / NA_HEAD_DIM) + EPS) * hg_refs[hg_i][:, j * tn:(j + 1) * tn]
            o_ref[0, :, j * tn:(j + 1) * tn] = y.astype(o_ref.dtype)
        hg_i += hn


def _nm_linear(x, g, mod, ws, out_dtypes, head_gains=None, tm=512, tn=256):
    bn, sn, d = x.shape
    tm = min(tm, sn)
    n_out = len(ws)
    head_gains = head_gains or [None] * n_out
    head_norm = tuple(hg is not None for hg in head_gains)
    hgs = [hg for hg in head_gains if hg is not None]
    in_specs = [pl.BlockSpec((1, tm, d), lambda b, i: (b, i, 0)),
                pl.BlockSpec((1, d), lambda b, i: (0, 0)),
                pl.BlockSpec((1, 2, d), lambda b, i: (b, 0, 0))]
    in_specs += [pl.BlockSpec(w.shape, lambda b, i: (0, 0)) for w in ws]
    in_specs += [pl.BlockSpec(hg.shape, lambda b, i: (0, 0)) for hg in hgs]
    out_specs = [pl.BlockSpec((1, tm, w.shape[1]), lambda b, i: (b, i, 0)) for w in ws]
    out_shape = [jax.ShapeDtypeStruct((bn, sn, w.shape[1]), dt) for w, dt in zip(ws, out_dtypes)]
    return pl.pallas_call(
        functools.partial(_nm_linear_kernel, n_out=n_out, head_norm=head_norm, tn=tn),
        grid=(bn, sn // tm),
        in_specs=in_specs, out_specs=out_specs, out_shape=out_shape,
        compiler_params=_cparams("arbitrary", "arbitrary"),
        name="nm_linear",
    )(x, g, mod, *ws, *hgs)


def _out_linear_kernel(*refs, n_in):
    a_refs = refs[:n_in]
    w_refs = refs[n_in:2 * n_in]
    x_ref, mod_ref, g_ref, xo_ref, ho_ref = refs[2 * n_in:]
    y = _dot(a_refs[0][0], w_refs[0][...])
    for a_ref, w_ref in zip(a_refs[1:], w_refs[1:]):
        y = y + _dot(a_ref[0], w_ref[...])
    xn = x_ref[0] + mod_ref[0, 0:1, :] * y
    xo_ref[0] = xn
    ho_ref[0] = _rms_mod(xn, g_ref[...], mod_ref[0, 1:2, :], mod_ref[0, 2:3, :]).astype(BF16)


def _out_linear(acts, ws, x, mod, g, tm=512):
    bn, sn, d = x.shape
    tm = min(tm, sn)
    n_in = len(acts)
    in_specs = [pl.BlockSpec((1, tm, a.shape[2]), lambda b, i: (b, i, 0)) for a in acts]
    in_specs += [pl.BlockSpec(w.shape, lambda b, i: (0, 0)) for w in ws]
    in_specs += [pl.BlockSpec((1, tm, d), lambda b, i: (b, i, 0)),
                 pl.BlockSpec((1, 3, d), lambda b, i: (b, 0, 0)),
                 pl.BlockSpec((1, d), lambda b, i: (0, 0))]
    return pl.pallas_call(
        functools.partial(_out_linear_kernel, n_in=n_in),
        grid=(bn, sn // tm),
        in_specs=in_specs,
        out_specs=[pl.BlockSpec((1, tm, d), lambda b, i: (b, i, 0))] * 2,
        out_shape=[jax.ShapeDtypeStruct((bn, sn, d), F32), jax.ShapeDtypeStruct((bn, sn, d), BF16)],
        compiler_params=_cparams("arbitrary", "arbitrary"),
        name="out_linear",
    )(*acts, *ws, x, mod, g)


CONV_HALO = 16


def _conv_chunk(src_ref, s, seg_len, w, bias, rows=SSD_CHUNK):
    ncol = src_ref.shape[2]
    if s > 0:
        prev = src_ref[0, s - CONV_HALO:s, :].astype(F32)
    else:
        prev = jnp.zeros((CONV_HALO, ncol), F32)
    cur = src_ref[0, s:s + rows, :].astype(F32)
    if s + rows < seg_len:
        nxt = src_ref[0, s + rows:s + rows + CONV_HALO, :].astype(F32)
    else:
        nxt = jnp.zeros((CONV_HALO, ncol), F32)
    win = jnp.concatenate([prev, cur, nxt], axis=0)
    taps = w.shape[0]
    acc = bias
    for k in range(taps):
        off = CONV_HALO - taps // 2 + k
        acc = acc + win[off:off + rows, :] * w[k:k + 1, :]
    return acc


def _lane_head_expand(cols, width):
    nh = len(cols)
    hd = width // nh
    rows = cols[0].shape[0]
    lane_head = lax.broadcasted_iota(jnp.int32, (rows, width), 1) // hd
    out = jnp.broadcast_to(cols[nh - 1], (rows, width))
    for h in range(nh - 2, -1, -1):
        out = jnp.where(lane_head == h, jnp.broadcast_to(cols[h], (rows, width)), out)
    return out


def _ssd_kernel(xs_x, bm_x, cm_x, xs_c, bm_c, cm_c, dt_x, dt_c, z_x, z_c,
                cw_xs, cw_b, cw_c, cb_xs, cb_b, cb_c, alog_ref, dtb_ref, dsk_ref, ng_ref,
                y_x, y_c,
                xs_s, bm_s, cm_s, dt_s, y_s, st_s, cs_s, cst_s, *, sx, sc):
    q = SSD_CHUNK
    nc_c, nc_x = sc // q, sx // q
    nc = nc_c + nc_x
    gw = xs_s.shape[1]

    for seg_ref3, seg_len, base in (((xs_c, bm_c, cm_c), sc, 0), ((xs_x, bm_x, cm_x), sx, sc)):
        for ci in range(seg_len // q):
            s = ci * q
            for src, dst, w_ref, b_ref in zip(seg_ref3, (xs_s, bm_s, cm_s), (cw_xs, cw_b, cw_c),
                                              (cb_xs, cb_b, cb_c)):
                dst[base + s:base + s + q, :] = _silu(_conv_chunk(src, s, seg_len, w_ref[...], b_ref[...]))
    dt_s[0:sc, :] = _softplus(dt_c[0] + dtb_ref[0])
    dt_s[sc:sc + sx, :] = _softplus(dt_x[0] + dtb_ref[0])

    a_neg = -jnp.exp(alog_ref[0])
    row = lax.broadcasted_iota(jnp.int32, (q, q), 0)
    col = lax.broadcasted_iota(jnp.int32, (q, q), 1)
    tri = ((col <= row).astype(F32), (col >= row).astype(F32))
    keep = (col <= row, col >= row)
    lane_head = lax.broadcasted_iota(jnp.int32, (q, gw), 1) // SSD_HEAD_DIM

    st_s[...] = jnp.zeros_like(st_s)
    y_s[...] = jnp.zeros_like(y_s)

    for ci in range(nc):
        la = dt_s[ci * q:(ci + 1) * q, :] * a_neg
        for d in range(2):
            cs = jnp.dot(tri[d], la, preferred_element_type=F32, precision=HIGHEST)
            cs_s[d, ci * q:(ci + 1) * q, :] = cs
            cst_s[d, ci * q:(ci + 1) * q, :] = cs.T

    def chunk_body(i, carry):
        for d in range(2):
            if d == 0:
                ci = i
            else:
                ci = jnp.where(i < nc_c, nc_c - 1 - i, nc + nc_c - 1 - i)
            r0 = pl.multiple_of(ci * q, q)
            xs = xs_s[pl.ds(r0, q), :]
            bm = bm_s[pl.ds(r0, q), :]
            cm = cm_s[pl.ds(r0, q), :]
            dt = dt_s[pl.ds(r0, q), :]
            cs = cs_s[d, pl.ds(r0, q), :]
            cs_t = cst_s[d, pl.ds(r0, q), :]
            cb = _dot_nt(cm.astype(BF16), bm.astype(BF16))
            heads = [d * SSD_HPG + h for h in range(SSD_HPG)]
            dt_mat = _lane_head_expand([dt[:, c:c + 1] for c in heads], gw)
            cs_mat = _lane_head_expand([cs[:, c:c + 1] for c in heads], gw)
            xd = xs * dt_mat
            xd_b = xd.astype(BF16)
            y = jnp.zeros((q, gw), F32)
            for h, c in enumerate(heads):
                diff = cs[:, c:c + 1] - cs_t[c:c + 1, :]
                lmat = jnp.exp(jnp.where(keep[d], diff, NEG_INF))
                y = jnp.where(lane_head == h, _dot((cb * lmat).astype(BF16), xd_b), y)
            st = st_s[d]
            y = y + _dot(cm.astype(BF16), st.astype(BF16)) * jnp.exp(cs_mat)
            end = q - 1 if d == 0 else 0
            cs_end = cs_mat[end:end + 1, :]
            s_new = _dot(bm.T.astype(BF16), (xd * jnp.exp(cs_end - cs_mat)).astype(BF16))
            st_s[d] = st * jnp.exp(cs_end) + s_new
            y_s[pl.ds(r0, q), :] = y_s[pl.ds(r0, q), :] + y
        return carry

    lax.fori_loop(0, nc, chunk_body, 0)

    for ci in range(nc):
        s = ci * q
        if ci < nc_c:
            z = z_c[0, s:s + q, :]
        else:
            z = z_x[0, s - sc:s - sc + q, :]
        y = (y_s[s:s + q, :] + dsk_ref[0] * xs_s[s:s + q, :]) * _silu(z.astype(F32))
        ms = jnp.mean(y * y, axis=-1, keepdims=True)
        out = (y * lax.rsqrt(ms + EPS) * ng_ref[0]).astype(BF16)
        if ci < nc_c:
            y_c[0, s:s + q, :] = out
        else:
            y_x[0, s - sc:s - sc + q, :] = out


def _ssd_mixer(xbc_x, xbc_c, dt_x, dt_c, z_x, z_c, conv_w, conv_b, alog_g, dtb_g, dsk_g, ng_g):
    bsz, sx, _ = xbc_x.shape
    sc = xbc_c.shape[1]
    g, gw, n = SSD_GROUPS, SSD_GW, SSD_STATE
    nb = SSD_INNER // n
    taps = conv_w.shape[0]

    def seq(s, w, off):
        return pl.BlockSpec((1, s, w), lambda b, j, off=off: (b, 0, off + j))

    def par(r, w, off):
        return pl.BlockSpec((r, w), lambda b, j, off=off: (0, off + j))

    def grp(w):
        return pl.BlockSpec((1, 1, w), lambda b, j: (j, 0, 0))

    in_specs = [seq(sx, gw, 0), seq(sx, n, nb), seq(sx, n, nb + g),
                seq(sc, gw, 0), seq(sc, n, nb), seq(sc, n, nb + g),
                seq(sx, LANES, 0), seq(sc, LANES, 0), seq(sx, gw, 0), seq(sc, gw, 0),
                par(taps, gw, 0), par(taps, n, nb), par(taps, n, nb + g),
                par(1, gw, 0), par(1, n, nb), par(1, n, nb + g),
                grp(LANES), grp(LANES), grp(gw), grp(gw)]
    stot = sx + sc
    return pl.pallas_call(
        functools.partial(_ssd_kernel, sx=sx, sc=sc),
        grid=(bsz, g),
        in_specs=in_specs,
        out_specs=[seq(sx, gw, 0), seq(sc, gw, 0)],
        out_shape=[jax.ShapeDtypeStruct((bsz, sx, SSD_INNER), BF16),
                   jax.ShapeDtypeStruct((bsz, sc, SSD_INNER), BF16)],
        scratch_shapes=[pltpu.VMEM((stot, gw), F32), pltpu.VMEM((stot, n), F32), pltpu.VMEM((stot, n), F32),
                        pltpu.VMEM((stot, LANES), F32), pltpu.VMEM((stot, gw), F32), pltpu.VMEM((2, n, gw), F32),
                        pltpu.VMEM((2, stot, LANES), F32), pltpu.VMEM((2, stot, LANES), F32)],
        compiler_params=_cparams("arbitrary", "arbitrary"),
        name="ssd_mixer",
    )(xbc_x, xbc_x, xbc_x, xbc_c, xbc_c, xbc_c, dt_x, dt_c, z_x, z_c,
      conv_w, conv_w, conv_w, conv_b, conv_b, conv_b, alog_g, dtb_g, dsk_g, ng_g)


def _lru_kernel(xl_x, xl_c, gt_x, gt_c, cw, cb, wa, wx, ba, bx, lam, y_x, y_c,
                xr_s, a_s, b_s, y_s, *, sx, sc):
    q = SSD_CHUNK
    stot = sx + sc
    w = xr_s.shape[1]
    for src, seg_len, base in ((xl_c, sc, 0), (xl_x, sx, sc)):
        for ci in range(seg_len // q):
            s = ci * q
            xr_s[base + s:base + s + q, :] = _conv_chunk(src, s, seg_len, cw[...], cb[...])

    ng = stot // SUBLANES
    ng_c = sc // SUBLANES
    sub = lax.broadcasted_iota(jnp.int32, (SUBLANES, w), 0)
    rt = q
    for d in range(2):
        nsp = _softplus(-lam[d:d + 1, :])
        for ci in range(stot // rt):
            s = ci * rt
            xr = xr_s[s:s + rt, :]
            xb = xr.astype(BF16)
            r = _sigmoid(_dot(xb, wa[d, 0]) + ba[d:d + 1, :])
            ig = _sigmoid(_dot(xb, wx[d, 0]) + bx[d:d + 1, :])
            a = jnp.exp(-LRU_C * r * nsp)
            a_s[d, s:s + rt, :] = a
            b_s[d, s:s + rt, :] = jnp.sqrt(1.0 - a * a) * (ig * xr)

    y_s[...] = jnp.zeros_like(y_s)

    def group_body(k, carries):
        new = []
        for d in range(2):
            if d == 0:
                gi = k
            else:
                gi = jnp.where(k < ng_c, ng_c - 1 - k, ng + ng_c - 1 - k)
            r0 = pl.multiple_of(gi * SUBLANES, SUBLANES)
            a = a_s[d, pl.ds(r0, SUBLANES), :]
            b = b_s[d, pl.ds(r0, SUBLANES), :]
            for sh in (1, 2, 4):
                if d == 0:
                    valid = sub >= sh
                    a_sh = pltpu.roll(a, sh, axis=0)
                    b_sh = pltpu.roll(b, sh, axis=0)
                else:
                    valid = sub < SUBLANES - sh
                    a_sh = pltpu.roll(a, SUBLANES - sh, axis=0)
                    b_sh = pltpu.roll(b, SUBLANES - sh, axis=0)
                b = jnp.where(valid, a * b_sh + b, b)
                a = jnp.where(valid, a * a_sh, a)
            h = a * carries[d] + b
            y_s[pl.ds(r0, SUBLANES), :] = y_s[pl.ds(r0, SUBLANES), :] + h
            last = h[SUBLANES - 1:SUBLANES, :] if d == 0 else h[0:1, :]
            new.append(jnp.broadcast_to(last, (SUBLANES, w)))
        return tuple(new)

    zero = jnp.zeros((SUBLANES, w), F32)
    lax.fori_loop(0, ng, group_body, (zero, zero))

    for ci in range(stot // rt):
        s = ci * rt
        if s < sc:
            gate = gt_c[0, s:s + rt, :]
        else:
            gate = gt_x[0, s - sc:s - sc + rt, :]
        out = (y_s[s:s + rt, :] * _gelu_tanh(gate.astype(F32))).astype(BF16)
        if s < sc:
            y_c[0, s:s + rt, :] = out
        else:
            y_x[0, s - sc:s - sc + rt, :] = out


def _lru_mixer(xl_x, xl_c, gt_x, gt_c, conv_w, conv_b, wa_bd, wx_bd, ba, bx, lam):
    bsz, sx, width = xl_x.shape
    sc = xl_c.shape[1]
    w = LRU_TILE
    taps = conv_w.shape[0]

    def seq(s):
        return pl.BlockSpec((1, s, w), lambda b, j: (b, 0, j))

    def par(r):
        return pl.BlockSpec((r, w), lambda b, j: (0, j))

    wspec = pl.BlockSpec((2, 1, w, w), lambda b, j: (0, j, 0, 0))
    stot = sx + sc
    return pl.pallas_call(
        functools.partial(_lru_kernel, sx=sx, sc=sc),
        grid=(bsz, width // w),
        in_specs=[seq(sx), seq(sc), seq(sx), seq(sc), par(taps), par(1), wspec, wspec, par(2), par(2), par(2)],
        out_specs=[seq(sx), seq(sc)],
        out_shape=[jax.ShapeDtypeStruct((bsz, sx, width), BF16), jax.ShapeDtypeStruct((bsz, sc, width), BF16)],
        scratch_shapes=[pltpu.VMEM((stot, w), F32), pltpu.VMEM((2, stot, w), F32), pltpu.VMEM((2, stot, w), F32),
                        pltpu.VMEM((stot, w), F32)],
        compiler_params=_cparams("arbitrary", "arbitrary"),
        name="lru_mixer",
    )(xl_x, xl_c, gt_x, gt_c, conv_w, conv_b, wa_bd, wx_bd, ba, bx, lam)


def _rpb_table_kernel(rpb_ref, o_ref, *, n_dr, n_dc):
    h = pl.program_id(0)
    w = GRID_W
    qcol = lax.broadcasted_iota(jnp.int32, (w, 2 * w), 0)
    lane = lax.broadcasted_iota(jnp.int32, (w, 2 * w), 1)
    kcol = lane % w
    hi = lane >= w
    rel = kcol - qcol + (NA_KW - 1)
    cstart = jnp.clip(qcol - NA_KW // 2, 0, w - NA_KW)
    in_win = (kcol >= cstart) & (kcol < cstart + NA_KW)
    for d in range(n_dr + 1):
        acc = jnp.full((w, 2 * w), NEG_INF, F32)
        for dc in range(n_dc):
            lo = rpb_ref[(h * n_dr + d - 1) * n_dc + dc] if d >= 1 else NEG_INF
            up = rpb_ref[(h * n_dr + d) * n_dc + dc] if d < n_dr else NEG_INF
            acc = jnp.where(rel == dc, jnp.where(hi, up, lo), acc)
        valid = in_win
        if d == 0:
            valid = valid & hi
        if d == n_dr:
            valid = valid & jnp.logical_not(hi)
        o_ref[0, d] = jnp.where(valid, acc, NEG_INF)


def _rpb_table(rpb):
    nh, n_dr, n_dc = rpb.shape
    return pl.pallas_call(
        functools.partial(_rpb_table_kernel, n_dr=n_dr, n_dc=n_dc),
        grid=(nh,),
        in_specs=[pl.BlockSpec(memory_space=pltpu.SMEM)],
        out_specs=pl.BlockSpec((1, n_dr + 1, GRID_W, 2 * GRID_W), lambda h: (h, 0, 0, 0)),
        out_shape=jax.ShapeDtypeStruct((nh, n_dr + 1, GRID_W, 2 * GRID_W), F32),
        compiler_params=_cparams("arbitrary"),
        name="rpb_table",
    )(rpb.reshape(-1))


def _na_kernel(*refs, sx, sc, want_ctx):
    if want_ctx:
        q_x, k_x, v_x, q_c, k_c, v_c, tab, o_x, o_c = refs
    else:
        q_x, k_x, v_x, k_c, v_c, tab, o_x = refs
    w = GRID_W
    rows = sx // w
    qb = NA_QROWS * w
    kb = NA_KROWS * w
    n_blk = rows // NA_QROWS
    scale = NA_HEAD_DIM ** -0.5
    lane = lax.broadcasted_iota(jnp.int32, (1, 2 * NA_HEAD_DIM), 1)
    in_head = (lane < NA_HEAD_DIM, lane >= NA_HEAD_DIM)
    keyrow = lax.broadcasted_iota(jnp.int32, (1, kb), 1) // w
    kc = k_c[0]
    vc = v_c[0]
    n_tab = tab.shape[1]

    def softmax_pv(parts):
        m = parts[0][0].max(axis=-1, keepdims=True)
        for s, _ in parts[1:]:
            m = jnp.maximum(m, s.max(axis=-1, keepdims=True))
        acc, den = None, None
        for s, v in parts:
            e = jnp.exp(s - m)
            den = e.sum(axis=-1, keepdims=True) if den is None else den + e.sum(axis=-1, keepdims=True)
            pv = _dot(e.astype(BF16), v)
            acc = pv if acc is None else acc + pv
        return acc / den

    def block_body(rb, carry):
        ws = jnp.clip(NA_QROWS * rb - NA_KH // 2, 0, rows - NA_KROWS)
        q0 = pl.multiple_of(rb * qb, qb)
        k0 = pl.multiple_of(ws * w, w)
        qblk = q_x[0, pl.ds(q0, qb), :]
        kwin = k_x[0, pl.ds(k0, kb), :]
        vwin = v_x[0, pl.ds(k0, kb), :]
        out = jnp.zeros((qb, 2 * NA_HEAD_DIM), F32)
        for hh in range(2):
            qm = jnp.where(in_head[hh], qblk, jnp.zeros_like(qblk))
            s_loc = _dot_nt(qm, kwin) * scale
            s_ctx = _dot_nt(qm, kc) * scale
            pieces = []
            for rq in range(NA_QROWS):
                r = NA_QROWS * rb + rq
                rs = jnp.clip(r - NA_KH // 2, 0, rows - NA_KH)
                lo = rs - ws
                valid = (keyrow >= lo) & (keyrow < lo + NA_KH)
                blocks = []
                for ip in range(NA_KROWS // 2):
                    dr_lo = ws + 2 * ip - r + NA_KH - 1
                    blocks.append(tab[hh, jnp.clip(dr_lo + 1, 0, n_tab - 1)])
                bias = jnp.concatenate(blocks, axis=1)
                piece = s_loc[rq * w:(rq + 1) * w, :] + bias
                pieces.append(jnp.where(valid, piece, NEG_INF))
            s_loc = jnp.concatenate(pieces, axis=0)
            o = softmax_pv([(s_loc, vwin), (s_ctx, vc)])
            out = jnp.where(in_head[hh], o, out)
        o_x[0, pl.ds(q0, qb), :] = out.astype(o_x.dtype)
        return carry

    lax.fori_loop(0, n_blk, block_body, 0)

    if want_ctx:
        qc = q_c[0]
        out = jnp.zeros((sc, 2 * NA_HEAD_DIM), F32)
        for hh in range(2):
            qm = jnp.where(in_head[hh], qc, jnp.zeros_like(qc))
            o = softmax_pv([(_dot_nt(qm, kc) * scale, vc)])
            out = jnp.where(in_head[hh], o, out)
        o_c[0] = out.astype(o_c.dtype)


def _na_attention(q_x, k_x, v_x, q_c, k_c, v_c, table, want_ctx):
    bsz, sx, dim = q_x.shape
    sc = k_c.shape[1]
    pw = 2 * NA_HEAD_DIM
    n_pair = dim // pw

    def seq(s):
        return pl.BlockSpec((1, s, pw), lambda p, b: (b, 0, p))

    tspec = pl.BlockSpec((2,) + table.shape[1:], lambda p, b: (p, 0, 0, 0))
    if want_ctx:
        args = (q_x, k_x, v_x, q_c, k_c, v_c, table)
        in_specs = [seq(sx)] * 3 + [seq(sc)] * 3 + [tspec]
        out_specs = [seq(sx), seq(sc)]
        out_shape = [jax.ShapeDtypeStruct((bsz, sx, dim), BF16), jax.ShapeDtypeStruct((bsz, sc, dim), BF16)]
    else:
        args = (q_x, k_x, v_x, k_c, v_c, table)
        in_specs = [seq(sx)] * 3 + [seq(sc)] * 2 + [tspec]
        out_specs = [seq(sx)]
        out_shape = [jax.ShapeDtypeStruct((bsz, sx, dim), BF16)]
    res = pl.pallas_call(
        functools.partial(_na_kernel, sx=sx, sc=sc, want_ctx=want_ctx),
        grid=(n_pair, bsz),
        in_specs=in_specs, out_specs=out_specs, out_shape=out_shape,
        compiler_params=_cparams("arbitrary", "arbitrary"),
        name="na_attention",
    )(*args)
    return res if want_ctx else (res[0], None)


def _top_rows(s, k, exact, want_rank=True):
    n, tt = s.shape
    top_id = lax.broadcasted_iota(jnp.int32, (k, tt), 0)
    if exact:
        rowid = lax.broadcasted_iota(jnp.int32, (n, tt), 0).astype(F32)
    work = s
    top = jnp.zeros((k, tt), F32)
    rank = jnp.full((n, tt), float(k), F32) if want_rank else None
    for it in range(k):
        m = jnp.max(work, axis=0, keepdims=True)
        sel = work == m
        if exact:
            sel = rowid == jnp.min(jnp.where(sel, rowid, float(n)), axis=0, keepdims=True)
        top = jnp.where(top_id == it, m, top)
        if want_rank:
            rank = jnp.where(sel, float(it), rank)
        work = jnp.where(sel, -jnp.inf, work)
    picked = work == -jnp.inf
    n_sel = jnp.sum(jnp.where(picked, 1.0, 0.0), axis=0, keepdims=True)
    return top, rank, picked, n_sel


def _peer_route_kernel(h_ref, wq_ref, keys_ref, cnt_ref, rk_ref, e0_ref, e1_ref, s_s, top_s, rank_s, cnti_s, z_s):
    nk, k, ic = PEER_KEYS, PEER_TOPK, PEER_ICHUNK
    tt = h_ref.shape[1]
    sw = top_s.shape[2]
    n_strip = tt // sw
    n_unit = PEER_HEADS * n_strip

    def scores(u):
        h, si = u // n_strip, u % n_strip
        rows = pl.ds(pl.multiple_of(si * sw, sw), sw)
        q = _dot(h_ref[0, rows, :], wq_ref[h]).astype(BF16)
        for z in range(2):
            s_s[u % 2, z] = _dot_nt(keys_ref[2 * h + z], q[:, z * nk:(z + 1) * nk])

    def pair_stage(t0, t1, exact):
        cand = jnp.concatenate([t0[0:1] + t1] + [t0[a:a + 1] + t1[0:8] for a in range(1, 8)]
                               + [t0[8:16] + t1[0:1]], axis=0)
        _, _, picked, n_sel = _top_rows(cand, k, exact, want_rank=False)
        pf = jnp.where(picked, 1.0, 0.0)
        z_sum = jnp.sum(pf * jnp.exp(cand - cand[0:1]), axis=0, keepdims=True)
        cnts = [jnp.sum(pf[0:k], axis=0, keepdims=True)]
        cnts += [jnp.sum(pf[k + 8 * (a - 1):k + 8 * a], axis=0, keepdims=True) for a in range(1, 8)]
        cnts += [pf[k + 56 + a:k + 57 + a] for a in range(8)]
        return cnts, z_sum, n_sel

    def route(u, prefetch):
        h, si = u // n_strip, u % n_strip
        cols = pl.ds(pl.multiple_of(si * sw, sw), sw)
        s0, s1 = s_s[u % 2, 0], s_s[u % 2, 1]
        if prefetch:
            scores(u + 1)

        t0, _, _, n0 = _top_rows(s0, k, False, want_rank=False)
        t1, rank1, _, n1 = _top_rows(s1, k, False)
        cnts, z_sum, n2 = pair_stage(t0, t1, False)
        cnt_i = jnp.zeros_like(s0)
        for a in range(k):
            cnt_i = jnp.where(s0 == t0[a:a + 1], cnts[a], cnt_i)
        top_s[0], top_s[1] = t0, t1
        rank_s[...] = rank1
        cnti_s[...] = cnt_i
        z_s[...] = z_sum
        ties = jnp.max(jnp.abs(n0 - float(k)) + jnp.abs(n1 - float(k)) + jnp.abs(n2 - float(k)))

        @pl.when(ties > 0.5)
        def _():
            t0, rank0, _, _ = _top_rows(s0, k, True)
            t1, rank1, _, _ = _top_rows(s1, k, True)
            cnts, z_sum, _ = pair_stage(t0, t1, True)
            cnt_i = jnp.zeros_like(s0)
            for a in range(k):
                cnt_i = jnp.where(rank0 == float(a), cnts[a], cnt_i)
            top_s[0], top_s[1] = t0, t1
            rank_s[...] = rank1
            cnti_s[...] = cnt_i
            z_s[...] = z_sum

        cnt_i = cnti_s[...]
        e0 = jnp.exp(s0 - top_s[0, 0:1]) / z_s[...]
        hi = pl.ds(pl.multiple_of(h * ic, ic), ic)
        hj = pl.ds(pl.multiple_of(h * nk, nk), nk)
        for c in range(nk // ic):
            cnt_ref[0, c, hi, cols] = cnt_i[c * ic:(c + 1) * ic]
            e0_ref[0, c, hi, cols] = e0[c * ic:(c + 1) * ic]
        rk_ref[0, hj, cols] = rank_s[...].astype(rk_ref.dtype)
        e1_ref[0, hj, cols] = jnp.exp(s1 - top_s[1, 0:1]).astype(e1_ref.dtype)

    scores(jnp.int32(0))

    def unit(u, carry):
        route(u, True)
        return carry

    lax.fori_loop(0, n_unit - 1, unit, 0)
    route(jnp.int32(n_unit - 1), False)


ROUTE_STRIP = 256


def _peer_route(hx, w_q, keys, tt):
    bn, sn, d = hx.shape
    nk, nh, ic = PEER_KEYS, PEER_HEADS, PEER_ICHUNK
    spec_i = pl.BlockSpec((1, nk // ic, nh * ic, tt), lambda b, i: (b, 0, 0, i))
    spec_j = pl.BlockSpec((1, nh * nk, tt), lambda b, i: (b, 0, i))
    shape_i = jax.ShapeDtypeStruct((bn, nk // ic, nh * ic, sn), F32)
    shape_j = jax.ShapeDtypeStruct((bn, nh * nk, sn), BF16)
    sw = min(ROUTE_STRIP, tt)
    return pl.pallas_call(
        _peer_route_kernel,
        grid=(bn, sn // tt),
        in_specs=[pl.BlockSpec((1, tt, d), lambda b, i: (b, i, 0)),
                  pl.BlockSpec(w_q.shape, lambda b, i: (0, 0, 0)),
                  pl.BlockSpec(keys.shape, lambda b, i: (0, 0, 0))],
        out_specs=[spec_i, spec_j, spec_i, spec_j],
        out_shape=[shape_i, shape_j, shape_i, shape_j],
        scratch_shapes=[pltpu.VMEM((2, 2, nk, sw), F32), pltpu.VMEM((2, PEER_TOPK, sw), F32),
                        pltpu.VMEM((nk, sw), F32), pltpu.VMEM((nk, sw), F32), pltpu.VMEM((1, sw), F32)],
        compiler_params=_cparams("arbitrary", "arbitrary"),
        name="peer_route",
    )(hx, w_q, keys)


def _peer_dense_kernel(h_ref, cnt_ref, rk_ref, e0_ref, e1_ref, u_ref, vt_ref, x_ref, g_ref, o_ref,
                       acc_ref, act_ref, p_ref, hx_s, rk_s, e1_s, cnt_s, e0_s):
    nk, ic = PEER_KEYS, PEER_ICHUNK
    ck = pl.program_id(2)
    n_slab = act_ref.shape[0]
    sw = min(PEER_STRIP, n_slab * LANES)
    per = sw // LANES
    n_strip = n_slab // per
    il_group, j_group = 4, 2
    pk = _rows_per_word(BF16)

    @pl.when(ck == 0)
    def _():
        acc_ref[...] = jnp.zeros_like(acc_ref)
        hx_s[...] = h_ref[0]
        for t in range(n_slab):
            rk_s[t] = _to_words(rk_ref[0, :, t * LANES:(t + 1) * LANES])
            e1_s[t] = _to_words(e1_ref[0, :, t * LANES:(t + 1) * LANES])

    for t in range(n_slab):
        cnt_s[t] = cnt_ref[0, 0, :, t * LANES:(t + 1) * LANES]
        e0_s[t] = e0_ref[0, 0, :, t * LANES:(t + 1) * LANES]

    def activations(s):
        r0 = pl.multiple_of(s * sw, sw)
        a = _gelu_tanh(_dot_nt(_from_words(u_ref[...], BF16), hx_s[pl.ds(r0, sw), :])).astype(BF16)
        for k in range(per):
            act_ref[s * per + k] = _to_words(a[:, k * LANES:(k + 1) * LANES])

    def gate_weights(t):
        jr = nk // j_group
        for ig in range(ic // il_group):
            for jg in range(j_group):
                wgt = [None] * il_group
                for h in range(PEER_HEADS):
                    cnt8 = cnt_s[t, h * ic:(h + 1) * ic, :]
                    e08 = e0_s[t, h * ic:(h + 1) * ic, :]
                    j0 = (h * nk + jg * jr) // pk
                    rk = _from_words(rk_s[t, j0:j0 + jr // pk, :], BF16)
                    e1 = _from_words(e1_s[t, j0:j0 + jr // pk, :], BF16)
                    for g in range(il_group):
                        il = ig * il_group + g
                        cnt_row = jnp.broadcast_to(cnt8[il:il + 1], (jr, LANES)).astype(BF16)
                        e0_row = jnp.broadcast_to(e08[il:il + 1], (jr, LANES)).astype(BF16)
                        term = jnp.where(rk < cnt_row, e1 * e0_row, jnp.zeros_like(e1))
                        wgt[g] = term if wgt[g] is None else wgt[g] + term
                for g in range(il_group):
                    r0 = ((ig * il_group + g) * nk + jg * jr) // pk
                    act = _from_words(act_ref[t, r0:r0 + jr // pk, :], BF16)
                    p_ref[t, r0:r0 + jr // pk, :] = _to_words(wgt[g] * act)

    def combine(s):
        for k in range(per):
            gate_weights(s * per + k)
        p = _from_words(jnp.concatenate([p_ref[s * per + k] for k in range(per)], axis=1), BF16)
        y = _dot(_from_words(vt_ref[...], BF16), p)
        for k in range(per):
            acc_ref[s * per + k] += y[:, k * LANES:(k + 1) * LANES]

    activations(0)

    def strip(s, carry):
        activations(s + 1)
        combine(s)
        return carry

    lax.fori_loop(0, n_strip - 1, strip, 0)
    combine(n_strip - 1)

    @pl.when(ck == pl.num_programs(2) - 1)
    def _():
        for t in range(n_slab):
            rows = slice(t * LANES, (t + 1) * LANES)
            o_ref[0, rows, :] = x_ref[0, rows, :] + g_ref[0] * acc_ref[t].T


PEER_STRIP = 256


def _rows_per_word(dt):
    return 4 // jnp.dtype(dt).itemsize


def _to_words(x):
    return pltpu.bitcast(x, jnp.uint32) if x.dtype.itemsize == 2 else x


def _from_words(x, dt):
    return pltpu.bitcast(x, dt) if jnp.dtype(dt).itemsize == 2 else x


def _pack_rows(x):
    if x.dtype.itemsize != 2:
        return x
    *lead, m, n = x.shape
    return lax.bitcast_convert_type(jnp.swapaxes(x.reshape(*lead, m // 2, 2, n), -1, -2), jnp.uint32)


def _pack_weight_kernel(w_ref, o_ref, *, transpose):
    w = w_ref[...].T if transpose else w_ref[...]
    o_ref[...] = _to_words(w.astype(BF16))


def _pack_weight(w, transpose):
    rows, cols = w.shape
    pk = _rows_per_word(BF16)
    blk = 1024
    if transpose:
        out_shape, out_spec = (cols // pk, rows), pl.BlockSpec((cols // pk, blk), lambda i: (0, i))
    else:
        out_shape, out_spec = (rows // pk, cols), pl.BlockSpec((blk // pk, cols), lambda i: (i, 0))
    return pl.pallas_call(
        functools.partial(_pack_weight_kernel, transpose=transpose),
        grid=(rows // blk,),
        in_specs=[pl.BlockSpec((blk, cols), lambda i: (i, 0))],
        out_specs=out_spec,
        out_shape=jax.ShapeDtypeStruct(out_shape, jnp.uint32 if pk == 2 else BF16),
        compiler_params=_cparams("arbitrary"),
        name="pack_weight",
    )(w)


def _peer_dense(hx, route, u, v_t, x, gate, tt):
    bn, sn, d = hx.shape
    nk, nh, ic = PEER_KEYS, PEER_HEADS, PEER_ICHUNK
    ne = ic * nk
    pk = _rows_per_word(BF16)
    wdt = jnp.uint32 if pk == 2 else BF16
    n_chunk = u.shape[0] * pk // ne
    spec_i = pl.BlockSpec((1, 1, nh * ic, tt), lambda b, i, c: (b, c, 0, i))
    spec_j = pl.BlockSpec((1, nh * nk, tt), lambda b, i, c: (b, 0, i))
    tok = pl.BlockSpec((1, tt, d), lambda b, i, c: (b, i, 0))
    n_slab = tt // LANES
    return pl.pallas_call(
        _peer_dense_kernel,
        grid=(bn, sn // tt, n_chunk),
        in_specs=[tok, spec_i, spec_j, spec_i, spec_j,
                  pl.BlockSpec((ne // pk, d), lambda b, i, c: (c, 0)),
                  pl.BlockSpec((d // pk, ne), lambda b, i, c: (0, c)),
                  pl.BlockSpec((1, tt, d), lambda b, i, c: (b, i, 0), pipeline_mode=pl.Buffered(1)),
                  pl.BlockSpec((1, 1, d), lambda b, i, c: (b, 0, 0))],
        out_specs=tok,
        out_shape=jax.ShapeDtypeStruct((bn, sn, d), F32),
        scratch_shapes=[pltpu.VMEM((n_slab, d, LANES), F32), pltpu.VMEM((n_slab, ne // pk, LANES), wdt),
                        pltpu.VMEM((n_slab, ne // pk, LANES), wdt), pltpu.VMEM((tt, d), BF16),
                        pltpu.VMEM((n_slab, nh * nk // pk, LANES), wdt),
                        pltpu.VMEM((n_slab, nh * nk // pk, LANES), wdt),
                        pltpu.VMEM((n_slab, nh * ic, LANES), F32), pltpu.VMEM((n_slab, nh * ic, LANES), F32)],
        compiler_params=_cparams("arbitrary", "arbitrary", "arbitrary"),
        name="peer_dense",
    )(hx, *route, u, v_t, x, gate)


def _peer(hx, x, gate, w_q, keys, u, v_t, tt=1024):
    tt = min(tt, hx.shape[1])
    route = _peer_route(hx, w_q, keys, tt)
    return _peer_dense(hx, route, u, v_t, x, gate, tt)


def _group_lanes(p, width):
    g = p.reshape(2, SSD_GROUPS, SSD_HPG).transpose(1, 0, 2).reshape(SSD_GROUPS, 2 * SSD_HPG)
    return jnp.pad(g, ((0, 0), (0, width - 2 * SSD_HPG))).reshape(SSD_GROUPS, 1, width)


def _block_diag(w, tile):
    two, nb, bd, _ = w.shape
    per = tile // bd
    w = w.reshape(two, nb // per, per, bd, bd)
    eye = jnp.eye(per, dtype=w.dtype)
    return jnp.einsum("dtpij,pq->dtpiqj", w, eye).reshape(two, nb // per, tile, tile)


def kernel(x, c, ctx, c_ctx, ada_w, ada_b, norm1_g, norm2_g, ev_w_in, ev_conv_w, ev_conv_b, ev_a_log,
           ev_dt_bias, ev_d, ev_ssd_norm_g, ev_lru_conv_w, ev_lru_conv_b, ev_lru_wa, ev_lru_ba, ev_lru_wx,
           ev_lru_bx, ev_lru_lam, ev_w_out, od_w_qkv, od_q_norm_g, od_k_norm_g, od_rpb, od_w_o,
           pe_w_q, pe_keys, pe_u, pe_v):
    bsz, sx, d = x.shape
    sc = ctx.shape[1]
    depth = ada_w.shape[0]

    n_c = bsz + 1
    rows = -(-n_c // SUBLANES) * SUBLANES
    c_all = jnp.concatenate([c, c_ctx[None], jnp.zeros((rows - n_c, d), F32)], axis=0)
    mods = _ada_mods(c_all, ada_w, ada_b).reshape(depth, rows, 6, d)

    ctx = ctx.reshape(1, bsz * sc, d)

    def per_batch(t):
        return t.reshape(bsz, sc, t.shape[-1])

    for layer in range(depth):
        last = layer == depth - 1
        j = layer // 2
        mod_x = mods[layer, :bsz]
        mod_c = mods[layer, bsz:bsz + 1]
        g1 = norm1_g[layer][None]
        g2 = norm2_g[layer][None]
        want_ctx = not last

        if layer % 2 == 0:
            w_in = ev_w_in[j]
            o_dt, o_xl = SSD_XBC, SSD_XBC + 2 * SSD_HEADS
            o_z = o_xl + LRU_WIDTH
            o_gate = o_z + SSD_INNER
            w_dt = w_in[:, o_dt:o_xl].reshape(d, 2, SSD_GROUPS, SSD_HPG).transpose(0, 2, 1, 3)
            w_dt = jnp.pad(w_dt.reshape(d, SSD_GROUPS, 2 * SSD_HPG), ((0, 0), (0, 0), (0, LANES - 2 * SSD_HPG)))
            ws = [w_in[:, :o_dt].astype(BF16), w_dt.reshape(d, SSD_GROUPS * LANES).astype(BF16),
                  w_in[:, o_xl:o_z].astype(BF16), w_in[:, o_z:o_gate].astype(BF16), w_in[:, o_gate:].astype(BF16)]
            dts = [BF16, F32, BF16, BF16, BF16]
            px = _nm_linear(x, g1, mod_x[:, 0:2], ws, dts)
            pc = [per_batch(t) for t in _nm_linear(ctx, g1, mod_c[:, 0:2], ws, dts)]
            y_ssd_x, y_ssd_c = _ssd_mixer(
                px[0], pc[0], px[1], pc[1], px[3], pc[3], ev_conv_w[j], ev_conv_b[j][None],
                _group_lanes(ev_a_log[j], LANES), _group_lanes(ev_dt_bias[j], LANES),
                jnp.repeat(ev_d[j], SSD_HEAD_DIM).reshape(SSD_GROUPS, 1, SSD_GW),
                ev_ssd_norm_g[j].reshape(SSD_GROUPS, 1, SSD_GW))
            y_lru_x, y_lru_c = _lru_mixer(
                px[2], pc[2], px[4], pc[4], ev_lru_conv_w[j], ev_lru_conv_b[j][None],
                _block_diag(ev_lru_wa[j], LRU_TILE).astype(BF16), _block_diag(ev_lru_wx[j], LRU_TILE).astype(BF16),
                ev_lru_ba[j], ev_lru_bx[j], ev_lru_lam[j])
            w_out = ev_w_out[j].astype(BF16)
            w_outs = [w_out[:SSD_INNER], w_out[SSD_INNER:]]
            acts_x = [y_ssd_x, y_lru_x]
            acts_c = [y_ssd_c.reshape(1, bsz * sc, -1), y_lru_c.reshape(1, bsz * sc, -1)]
        else:
            w_qkv = od_w_qkv[j].astype(BF16)
            nd = w_qkv.shape[1] // 3
            ws = [w_qkv[:, :nd], w_qkv[:, nd:2 * nd], w_qkv[:, 2 * nd:]]
            gains = [jnp.tile(od_q_norm_g[j], NA_HEADS)[None], jnp.tile(od_k_norm_g[j], NA_HEADS)[None], None]
            q_x, k_x, v_x = _nm_linear(x, g1, mod_x[:, 0:2], ws, [BF16] * 3, gains)
            q_c, k_c, v_c = [per_batch(t) for t in _nm_linear(ctx, g1, mod_c[:, 0:2], ws, [BF16] * 3, gains)]
            table = _rpb_table(od_rpb[j])
            o_x, o_c = _na_attention(q_x, k_x, v_x, q_c, k_c, v_c, table, want_ctx)
            w_outs = [od_w_o[j].astype(BF16)]
            acts_x = [o_x]
            acts_c = [o_c.reshape(1, bsz * sc, -1)] if want_ctx else None

        w_q = pe_w_q[layer].reshape(d, PEER_HEADS, 2 * PEER_KEYS).transpose(1, 0, 2).astype(BF16)
        keys = pe_keys[layer].reshape(2 * PEER_HEADS, PEER_KEYS, -1).astype(BF16)
        u = _pack_weight(pe_u[layer], transpose=False)
        v_t = _pack_weight(pe_v[layer], transpose=True)

        x, hx = _out_linear(acts_x, w_outs, x, mod_x[:, 2:5], g2)
        x = _peer(hx, x, mod_x[:, 5:6], w_q, keys, u, v_t)
        if want_ctx:
            ctx, hc = _out_linear(acts_c, w_outs, ctx, mod_c[:, 2:5], g2)
            ctx = _peer(hc, ctx, mod_c[:, 5:6], w_q, keys, u, v_t)
    return x
```

```python
import functools
import math

import jax
import jax.numpy as jnp
from jax import lax
from jax.experimental import pallas as pl
from jax.experimental.pallas import tpu as pltpu

F32 = jnp.float32
BF16 = jnp.bfloat16
HIGHEST = lax.Precision.HIGHEST

EPS = 1e-6
NEG_INF = -1e30

GRID_W = 64
SSD_HEADS = 16
SSD_HEAD_DIM = 64
SSD_GROUPS = 4
SSD_HPG = SSD_HEADS // SSD_GROUPS
SSD_STATE = 128
SSD_CHUNK = 128
SSD_INNER = SSD_HEADS * SSD_HEAD_DIM
SSD_GW = SSD_INNER // SSD_GROUPS
SSD_XBC = SSD_INNER + 2 * SSD_GROUPS * SSD_STATE
LRU_WIDTH = 1024
LRU_BLOCKS = 16
LRU_BLOCK_DIM = LRU_WIDTH // LRU_BLOCKS
LRU_C = 8.0
LRU_TILE = 256
NA_HEADS = 16
NA_HEAD_DIM = 64
NA_KH = 8
NA_KW = 16
NA_QROWS = 4
NA_KROWS = 12
PEER_HEADS = 8
PEER_KEYS = 128
PEER_TOPK = 16
PEER_ICHUNK = 8

LANES = 128
SUBLANES = 8
VMEM_LIMIT_BYTES = 56 * 1024 * 1024


def _cparams(*sem):
    return pltpu.CompilerParams(dimension_semantics=sem, vmem_limit_bytes=VMEM_LIMIT_BYTES)


def _silu(x):
    return x * (1.0 / (1.0 + jnp.exp(-x)))


def _sigmoid(x):
    return 1.0 / (1.0 + jnp.exp(-x))


def _softplus(x):
    return jnp.maximum(x, 0.0) + jnp.log(1.0 + jnp.exp(-jnp.abs(x)))


def _gelu_tanh(x):
    k0 = -2.0 * math.sqrt(2.0 / math.pi)
    return x / (1.0 + jnp.exp(x * (k0 + (k0 * 0.044715) * (x * x))))


def _rms_mod(x, g, shift, scale):
    ms = jnp.mean(x * x, axis=-1, keepdims=True)
    y = x * lax.rsqrt(ms + EPS) * g
    return y * (1.0 + scale) + shift


def _dot(a, b):
    return jnp.dot(a, b, preferred_element_type=F32)


def _dot_nt(a, b):
    return lax.dot_general(a, b, (((1,), (1,)), ((), ())), preferred_element_type=F32)


def _ada_kernel(c_ref, w_ref, b_ref, o_ref):
    s = _silu(c_ref[...])
    o_ref[0] = jnp.dot(s, w_ref[0], preferred_element_type=F32, precision=HIGHEST) + b_ref[0]


def _ada_mods(c_all, ada_w, ada_b):
    depth, d, n = ada_w.shape
    rows = c_all.shape[0]
    tn = 1536
    return pl.pallas_call(
        _ada_kernel,
        grid=(depth, n // tn),
        in_specs=[pl.BlockSpec((rows, d), lambda l, j: (0, 0)),
                  pl.BlockSpec((1, d, tn), lambda l, j: (l, 0, j)),
                  pl.BlockSpec((1, 1, tn), lambda l, j: (l, 0, j))],
        out_specs=pl.BlockSpec((1, rows, tn), lambda l, j: (l, 0, j)),
        out_shape=jax.ShapeDtypeStruct((depth, rows, n), F32),
        compiler_params=_cparams("arbitrary", "arbitrary"),
        name="ada_mods",
    )(c_all, ada_w, ada_b.reshape(depth, 1, n))


def _head_block_ones(n):
    r = lax.broadcasted_iota(jnp.int32, (n, n), 0) // NA_HEAD_DIM
    c = lax.broadcasted_iota(jnp.int32, (n, n), 1) // NA_HEAD_DIM
    return (r == c).astype(F32)


def _nm_linear_kernel(*refs, n_out, head_norm, tn):
    x_ref, g_ref, mod_ref = refs[:3]
    w_refs = refs[3:3 + n_out]
    hg_refs = refs[3 + n_out:3 + n_out + sum(head_norm)]
    o_refs = refs[3 + n_out + sum(head_norm):]
    h = _rms_mod(x_ref[0], g_ref[...], mod_ref[0, 0:1, :], mod_ref[0, 1:2, :]).astype(BF16)
    hg_i = 0
    for w_ref, o_ref, hn in zip(w_refs, o_refs, head_norm):
        n = w_ref.shape[1]
        for j in range(n // tn):
            y = _dot(h, w_ref[:, j * tn:(j + 1) * tn])
            if hn:
                ss = jnp.dot(y * y, _head_block_ones(tn), preferred_element_type=F32, precision=HIGHEST)
                y = y * lax.rsqrt(ss * (1.0 / NA_HEAD_DIM) + EPS) * hg_refs[hg_i][:, j * tn:(j + 1) * tn]
            o_ref[0, :, j * tn:(j + 1) * tn] = y.astype(o_ref.dtype)
        hg_i += hn


def _nm_linear(x, g, mod, ws, out_dtypes, head_gains=None, tm=512, tn=256):
    bn, sn, d = x.shape
    tm = min(tm, sn)
    n_out = len(ws)
    head_gains = head_gains or [None] * n_out
    head_norm = tuple(hg is not None for hg in head_gains)
    hgs = [hg for hg in head_gains if hg is not None]
    in_specs = [pl.BlockSpec((1, tm, d), lambda b, i: (b, i, 0)),
                pl.BlockSpec((1, d), lambda b, i: (0, 0)),
                pl.BlockSpec((1, 2, d), lambda b, i: (b, 0, 0))]
    in_specs += [pl.BlockSpec(w.shape, lambda b, i: (0, 0)) for w in ws]
    in_specs += [pl.BlockSpec(hg.shape, lambda b, i: (0, 0)) for hg in hgs]
    out_specs = [pl.BlockSpec((1, tm, w.shape[1]), lambda b, i: (b, i, 0)) for w in ws]
    out_shape = [jax.ShapeDtypeStruct((bn, sn, w.shape[1]), dt) for w, dt in zip(ws, out_dtypes)]
    return pl.pallas_call(
        functools.partial(_nm_linear_kernel, n_out=n_out, head_norm=head_norm, tn=tn),
        grid=(bn, sn // tm),
        in_specs=in_specs, out_specs=out_specs, out_shape=out_shape,
        compiler_params=_cparams("arbitrary", "arbitrary"),
        name="nm_linear",
    )(x, g, mod, *ws, *hgs)


def _out_linear_kernel(*refs, n_in):
    a_refs = refs[:n_in]
    w_refs = refs[n_in:2 * n_in]
    x_ref, mod_ref, g_ref, xo_ref, ho_ref = refs[2 * n_in:]
    y = _dot(a_refs[0][0], w_refs[0][...])
    for a_ref, w_ref in zip(a_refs[1:], w_refs[1:]):
        y = y + _dot(a_ref[0], w_ref[...])
    xn = x_ref[0] + mod_ref[0, 0:1, :] * y
    xo_ref[0] = xn
    ho_ref[0] = _rms_mod(xn, g_ref[...], mod_ref[0, 1:2, :], mod_ref[0, 2:3, :]).astype(BF16)


def _out_linear(acts, ws, x, mod, g, tm=512):
    bn, sn, d = x.shape
    tm = min(tm, sn)
    n_in = len(acts)
    in_specs = [pl.BlockSpec((1, tm, a.shape[2]), lambda b, i: (b, i, 0)) for a in acts]
    in_specs += [pl.BlockSpec(w.shape, lambda b, i: (0, 0)) for w in ws]
    in_specs += [pl.BlockSpec((1, tm, d), lambda b, i: (b, i, 0)),
                 pl.BlockSpec((1, 3, d), lambda b, i: (b, 0, 0)),
                 pl.BlockSpec((1, d), lambda b, i: (0, 0))]
    return pl.pallas_call(
        functools.partial(_out_linear_kernel, n_in=n_in),
        grid=(bn, sn // tm),
        in_specs=in_specs,
        out_specs=[pl.BlockSpec((1, tm, d), lambda b, i: (b, i, 0))] * 2,
        out_shape=[jax.ShapeDtypeStruct((bn, sn, d), F32), jax.ShapeDtypeStruct((bn, sn, d), BF16)],
        compiler_params=_cparams("arbitrary", "arbitrary"),
        name="out_linear",
    )(*acts, *ws, x, mod, g)


CONV_HALO = 16


def _conv_chunk(src_ref, s, seg_len, w, bias, rows=SSD_CHUNK):
    ncol = src_ref.shape[2]
    if s > 0:
        prev = src_ref[0, s - CONV_HALO:s, :].astype(F32)
    else:
        prev = jnp.zeros((CONV_HALO, ncol), F32)
    cur = src_ref[0, s:s + rows, :].astype(F32)
    if s + rows < seg_len:
        nxt = src_ref[0, s + rows:s + rows + CONV_HALO, :].astype(F32)
    else:
        nxt = jnp.zeros((CONV_HALO, ncol), F32)
    win = jnp.concatenate([prev, cur, nxt], axis=0)
    taps = w.shape[0]
    acc = bias
    for k in range(taps):
        off = CONV_HALO - taps // 2 + k
        acc = acc + win[off:off + rows, :] * w[k:k + 1, :]
    return acc


def _lane_head_expand(cols, width):
    nh = len(cols)
    hd = width // nh
    rows = cols[0].shape[0]
    lane_head = lax.broadcasted_iota(jnp.int32, (rows, width), 1) // hd
    out = jnp.broadcast_to(cols[nh - 1], (rows, width))
    for h in range(nh - 2, -1, -1):
        out = jnp.where(lane_head == h, jnp.broadcast_to(cols[h], (rows, width)), out)
    return out


def _ssd_kernel(xs_x, bm_x, cm_x, xs_c, bm_c, cm_c, dt_x, dt_c, z_x, z_c,
                cw_xs, cw_b, cw_c, cb_xs, cb_b, cb_c, alog_ref, dtb_ref, dsk_ref, ng_ref,
                y_x, y_c,
                xs_s, bm_s, cm_s, dt_s, y_s, st_s, cs_s, cst_s, *, sx, sc):
    q = SSD_CHUNK
    nc_c, nc_x = sc // q, sx // q
    nc = nc_c + nc_x
    gw = xs_s.shape[1]

    for seg_ref3, seg_len, base in (((xs_c, bm_c, cm_c), sc, 0), ((xs_x, bm_x, cm_x), sx, sc)):
        for ci in range(seg_len // q):
            s = ci * q
            for src, dst, w_ref, b_ref in zip(seg_ref3, (xs_s, bm_s, cm_s), (cw_xs, cw_b, cw_c),
                                              (cb_xs, cb_b, cb_c)):
                dst[base + s:base + s + q, :] = _silu(_conv_chunk(src, s, seg_len, w_ref[...], b_ref[...]))
    dt_s[0:sc, :] = _softplus(dt_c[0] + dtb_ref[0])
    dt_s[sc:sc + sx, :] = _softplus(dt_x[0] + dtb_ref[0])

    a_neg = -jnp.exp(alog_ref[0])
    row = lax.broadcasted_iota(jnp.int32, (q, q), 0)
    col = lax.broadcasted_iota(jnp.int32, (q, q), 1)
    tri = ((col <= row).astype(F32), (col >= row).astype(F32))
    keep = (col <= row, col >= row)
    lane_head = lax.broadcasted_iota(jnp.int32, (q, gw), 1) // SSD_HEAD_DIM

    st_s[...] = jnp.zeros_like(st_s)
    y_s[...] = jnp.zeros_like(y_s)

    for ci in range(nc):
        la = dt_s[ci * q:(ci + 1) * q, :] * a_neg
        for d in range(2):
            cs = jnp.dot(tri[d], la, preferred_element_type=F32, precision=HIGHEST)
            cs_s[d, ci * q:(ci + 1) * q, :] = cs
            cst_s[d, ci * q:(ci + 1) * q, :] = cs.T

    def chunk_body(i, carry):
        for d in range(2):
            if d == 0:
                ci = i
            else:
                ci = jnp.where(i < nc_c, nc_c - 1 - i, nc + nc_c - 1 - i)
            r0 = pl.multiple_of(ci * q, q)
            xs = xs_s[pl.ds(r0, q), :]
            bm = bm_s[pl.ds(r0, q), :]
            cm = cm_s[pl.ds(r0, q), :]
            dt = dt_s[pl.ds(r0, q), :]
            cs = cs_s[d, pl.ds(r0, q), :]
            cs_t = cst_s[d, pl.ds(r0, q), :]
            cb = _dot_nt(cm.astype(BF16), bm.astype(BF16))
            heads = [d * SSD_HPG + h for h in range(SSD_HPG)]
            dt_mat = _lane_head_expand([dt[:, c:c + 1] for c in heads], gw)
            cs_mat = _lane_head_expand([cs[:, c:c + 1] for c in heads], gw)
            xd = xs * dt_mat
            xd_b = xd.astype(BF16)
            y = jnp.zeros((q, gw), F32)
            for h, c in enumerate(heads):
                diff = cs[:, c:c + 1] - cs_t[c:c + 1, :]
                lmat = jnp.exp(jnp.where(keep[d], diff, NEG_INF))
                y = jnp.where(lane_head == h, _dot((cb * lmat).astype(BF16), xd_b), y)
            st = st_s[d]
            y = y + _dot(cm.astype(BF16), st.astype(BF16)) * jnp.exp(cs_mat)
            end = q - 1 if d == 0 else 0
            cs_end = cs_mat[end:end + 1, :]
            s_new = _dot(bm.T.astype(BF16), (xd * jnp.exp(cs_end - cs_mat)).astype(BF16))
            st_s[d] = st * jnp.exp(cs_end) + s_new
            y_s[pl.ds(r0, q), :] = y_s[pl.ds(r0, q), :] + y
        return carry

    lax.fori_loop(0, nc, chunk_body, 0)

    for ci in range(nc):
        s = ci * q
        if ci < nc_c:
            z = z_c[0, s:s + q, :]
        else:
            z = z_x[0, s - sc:s - sc + q, :]
        y = (y_s[s:s + q, :] + dsk_ref[0] * xs_s[s:s + q, :]) * _silu(z.astype(F32))
        ms = jnp.mean(y * y, axis=-1, keepdims=True)
        out = (y * lax.rsqrt(ms + EPS) * ng_ref[0]).astype(BF16)
        if ci < nc_c:
            y_c[0, s:s + q, :] = out
        else:
            y_x[0, s - sc:s - sc + q, :] = out


def _ssd_mixer(xbc_x, xbc_c, dt_x, dt_c, z_x, z_c, conv_w, conv_b, alog_g, dtb_g, dsk_g, ng_g):
    bsz, sx, _ = xbc_x.shape
    sc = xbc_c.shape[1]
    g, gw, n = SSD_GROUPS, SSD_GW, SSD_STATE
    nb = SSD_INNER // n
    taps = conv_w.shape[0]

    def seq(s, w, off):
        return pl.BlockSpec((1, s, w), lambda b, j, off=off: (b, 0, off + j))

    def par(r, w, off):
        return pl.BlockSpec((r, w), lambda b, j, off=off: (0, off + j))

    def grp(w):
        return pl.BlockSpec((1, 1, w), lambda b, j: (j, 0, 0))

    in_specs = [seq(sx, gw, 0), seq(sx, n, nb), seq(sx, n, nb + g),
                seq(sc, gw, 0), seq(sc, n, nb), seq(sc, n, nb + g),
                seq(sx, LANES, 0), seq(sc, LANES, 0), seq(sx, gw, 0), seq(sc, gw, 0),
                par(taps, gw, 0), par(taps, n, nb), par(taps, n, nb + g),
                par(1, gw, 0), par(1, n, nb), par(1, n, nb + g),
                grp(LANES), grp(LANES), grp(gw), grp(gw)]
    stot = sx + sc
    return pl.pallas_call(
        functools.partial(_ssd_kernel, sx=sx, sc=sc),
        grid=(bsz, g),
        in_specs=in_specs,
        out_specs=[seq(sx, gw, 0), seq(sc, gw, 0)],
        out_shape=[jax.ShapeDtypeStruct((bsz, sx, SSD_INNER), BF16),
                   jax.ShapeDtypeStruct((bsz, sc, SSD_INNER), BF16)],
        scratch_shapes=[pltpu.VMEM((stot, gw), F32), pltpu.VMEM((stot, n), F32), pltpu.VMEM((stot, n), F32),
                        pltpu.VMEM((stot, LANES), F32), pltpu.VMEM((stot, gw), F32), pltpu.VMEM((2, n, gw), F32),
                        pltpu.VMEM((2, stot, LANES), F32), pltpu.VMEM((2, stot, LANES), F32)],
        compiler_params=_cparams("arbitrary", "arbitrary"),
        name="ssd_mixer",
    )(xbc_x, xbc_x, xbc_x, xbc_c, xbc_c, xbc_c, dt_x, dt_c, z_x, z_c,
      conv_w, conv_w, conv_w, conv_b, conv_b, conv_b, alog_g, dtb_g, dsk_g, ng_g)


def _lru_kernel(xl_x, xl_c, gt_x, gt_c, cw, cb, wa, wx, ba, bx, lam, y_x, y_c,
                xr_s, a_s, b_s, y_s, *, sx, sc):
    q = SSD_CHUNK
    stot = sx + sc
    w = xr_s.shape[1]
    for src, seg_len, base in ((xl_c, sc, 0), (xl_x, sx, sc)):
        for ci in range(seg_len // q):
            s = ci * q
            xr_s[base + s:base + s + q, :] = _conv_chunk(src, s, seg_len, cw[...], cb[...])

    ng = stot // SUBLANES
    ng_c = sc // SUBLANES
    sub = lax.broadcasted_iota(jnp.int32, (SUBLANES, w), 0)
    rt = q
    for d in range(2):
        nsp = _softplus(-lam[d:d + 1, :])
        for ci in range(stot // rt):
            s = ci * rt
            xr = xr_s[s:s + rt, :]
            xb = xr.astype(BF16)
            r = _sigmoid(_dot(xb, wa[d, 0]) + ba[d:d + 1, :])
            ig = _sigmoid(_dot(xb, wx[d, 0]) + bx[d:d + 1, :])
            a = jnp.exp(-LRU_C * r * nsp)
            a_s[d, s:s + rt, :] = a
            b_s[d, s:s + rt, :] = jnp.sqrt(1.0 - a * a) * (ig * xr)

    y_s[...] = jnp.zeros_like(y_s)

    def group_body(k, carries):
        new = []
        for d in range(2):
            if d == 0:
                gi = k
            else:
                gi = jnp.where(k < ng_c, ng_c - 1 - k, ng + ng_c - 1 - k)
            r0 = pl.multiple_of(gi * SUBLANES, SUBLANES)
            a = a_s[d, pl.ds(r0, SUBLANES), :]
            b = b_s[d, pl.ds(r0, SUBLANES), :]
            for sh in (1, 2, 4):
                if d == 0:
                    valid = sub >= sh
                    a_sh = pltpu.roll(a, sh, axis=0)
                    b_sh = pltpu.roll(b, sh, axis=0)
                else:
                    valid = sub < SUBLANES - sh
                    a_sh = pltpu.roll(a, SUBLANES - sh, axis=0)
                    b_sh = pltpu.roll(b, SUBLANES - sh, axis=0)
                b = jnp.where(valid, a * b_sh + b, b)
                a = jnp.where(valid, a * a_sh, a)
            h = a * carries[d] + b
            y_s[pl.ds(r0, SUBLANES), :] = y_s[pl.ds(r0, SUBLANES), :] + h
            last = h[SUBLANES - 1:SUBLANES, :] if d == 0 else h[0:1, :]
            new.append(jnp.broadcast_to(last, (SUBLANES, w)))
        return tuple(new)

    zero = jnp.zeros((SUBLANES, w), F32)
    lax.fori_loop(0, ng, group_body, (zero, zero))

    for ci in range(stot // rt):
        s = ci * rt
        if s < sc:
            gate = gt_c[0, s:s + rt, :]
        else:
            gate = gt_x[0, s - sc:s - sc + rt, :]
        out = (y_s[s:s + rt, :] * _gelu_tanh(gate.astype(F32))).astype(BF16)
        if s < sc:
            y_c[0, s:s + rt, :] = out
        else:
            y_x[0, s - sc:s - sc + rt, :] = out


def _lru_mixer(xl_x, xl_c, gt_x, gt_c, conv_w, conv_b, wa_bd, wx_bd, ba, bx, lam):
    bsz, sx, width = xl_x.shape
    sc = xl_c.shape[1]
    w = LRU_TILE
    taps = conv_w.shape[0]

    def seq(s):
        return pl.BlockSpec((1, s, w), lambda b, j: (b, 0, j))

    def par(r):
        return pl.BlockSpec((r, w), lambda b, j: (0, j))

    wspec = pl.BlockSpec((2, 1, w, w), lambda b, j: (0, j, 0, 0))
    stot = sx + sc
    return pl.pallas_call(
        functools.partial(_lru_kernel, sx=sx, sc=sc),
        grid=(bsz, width // w),
        in_specs=[seq(sx), seq(sc), seq(sx), seq(sc), par(taps), par(1), wspec, wspec, par(2), par(2), par(2)],
        out_specs=[seq(sx), seq(sc)],
        out_shape=[jax.ShapeDtypeStruct((bsz, sx, width), BF16), jax.ShapeDtypeStruct((bsz, sc, width), BF16)],
        scratch_shapes=[pltpu.VMEM((stot, w), F32), pltpu.VMEM((2, stot, w), F32), pltpu.VMEM((2, stot, w), F32),
                        pltpu.VMEM((stot, w), F32)],
        compiler_params=_cparams("arbitrary", "arbitrary"),
        name="lru_mixer",
    )(xl_x, xl_c, gt_x, gt_c, conv_w, conv_b, wa_bd, wx_bd, ba, bx, lam)


def _rpb_table_kernel(rpb_ref, o_ref, *, n_dr, n_dc):
    h = pl.program_id(0)
    w = GRID_W
    qcol = lax.broadcasted_iota(jnp.int32, (w, 2 * w), 0)
    lane = lax.broadcasted_iota(jnp.int32, (w, 2 * w), 1)
    kcol = lane % w
    hi = lane >= w
    rel = kcol - qcol + (NA_KW - 1)
    cstart = jnp.clip(qcol - NA_KW // 2, 0, w - NA_KW)
    in_win = (kcol >= cstart) & (kcol < cstart + NA_KW)
    for d in range(n_dr + 1):
        acc = jnp.full((w, 2 * w), NEG_INF, F32)
        for dc in range(n_dc):
            lo = rpb_ref[(h * n_dr + d - 1) * n_dc + dc] if d >= 1 else NEG_INF
            up = rpb_ref[(h * n_dr + d) * n_dc + dc] if d < n_dr else NEG_INF
            acc = jnp.where(rel == dc, jnp.where(hi, up, lo), acc)
        valid = in_win
        if d == 0:
            valid = valid & hi
        if d == n_dr:
            valid = valid & jnp.logical_not(hi)
        o_ref[0, d] = jnp.where(valid, acc, NEG_INF)


def _rpb_table(rpb):
    nh, n_dr, n_dc = rpb.shape
    return pl.pallas_call(
        functools.partial(_rpb_table_kernel, n_dr=n_dr, n_dc=n_dc),
        grid=(nh,),
        in_specs=[pl.BlockSpec(memory_space=pltpu.SMEM)],
        out_specs=pl.BlockSpec((1, n_dr + 1, GRID_W, 2 * GRID_W), lambda h: (h, 0, 0, 0)),
        out_shape=jax.ShapeDtypeStruct((nh, n_dr + 1, GRID_W, 2 * GRID_W), F32),
        compiler_params=_cparams("arbitrary"),
        name="rpb_table",
    )(rpb.reshape(-1))


def _na_kernel(*refs, sx, sc, want_ctx):
    if want_ctx:
        q_x, k_x, v_x, q_c, k_c, v_c, tab, o_x, o_c = refs
    else:
        q_x, k_x, v_x, k_c, v_c, tab, o_x = refs
    w = GRID_W
    rows = sx // w
    qb = NA_QROWS * w
    kb = NA_KROWS * w
    n_blk = rows // NA_QROWS
    scale = NA_HEAD_DIM ** -0.5
    lane = lax.broadcasted_iota(jnp.int32, (1, 2 * NA_HEAD_DIM), 1)
    in_head = (lane < NA_HEAD_DIM, lane >= NA_HEAD_DIM)
    keyrow = lax.broadcasted_iota(jnp.int32, (1, kb), 1) // w
    kc = k_c[0]
    vc = v_c[0]
    n_tab = tab.shape[1]

    def softmax_pv(parts):
        m = parts[0][0].max(axis=-1, keepdims=True)
        for s, _ in parts[1:]:
            m = jnp.maximum(m, s.max(axis=-1, keepdims=True))
        acc, den = None, None
        for s, v in parts:
            e = jnp.exp(s - m)
            den = e.sum(axis=-1, keepdims=True) if den is None else den + e.sum(axis=-1, keepdims=True)
            pv = _dot(e.astype(BF16), v)
            acc = pv if acc is None else acc + pv
        return acc / den

    def block_body(rb, carry):
        ws = jnp.clip(NA_QROWS * rb - NA_KH // 2, 0, rows - NA_KROWS)
        q0 = pl.multiple_of(rb * qb, qb)
        k0 = pl.multiple_of(ws * w, w)
        qblk = q_x[0, pl.ds(q0, qb), :]
        kwin = k_x[0, pl.ds(k0, kb), :]
        vwin = v_x[0, pl.ds(k0, kb), :]
        out = jnp.zeros((qb, 2 * NA_HEAD_DIM), F32)
        for hh in range(2):
            qm = jnp.where(in_head[hh], qblk * scale, jnp.zeros_like(qblk))
            s_loc = _dot_nt(qm, kwin)
            s_ctx = _dot_nt(qm, kc)
            pieces = []
            for rq in range(NA_QROWS):
                r = NA_QROWS * rb + rq
                rs = jnp.clip(r - NA_KH // 2, 0, rows - NA_KH)
                lo = rs - ws
                valid = (keyrow >= lo) & (keyrow < lo + NA_KH)
                blocks = []
                for ip in range(NA_KROWS // 2):
                    dr_lo = ws + 2 * ip - r + NA_KH - 1
                    blocks.append(tab[hh, jnp.clip(dr_lo + 1, 0, n_tab - 1)])
                bias = jnp.concatenate(blocks, axis=1)
                piece = s_loc[rq * w:(rq + 1) * w, :] + bias
                pieces.append(jnp.where(valid, piece, NEG_INF))
            s_loc = jnp.concatenate(pieces, axis=0)
            o = softmax_pv([(s_loc, vwin), (s_ctx, vc)])
            out = jnp.where(in_head[hh], o, out)
        o_x[0, pl.ds(q0, qb), :] = out.astype(o_x.dtype)
        return carry

    lax.fori_loop(0, n_blk, block_body, 0)

    if want_ctx:
        qc = q_c[0]
        out = jnp.zeros((sc, 2 * NA_HEAD_DIM), F32)
        for hh in range(2):
            qm = jnp.where(in_head[hh], qc * scale, jnp.zeros_like(qc))
            o = softmax_pv([(_dot_nt(qm, kc), vc)])
            out = jnp.where(in_head[hh], o, out)
        o_c[0] = out.astype(o_c.dtype)


def _na_attention(q_x, k_x, v_x, q_c, k_c, v_c, table, want_ctx):
    bsz, sx, dim = q_x.shape
    sc = k_c.shape[1]
    pw = 2 * NA_HEAD_DIM
    n_pair = dim // pw

    def seq(s):
        return pl.BlockSpec((1, s, pw), lambda p, b: (b, 0, p))

    tspec = pl.BlockSpec((2,) + table.shape[1:], lambda p, b: (p, 0, 0, 0))
    if want_ctx:
        args = (q_x, k_x, v_x, q_c, k_c, v_c, table)
        in_specs = [seq(sx)] * 3 + [seq(sc)] * 3 + [tspec]
        out_specs = [seq(sx), seq(sc)]
        out_shape = [jax.ShapeDtypeStruct((bsz, sx, dim), BF16), jax.ShapeDtypeStruct((bsz, sc, dim), BF16)]
    else:
        args = (q_x, k_x, v_x, k_c, v_c, table)
        in_specs = [seq(sx)] * 3 + [seq(sc)] * 2 + [tspec]
        out_specs = [seq(sx)]
        out_shape = [jax.ShapeDtypeStruct((bsz, sx, dim), BF16)]
    res = pl.pallas_call(
        functools.partial(_na_kernel, sx=sx, sc=sc, want_ctx=want_ctx),
        grid=(n_pair, bsz),
        in_specs=in_specs, out_specs=out_specs, out_shape=out_shape,
        compiler_params=_cparams("arbitrary", "arbitrary"),
        name="na_attention",
    )(*args)
    return res if want_ctx else (res[0], None)


def _top_rows(s, k, exact, want_rank=True):
    n, tt = s.shape
    top_id = lax.broadcasted_iota(jnp.int32, (k, tt), 0)
    if exact:
        rowid = lax.broadcasted_iota(jnp.int32, (n, tt), 0).astype(F32)
    work = s
    top = jnp.zeros((k, tt), F32)
    rank = jnp.full((n, tt), float(k), F32) if want_rank else None
    for it in range(k):
        m = jnp.max(work, axis=0, keepdims=True)
        sel = work == m
        if exact:
            sel = rowid == jnp.min(jnp.where(sel, rowid, float(n)), axis=0, keepdims=True)
        top = jnp.where(top_id == it, m, top)
        if want_rank:
            rank = jnp.where(sel, float(it), rank)
        work = jnp.where(sel, -jnp.inf, work)
    picked = work == -jnp.inf
    n_sel = jnp.sum(jnp.where(picked, 1.0, 0.0), axis=0, keepdims=True)
    return top, rank, picked, n_sel


def _peer_route_kernel(h_ref, wq_ref, keys_ref, cnt_ref, rk_ref, e0_ref, e1_ref, s_s, top_s, rank_s, cnti_s, z_s):
    nk, k, ic = PEER_KEYS, PEER_TOPK, PEER_ICHUNK
    tt = h_ref.shape[1]
    sw = top_s.shape[2]
    n_strip = tt // sw
    n_unit = PEER_HEADS * n_strip

    def scores(u):
        h, si = u // n_strip, u % n_strip
        rows = pl.ds(pl.multiple_of(si * sw, sw), sw)
        q = _dot(h_ref[0, rows, :], wq_ref[h]).astype(BF16)
        for z in range(2):
            s_s[u % 2, z] = _dot_nt(keys_ref[2 * h + z], q[:, z * nk:(z + 1) * nk])

    def pair_stage(t0, t1, exact):
        cand = jnp.concatenate([t0[0:1] + t1] + [t0[a:a + 1] + t1[0:8] for a in range(1, 8)]
                               + [t0[8:16] + t1[0:1]], axis=0)
        _, _, picked, n_sel = _top_rows(cand, k, exact, want_rank=False)
        pf = jnp.where(picked, 1.0, 0.0)
        z_sum = jnp.sum(pf * jnp.exp(cand - cand[0:1]), axis=0, keepdims=True)
        cnts = [jnp.sum(pf[0:k], axis=0, keepdims=True)]
        cnts += [jnp.sum(pf[k + 8 * (a - 1):k + 8 * a], axis=0, keepdims=True) for a in range(1, 8)]
        cnts += [pf[k + 56 + a:k + 57 + a] for a in range(8)]
        return cnts, z_sum, n_sel

    def route(u, prefetch):
        h, si = u // n_strip, u % n_strip
        cols = pl.ds(pl.multiple_of(si * sw, sw), sw)
        s0, s1 = s_s[u % 2, 0], s_s[u % 2, 1]
        if prefetch:
            scores(u + 1)

        t0, _, _, n0 = _top_rows(s0, k, False, want_rank=False)
        t1, rank1, _, n1 = _top_rows(s1, k, False)
        cnts, z_sum, n2 = pair_stage(t0, t1, False)
        cnt_i = jnp.zeros_like(s0)
        for a in range(k):
            cnt_i = jnp.where(s0 == t0[a:a + 1], cnts[a], cnt_i)
        top_s[0], top_s[1] = t0, t1
        rank_s[...] = rank1
        cnti_s[...] = cnt_i
        z_s[...] = z_sum
        ties = jnp.max(jnp.abs(n0 - float(k)) + jnp.abs(n1 - float(k)) + jnp.abs(n2 - float(k)))

        @pl.when(ties > 0.5)
        def _():
            t0, rank0, _, _ = _top_rows(s0, k, True)
            t1, rank1, _, _ = _top_rows(s1, k, True)
            cnts, z_sum, _ = pair_stage(t0, t1, True)
            cnt_i = jnp.zeros_like(s0)
            for a in range(k):
                cnt_i = jnp.where(rank0 == float(a), cnts[a], cnt_i)
            top_s[0], top_s[1] = t0, t1
            rank_s[...] = rank1
            cnti_s[...] = cnt_i
            z_s[...] = z_sum

        cnt_i = cnti_s[...]
        e0 = jnp.exp(s0 - top_s[0, 0:1]) / z_s[...]
        hi = pl.ds(pl.multiple_of(h * ic, ic), ic)
        hj = pl.ds(pl.multiple_of(h * nk, nk), nk)
        for c in range(nk // ic):
            cnt_ref[0, c, hi, cols] = cnt_i[c * ic:(c + 1) * ic]
            e0_ref[0, c, hi, cols] = e0[c * ic:(c + 1) * ic]
        rk_ref[0, hj, cols] = rank_s[...].astype(rk_ref.dtype)
        e1_ref[0, hj, cols] = jnp.exp(s1 - top_s[1, 0:1]).astype(e1_ref.dtype)

    scores(jnp.int32(0))

    def unit(u, carry):
        route(u, True)
        return carry

    lax.fori_loop(0, n_unit - 1, unit, 0)
    route(jnp.int32(n_unit - 1), False)


ROUTE_STRIP = 256


def _peer_route(hx, w_q, keys, tt):
    bn, sn, d = hx.shape
    nk, nh, ic = PEER_KEYS, PEER_HEADS, PEER_ICHUNK
    spec_i = pl.BlockSpec((1, nk // ic, nh * ic, tt), lambda b, i: (b, 0, 0, i))
    spec_j = pl.BlockSpec((1, nh * nk, tt), lambda b, i: (b, 0, i))
    shape_i = jax.ShapeDtypeStruct((bn, nk // ic, nh * ic, sn), F32)
    shape_j = jax.ShapeDtypeStruct((bn, nh * nk, sn), BF16)
    sw = min(ROUTE_STRIP, tt)
    return pl.pallas_call(
        _peer_route_kernel,
        grid=(bn, sn // tt),
        in_specs=[pl.BlockSpec((1, tt, d), lambda b, i: (b, i, 0)),
                  pl.BlockSpec(w_q.shape, lambda b, i: (0, 0, 0)),
                  pl.BlockSpec(keys.shape, lambda b, i: (0, 0, 0))],
        out_specs=[spec_i, spec_j, spec_i, spec_j],
        out_shape=[shape_i, shape_j, shape_i, shape_j],
        scratch_shapes=[pltpu.VMEM((2, 2, nk, sw), F32), pltpu.VMEM((2, PEER_TOPK, sw), F32),
                        pltpu.VMEM((nk, sw), F32), pltpu.VMEM((nk, sw), F32), pltpu.VMEM((1, sw), F32)],
        compiler_params=_cparams("arbitrary", "arbitrary"),
        name="peer_route",
    )(hx, w_q, keys)


def _peer_dense_kernel(h_ref, cnt_ref, rk_ref, e0_ref, e1_ref, u_ref, vt_ref, x_ref, g_ref, o_ref,
                       acc_ref, act_ref, p_ref, hx_s, rk_s, e1_s, cnt_s, e0_s):
    nk, ic = PEER_KEYS, PEER_ICHUNK
    ck = pl.program_id(2)
    n_slab = act_ref.shape[0]
    sw = min(PEER_STRIP, n_slab * LANES)
    per = sw // LANES
    n_strip = n_slab // per
    il_group, j_group = 4, 2
    pk = _rows_per_word(BF16)

    @pl.when(ck == 0)
    def _():
        acc_ref[...] = jnp.zeros_like(acc_ref)
        hx_s[...] = h_ref[0]
        for t in range(n_slab):
            rk_s[t] = _to_words(rk_ref[0, :, t * LANES:(t + 1) * LANES])
            e1_s[t] = _to_words(e1_ref[0, :, t * LANES:(t + 1) * LANES])

    for t in range(n_slab):
        cnt_s[t] = cnt_ref[0, 0, :, t * LANES:(t + 1) * LANES]
        e0_s[t] = e0_ref[0, 0, :, t * LANES:(t + 1) * LANES]

    def activations(s):
        r0 = pl.multiple_of(s * sw, sw)
        a = _gelu_tanh(_dot_nt(_from_words(u_ref[...], BF16), hx_s[pl.ds(r0, sw), :])).astype(BF16)
        for k in range(per):
            act_ref[s * per + k] = _to_words(a[:, k * LANES:(k + 1) * LANES])

    def gate_weights(t):
        jr = nk // j_group
        for ig in range(ic // il_group):
            for jg in range(j_group):
                wgt = [None] * il_group
                for h in range(PEER_HEADS):
                    cnt8 = cnt_s[t, h * ic:(h + 1) * ic, :]
                    e08 = e0_s[t, h * ic:(h + 1) * ic, :]
                    j0 = (h * nk + jg * jr) // pk
                    rk = _from_words(rk_s[t, j0:j0 + jr // pk, :], BF16)
                    e1 = _from_words(e1_s[t, j0:j0 + jr // pk, :], BF16)
                    for g in range(il_group):
                        il = ig * il_group + g
                        cnt_row = jnp.broadcast_to(cnt8[il:il + 1], (jr, LANES)).astype(BF16)
                        e0_row = jnp.broadcast_to(e08[il:il + 1], (jr, LANES)).astype(BF16)
                        term = jnp.where(rk < cnt_row, e1 * e0_row, jnp.zeros_like(e1))
                        wgt[g] = term if wgt[g] is None else wgt[g] + term
                for g in range(il_group):
                    r0 = ((ig * il_group + g) * nk + jg * jr) // pk
                    act = _from_words(act_ref[t, r0:r0 + jr // pk, :], BF16)
                    p_ref[t, r0:r0 + jr // pk, :] = _to_words(wgt[g] * act)

    def combine(s):
        for k in range(per):
            gate_weights(s * per + k)
        p = _from_words(jnp.concatenate([p_ref[s * per + k] for k in range(per)], axis=1), BF16)
        y = _dot(_from_words(vt_ref[...], BF16), p)
        for k in range(per):
            acc_ref[s * per + k] += y[:, k * LANES:(k + 1) * LANES]

    activations(0)

    def strip(s, carry):
        activations(s + 1)
        combine(s)
        return carry

    lax.fori_loop(0, n_strip - 1, strip, 0)
    combine(n_strip - 1)

    @pl.when(ck == pl.num_programs(2) - 1)
    def _():
        for t in range(n_slab):
            rows = slice(t * LANES, (t + 1) * LANES)
            o_ref[0, rows, :] = x_ref[0, rows, :] + g_ref[0] * acc_ref[t].T


PEER_STRIP = 256


def _rows_per_word(dt):
    return 4 // jnp.dtype(dt).itemsize


def _to_words(x):
    return pltpu.bitcast(x, jnp.uint32) if x.dtype.itemsize == 2 else x


def _from_words(x, dt):
    return pltpu.bitcast(x, dt) if jnp.dtype(dt).itemsize == 2 else x


def _pack_weight_kernel(w_ref, o_ref, *, transpose):
    w = w_ref[0].T if transpose else w_ref[0]
    o_ref[...] = _to_words(w.astype(BF16))


def _pack_weight(w, layer, transpose):
    _, rows, cols = w.shape
    pk = _rows_per_word(BF16)
    blk = 1024
    if transpose:
        out_shape, out_spec = (cols // pk, rows), pl.BlockSpec((cols // pk, blk), lambda i: (0, i))
    else:
        out_shape, out_spec = (rows // pk, cols), pl.BlockSpec((blk // pk, cols), lambda i: (i, 0))
    return pl.pallas_call(
        functools.partial(_pack_weight_kernel, transpose=transpose),
        grid=(rows // blk,),
        in_specs=[pl.BlockSpec((1, blk, cols), lambda i: (layer, i, 0))],
        out_specs=out_spec,
        out_shape=jax.ShapeDtypeStruct(out_shape, jnp.uint32 if pk == 2 else BF16),
        compiler_params=_cparams("arbitrary"),
        name="pack_weight",
    )(w)


def _peer_dense(hx, route, u, v_t, x, gate, tt):
    bn, sn, d = hx.shape
    nk, nh, ic = PEER_KEYS, PEER_HEADS, PEER_ICHUNK
    ne = ic * nk
    pk = _rows_per_word(BF16)
    wdt = jnp.uint32 if pk == 2 else BF16
    n_chunk = u.shape[0] * pk // ne
    spec_i = pl.BlockSpec((1, 1, nh * ic, tt), lambda b, i, c: (b, c, 0, i))
    spec_j = pl.BlockSpec((1, nh * nk, tt), lambda b, i, c: (b, 0, i))
    tok = pl.BlockSpec((1, tt, d), lambda b, i, c: (b, i, 0))
    n_slab = tt // LANES
    return pl.pallas_call(
        _peer_dense_kernel,
        grid=(bn, sn // tt, n_chunk),
        in_specs=[tok, spec_i, spec_j, spec_i, spec_j,
                  pl.BlockSpec((ne // pk, d), lambda b, i, c: (c, 0)),
                  pl.BlockSpec((d // pk, ne), lambda b, i, c: (0, c)),
                  pl.BlockSpec((1, tt, d), lambda b, i, c: (b, i, 0), pipeline_mode=pl.Buffered(1)),
                  pl.BlockSpec((1, 1, d), lambda b, i, c: (b, 0, 0))],
        out_specs=tok,
        out_shape=jax.ShapeDtypeStruct((bn, sn, d), F32),
        scratch_shapes=[pltpu.VMEM((n_slab, d, LANES), F32), pltpu.VMEM((n_slab, ne // pk, LANES), wdt),
                        pltpu.VMEM((n_slab, ne // pk, LANES), wdt), pltpu.VMEM((tt, d), BF16),
                        pltpu.VMEM((n_slab, nh * nk // pk, LANES), wdt),
                        pltpu.VMEM((n_slab, nh * nk // pk, LANES), wdt),
                        pltpu.VMEM((n_slab, nh * ic, LANES), F32), pltpu.VMEM((n_slab, nh * ic, LANES), F32)],
        compiler_params=_cparams("arbitrary", "arbitrary", "arbitrary"),
        name="peer_dense",
    )(hx, *route, u, v_t, x, gate)


def _peer(hx, x, gate, w_q, keys, u, v_t, tt=1024):
    tt = min(tt, hx.shape[1])
    route = _peer_route(hx, w_q, keys, tt)
    return _peer_dense(hx, route, u, v_t, x, gate, tt)


def _group_lanes(p, width):
    g = p.reshape(2, SSD_GROUPS, SSD_HPG).transpose(1, 0, 2).reshape(SSD_GROUPS, 2 * SSD_HPG)
    return jnp.pad(g, ((0, 0), (0, width - 2 * SSD_HPG))).reshape(SSD_GROUPS, 1, width)


def _block_diag(w, tile):
    two, nb, bd, _ = w.shape
    per = tile // bd
    w = w.reshape(two, nb // per, per, bd, bd)
    eye = jnp.eye(per, dtype=w.dtype)
    return jnp.einsum("dtpij,pq->dtpiqj", w, eye).reshape(two, nb // per, tile, tile)


def kernel(x, c, ctx, c_ctx, ada_w, ada_b, norm1_g, norm2_g, ev_w_in, ev_conv_w, ev_conv_b, ev_a_log,
           ev_dt_bias, ev_d, ev_ssd_norm_g, ev_lru_conv_w, ev_lru_conv_b, ev_lru_wa, ev_lru_ba, ev_lru_wx,
           ev_lru_bx, ev_lru_lam, ev_w_out, od_w_qkv, od_q_norm_g, od_k_norm_g, od_rpb, od_w_o,
           pe_w_q, pe_keys, pe_u, pe_v):
    bsz, sx, d = x.shape
    sc = ctx.shape[1]
    depth = ada_w.shape[0]

    n_c = bsz + 1
    rows = -(-n_c // SUBLANES) * SUBLANES
    c_all = jnp.concatenate([c, c_ctx[None], jnp.zeros((rows - n_c, d), F32)], axis=0)
    mods = _ada_mods(c_all, ada_w, ada_b).reshape(depth, rows, 6, d)

    ctx = ctx.reshape(1, bsz * sc, d)

    def per_batch(t):
        return t.reshape(bsz, sc, t.shape[-1])

    for layer in range(depth):
        last = layer == depth - 1
        j = layer // 2
        mod_x = mods[layer, :bsz]
        mod_c = mods[layer, bsz:bsz + 1]
        g1 = norm1_g[layer][None]
        g2 = norm2_g[layer][None]
        want_ctx = not last

        if layer % 2 == 0:
            w_in = ev_w_in[j]
            o_dt, o_xl = SSD_XBC, SSD_XBC + 2 * SSD_HEADS
            o_z = o_xl + LRU_WIDTH
            o_gate = o_z + SSD_INNER
            w_dt = w_in[:, o_dt:o_xl].reshape(d, 2, SSD_GROUPS, SSD_HPG).transpose(0, 2, 1, 3)
            w_dt = jnp.pad(w_dt.reshape(d, SSD_GROUPS, 2 * SSD_HPG), ((0, 0), (0, 0), (0, LANES - 2 * SSD_HPG)))
            ws = [w_in[:, :o_dt].astype(BF16), w_dt.reshape(d, SSD_GROUPS * LANES).astype(BF16),
                  w_in[:, o_xl:o_z].astype(BF16), w_in[:, o_z:o_gate].astype(BF16), w_in[:, o_gate:].astype(BF16)]
            dts = [BF16, F32, BF16, BF16, BF16]
            px = _nm_linear(x, g1, mod_x[:, 0:2], ws, dts)
            pc = [per_batch(t) for t in _nm_linear(ctx, g1, mod_c[:, 0:2], ws, dts)]
            y_ssd_x, y_ssd_c = _ssd_mixer(
                px[0], pc[0], px[1], pc[1], px[3], pc[3], ev_conv_w[j], ev_conv_b[j][None],
                _group_lanes(ev_a_log[j], LANES), _group_lanes(ev_dt_bias[j], LANES),
                jnp.repeat(ev_d[j], SSD_HEAD_DIM).reshape(SSD_GROUPS, 1, SSD_GW),
                ev_ssd_norm_g[j].reshape(SSD_GROUPS, 1, SSD_GW))
            y_lru_x, y_lru_c = _lru_mixer(
                px[2], pc[2], px[4], pc[4], ev_lru_conv_w[j], ev_lru_conv_b[j][None],
                _block_diag(ev_lru_wa[j], LRU_TILE).astype(BF16), _block_diag(ev_lru_wx[j], LRU_TILE).astype(BF16),
                ev_lru_ba[j], ev_lru_bx[j], ev_lru_lam[j])
            w_out = ev_w_out[j].astype(BF16)
            w_outs = [w_out[:SSD_INNER], w_out[SSD_INNER:]]
            acts_x = [y_ssd_x, y_lru_x]
            acts_c = [y_ssd_c.reshape(1, bsz * sc, -1), y_lru_c.reshape(1, bsz * sc, -1)]
        else:
            w_qkv = od_w_qkv[j].astype(BF16)
            nd = w_qkv.shape[1] // 3
            ws = [w_qkv[:, :nd], w_qkv[:, nd:2 * nd], w_qkv[:, 2 * nd:]]
            gains = [jnp.tile(od_q_norm_g[j], NA_HEADS)[None], jnp.tile(od_k_norm_g[j], NA_HEADS)[None], None]
            q_x, k_x, v_x = _nm_linear(x, g1, mod_x[:, 0:2], ws, [BF16] * 3, gains)
            q_c, k_c, v_c = [per_batch(t) for t in _nm_linear(ctx, g1, mod_c[:, 0:2], ws, [BF16] * 3, gains)]
            table = _rpb_table(od_rpb[j])
            o_x, o_c = _na_attention(q_x, k_x, v_x, q_c, k_c, v_c, table, want_ctx)
            w_outs = [od_w_o[j].astype(BF16)]
            acts_x = [o_x]
            acts_c = [o_c.reshape(1, bsz * sc, -1)] if want_ctx else None

        w_q = pe_w_q[layer].reshape(d, PEER_HEADS, 2 * PEER_KEYS).transpose(1, 0, 2).astype(BF16)
        keys = pe_keys[layer].reshape(2 * PEER_HEADS, PEER_KEYS, -1).astype(BF16)
        u = _pack_weight(pe_u, layer, transpose=False)
        v_t = _pack_weight(pe_v, layer, transpose=True)

        x, hx = _out_linear(acts_x, w_outs, x, mod_x[:, 2:5], g2)
        x = _peer(hx, x, mod_x[:, 5:6], w_q, keys, u, v_t)
        if want_ctx:
            ctx, hc = _out_linear(acts_c, w_outs, ctx, mod_c[:, 2:5], g2)
            ctx = _peer(hc, ctx, mod_c[:, 5:6], w_q, keys, u, v_t)
    return x
```

```python
import functools
import math

import jax
import jax.numpy as jnp
from jax import lax
from jax.experimental import pallas as pl
from jax.experimental.pallas import tpu as pltpu

F32 = jnp.float32
BF16 = jnp.bfloat16
HIGHEST = lax.Precision.HIGHEST

EPS = 1e-6
NEG_INF = -1e30

GRID_W = 64
SSD_HEADS = 16
SSD_HEAD_DIM = 64
SSD_GROUPS = 4
SSD_HPG = SSD_HEADS // SSD_GROUPS
SSD_STATE = 128
SSD_CHUNK = 128
SSD_INNER = SSD_HEADS * SSD_HEAD_DIM
SSD_GW = SSD_INNER // SSD_GROUPS
SSD_XBC = SSD_INNER + 2 * SSD_GROUPS * SSD_STATE
LRU_WIDTH = 1024
LRU_BLOCKS = 16
LRU_BLOCK_DIM = LRU_WIDTH // LRU_BLOCKS
LRU_C = 8.0
LRU_TILE = 256
NA_HEADS = 16
NA_HEAD_DIM = 64
NA_KH = 8
NA_KW = 16
NA_QROWS = 4
NA_KROWS = NA_KH + NA_QROWS
PEER_HEADS = 8
PEER_KEYS = 128
PEER_TOPK = 16
PEER_ICHUNK = 8

LANES = 128
SUBLANES = 8
VMEM_LIMIT_BYTES = 56 * 1024 * 1024


def _cparams(*sem):
    return pltpu.CompilerParams(dimension_semantics=sem, vmem_limit_bytes=VMEM_LIMIT_BYTES)


def _silu(x):
    return x * (1.0 / (1.0 + jnp.exp(-x)))


def _sigmoid(x):
    return 1.0 / (1.0 + jnp.exp(-x))


def _softplus(x):
    return jnp.maximum(x, 0.0) + jnp.log(1.0 + jnp.exp(-jnp.abs(x)))


def _gelu_tanh(x):
    k0 = -2.0 * math.sqrt(2.0 / math.pi)
    return x / (1.0 + jnp.exp(x * (k0 + (k0 * 0.044715) * (x * x))))


def _rms_mod(x, g, shift, scale):
    ms = jnp.mean(x * x, axis=-1, keepdims=True)
    y = x * lax.rsqrt(ms + EPS) * g
    return y * (1.0 + scale) + shift


def _dot(a, b):
    return jnp.dot(a, b, preferred_element_type=F32)


def _dot_nt(a, b):
    return lax.dot_general(a, b, (((1,), (1,)), ((), ())), preferred_element_type=F32)


def _ada_kernel(c_ref, w_ref, b_ref, o_ref):
    s = _silu(c_ref[...])
    o_ref[0] = jnp.dot(s, w_ref[0], preferred_element_type=F32, precision=HIGHEST) + b_ref[0]


def _ada_mods(c_all, ada_w, ada_b):
    depth, d, n = ada_w.shape
    rows = c_all.shape[0]
    tn = 1536
    return pl.pallas_call(
        _ada_kernel,
        grid=(depth, n // tn),
        in_specs=[pl.BlockSpec((rows, d), lambda l, j: (0, 0)),
                  pl.BlockSpec((1, d, tn), lambda l, j: (l, 0, j)),
                  pl.BlockSpec((1, 1, tn), lambda l, j: (l, 0, j))],
        out_specs=pl.BlockSpec((1, rows, tn), lambda l, j: (l, 0, j)),
        out_shape=jax.ShapeDtypeStruct((depth, rows, n), F32),
        compiler_params=_cparams("arbitrary", "arbitrary"),
        name="ada_mods",
    )(c_all, ada_w, ada_b.reshape(depth, 1, n))


def _head_block_ones(n):
    r = lax.broadcasted_iota(jnp.int32, (n, n), 0) // NA_HEAD_DIM
    c = lax.broadcasted_iota(jnp.int32, (n, n), 1) // NA_HEAD_DIM
    return (r == c).astype(F32)


def _nm_linear_kernel(*refs, n_out, head_norm, tn):
    x_ref, g_ref, mod_ref = refs[:3]
    w_refs = refs[3:3 + n_out]
    hg_refs = refs[3 + n_out:3 + n_out + sum(head_norm)]
    o_refs = refs[3 + n_out + sum(head_norm):]
    h = _rms_mod(x_ref[0], g_ref[...], mod_ref[0, 0:1, :], mod_ref[0, 1:2, :]).astype(BF16)
    hg_i = 0
    for w_ref, o_ref, hn in zip(w_refs, o_refs, head_norm):
        n = w_ref.shape[1]
        for j in range(n // tn):
            y = _dot(h, w_ref[:, j * tn:(j + 1) * tn])
            if hn:
                ss = jnp.dot(y * y, _head_block_ones(tn), preferred_element_type=F32, precision=HIGHEST)
                y = y * lax.rsqrt(ss * (1.0 / NA_HEAD_DIM) + EPS) * hg_refs[hg_i][:, j * tn:(j + 1) * tn]
            o_ref[0, :, j * tn:(j + 1) * tn] = y.astype(o_ref.dtype)
        hg_i += hn


def _nm_linear(x, g, mod, ws, out_dtypes, head_gains=None, tm=512, tn=256):
    bn, sn, d = x.shape
    tm = min(tm, sn)
    n_out = len(ws)
    head_gains = head_gains or [None] * n_out
    head_norm = tuple(hg is not None for hg in head_gains)
    hgs = [hg for hg in head_gains if hg is not None]
    in_specs = [pl.BlockSpec((1, tm, d), lambda b, i: (b, i, 0)),
                pl.BlockSpec((1, d), lambda b, i: (0, 0)),
                pl.BlockSpec((1, 2, d), lambda b, i: (b, 0, 0))]
    in_specs += [pl.BlockSpec(w.shape, lambda b, i: (0, 0)) for w in ws]
    in_specs += [pl.BlockSpec(hg.shape, lambda b, i: (0, 0)) for hg in hgs]
    out_specs = [pl.BlockSpec((1, tm, w.shape[1]), lambda b, i: (b, i, 0)) for w in ws]
    out_shape = [jax.ShapeDtypeStruct((bn, sn, w.shape[1]), dt) for w, dt in zip(ws, out_dtypes)]
    return pl.pallas_call(
        functools.partial(_nm_linear_kernel, n_out=n_out, head_norm=head_norm, tn=tn),
        grid=(bn, sn // tm),
        in_specs=in_specs, out_specs=out_specs, out_shape=out_shape,
        compiler_params=_cparams("arbitrary", "arbitrary"),
        name="nm_linear",
    )(x, g, mod, *ws, *hgs)


def _out_linear_kernel(*refs, n_in):
    a_refs = refs[:n_in]
    w_refs = refs[n_in:2 * n_in]
    x_ref, mod_ref, g_ref, xo_ref, ho_ref = refs[2 * n_in:]
    y = _dot(a_refs[0][0], w_refs[0][...])
    for a_ref, w_ref in zip(a_refs[1:], w_refs[1:]):
        y = y + _dot(a_ref[0], w_ref[...])
    xn = x_ref[0] + mod_ref[0, 0:1, :] * y
    xo_ref[0] = xn
    ho_ref[0] = _rms_mod(xn, g_ref[...], mod_ref[0, 1:2, :], mod_ref[0, 2:3, :]).astype(BF16)


def _out_linear(acts, ws, x, mod, g, tm=512):
    bn, sn, d = x.shape
    tm = min(tm, sn)
    n_in = len(acts)
    in_specs = [pl.BlockSpec((1, tm, a.shape[2]), lambda b, i: (b, i, 0)) for a in acts]
    in_specs += [pl.BlockSpec(w.shape, lambda b, i: (0, 0)) for w in ws]
    in_specs += [pl.BlockSpec((1, tm, d), lambda b, i: (b, i, 0)),
                 pl.BlockSpec((1, 3, d), lambda b, i: (b, 0, 0)),
                 pl.BlockSpec((1, d), lambda b, i: (0, 0))]
    return pl.pallas_call(
        functools.partial(_out_linear_kernel, n_in=n_in),
        grid=(bn, sn // tm),
        in_specs=in_specs,
        out_specs=[pl.BlockSpec((1, tm, d), lambda b, i: (b, i, 0))] * 2,
        out_shape=[jax.ShapeDtypeStruct((bn, sn, d), F32), jax.ShapeDtypeStruct((bn, sn, d), BF16)],
        compiler_params=_cparams("arbitrary", "arbitrary"),
        name="out_linear",
    )(*acts, *ws, x, mod, g)


CONV_HALO = 16


def _conv_chunk(src_ref, s, seg_len, w, bias, rows=SSD_CHUNK):
    ncol = src_ref.shape[2]
    if s > 0:
        prev = src_ref[0, s - CONV_HALO:s, :].astype(F32)
    else:
        prev = jnp.zeros((CONV_HALO, ncol), F32)
    cur = src_ref[0, s:s + rows, :].astype(F32)
    if s + rows < seg_len:
        nxt = src_ref[0, s + rows:s + rows + CONV_HALO, :].astype(F32)
    else:
        nxt = jnp.zeros((CONV_HALO, ncol), F32)
    win = jnp.concatenate([prev, cur, nxt], axis=0)
    taps = w.shape[0]
    acc = bias
    for k in range(taps):
        off = CONV_HALO - taps // 2 + k
        acc = acc + win[off:off + rows, :] * w[k:k + 1, :]
    return acc


def _lane_head_expand(cols, width):
    nh = len(cols)
    hd = width // nh
    rows = cols[0].shape[0]
    lane_head = lax.broadcasted_iota(jnp.int32, (rows, width), 1) // hd
    out = jnp.broadcast_to(cols[nh - 1], (rows, width))
    for h in range(nh - 2, -1, -1):
        out = jnp.where(lane_head == h, jnp.broadcast_to(cols[h], (rows, width)), out)
    return out


def _ssd_kernel(xs_x, bm_x, cm_x, xs_c, bm_c, cm_c, dt_x, dt_c, z_x, z_c,
                cw_xs, cw_b, cw_c, cb_xs, cb_b, cb_c, alog_ref, dtb_ref, dsk_ref, ng_ref,
                y_x, y_c,
                xs_s, bm_s, cm_s, dt_s, y_s, st_s, cs_s, cst_s, *, sx, sc):
    q = SSD_CHUNK
    nc_c, nc_x = sc // q, sx // q
    nc = nc_c + nc_x
    gw = xs_s.shape[1]

    for seg_ref3, seg_len, base in (((xs_c, bm_c, cm_c), sc, 0), ((xs_x, bm_x, cm_x), sx, sc)):
        for ci in range(seg_len // q):
            s = ci * q
            for src, dst, w_ref, b_ref in zip(seg_ref3, (xs_s, bm_s, cm_s), (cw_xs, cw_b, cw_c),
                                              (cb_xs, cb_b, cb_c)):
                dst[base + s:base + s + q, :] = _silu(_conv_chunk(src, s, seg_len, w_ref[...], b_ref[...]))
    dt_s[0:sc, :] = _softplus(dt_c[0] + dtb_ref[0])
    dt_s[sc:sc + sx, :] = _softplus(dt_x[0] + dtb_ref[0])

    a_neg = -jnp.exp(alog_ref[0])
    row = lax.broadcasted_iota(jnp.int32, (q, q), 0)
    col = lax.broadcasted_iota(jnp.int32, (q, q), 1)
    tri = ((col <= row).astype(F32), (col >= row).astype(F32))
    keep = (col <= row, col >= row)
    lane_head = lax.broadcasted_iota(jnp.int32, (q, gw), 1) // SSD_HEAD_DIM

    st_s[...] = jnp.zeros_like(st_s)
    y_s[...] = jnp.zeros_like(y_s)

    for ci in range(nc):
        la = dt_s[ci * q:(ci + 1) * q, :] * a_neg
        for d in range(2):
            cs = jnp.dot(tri[d], la, preferred_element_type=F32, precision=HIGHEST)
            cs_s[d, ci * q:(ci + 1) * q, :] = cs
            cst_s[d, ci * q:(ci + 1) * q, :] = cs.T

    def chunk_body(i, carry):
        for d in range(2):
            if d == 0:
                ci = i
            else:
                ci = jnp.where(i < nc_c, nc_c - 1 - i, nc + nc_c - 1 - i)
            r0 = pl.multiple_of(ci * q, q)
            xs = xs_s[pl.ds(r0, q), :]
            bm = bm_s[pl.ds(r0, q), :]
            cm = cm_s[pl.ds(r0, q), :]
            dt = dt_s[pl.ds(r0, q), :]
            cs = cs_s[d, pl.ds(r0, q), :]
            cs_t = cst_s[d, pl.ds(r0, q), :]
            cb = _dot_nt(cm.astype(BF16), bm.astype(BF16))
            heads = [d * SSD_HPG + h for h in range(SSD_HPG)]
            dt_mat = _lane_head_expand([dt[:, c:c + 1] for c in heads], gw)
            cs_mat = _lane_head_expand([cs[:, c:c + 1] for c in heads], gw)
            xd = xs * dt_mat
            xd_b = xd.astype(BF16)
            y = jnp.zeros((q, gw), F32)
            for h, c in enumerate(heads):
                diff = cs[:, c:c + 1] - cs_t[c:c + 1, :]
                lmat = jnp.exp(jnp.where(keep[d], diff, NEG_INF))
                y = jnp.where(lane_head == h, _dot((cb * lmat).astype(BF16), xd_b), y)
            st = st_s[d]
            y = y + _dot(cm.astype(BF16), st.astype(BF16)) * jnp.exp(cs_mat)
            end = q - 1 if d == 0 else 0
            cs_end = cs_mat[end:end + 1, :]
            s_new = _dot(bm.T.astype(BF16), (xd * jnp.exp(cs_end - cs_mat)).astype(BF16))
            st_s[d] = st * jnp.exp(cs_end) + s_new
            y_s[pl.ds(r0, q), :] = y_s[pl.ds(r0, q), :] + y
        return carry

    lax.fori_loop(0, nc, chunk_body, 0)

    for ci in range(nc):
        s = ci * q
        if ci < nc_c:
            z = z_c[0, s:s + q, :]
        else:
            z = z_x[0, s - sc:s - sc + q, :]
        y = (y_s[s:s + q, :] + dsk_ref[0] * xs_s[s:s + q, :]) * _silu(z.astype(F32))
        ms = jnp.mean(y * y, axis=-1, keepdims=True)
        out = (y * lax.rsqrt(ms + EPS) * ng_ref[0]).astype(BF16)
        if ci < nc_c:
            y_c[0, s:s + q, :] = out
        else:
            y_x[0, s - sc:s - sc + q, :] = out


def _ssd_mixer(xbc_x, xbc_c, dt_x, dt_c, z_x, z_c, conv_w, conv_b, alog_g, dtb_g, dsk_g, ng_g):
    bsz, sx, _ = xbc_x.shape
    sc = xbc_c.shape[1]
    g, gw, n = SSD_GROUPS, SSD_GW, SSD_STATE
    nb = SSD_INNER // n
    taps = conv_w.shape[0]

    def seq(s, w, off):
        return pl.BlockSpec((1, s, w), lambda b, j, off=off: (b, 0, off + j))

    def par(r, w, off):
        return pl.BlockSpec((r, w), lambda b, j, off=off: (0, off + j))

    def grp(w):
        return pl.BlockSpec((1, 1, w), lambda b, j: (j, 0, 0))

    in_specs = [seq(sx, gw, 0), seq(sx, n, nb), seq(sx, n, nb + g),
                seq(sc, gw, 0), seq(sc, n, nb), seq(sc, n, nb + g),
                seq(sx, LANES, 0), seq(sc, LANES, 0), seq(sx, gw, 0), seq(sc, gw, 0),
                par(taps, gw, 0), par(taps, n, nb), par(taps, n, nb + g),
                par(1, gw, 0), par(1, n, nb), par(1, n, nb + g),
                grp(LANES), grp(LANES), grp(gw), grp(gw)]
    stot = sx + sc
    return pl.pallas_call(
        functools.partial(_ssd_kernel, sx=sx, sc=sc),
        grid=(bsz, g),
        in_specs=in_specs,
        out_specs=[seq(sx, gw, 0), seq(sc, gw, 0)],
        out_shape=[jax.ShapeDtypeStruct((bsz, sx, SSD_INNER), BF16),
                   jax.ShapeDtypeStruct((bsz, sc, SSD_INNER), BF16)],
        scratch_shapes=[pltpu.VMEM((stot, gw), F32), pltpu.VMEM((stot, n), F32), pltpu.VMEM((stot, n), F32),
                        pltpu.VMEM((stot, LANES), F32), pltpu.VMEM((stot, gw), F32), pltpu.VMEM((2, n, gw), F32),
                        pltpu.VMEM((2, stot, LANES), F32), pltpu.VMEM((2, stot, LANES), F32)],
        compiler_params=_cparams("arbitrary", "arbitrary"),
        name="ssd_mixer",
    )(xbc_x, xbc_x, xbc_x, xbc_c, xbc_c, xbc_c, dt_x, dt_c, z_x, z_c,
      conv_w, conv_w, conv_w, conv_b, conv_b, conv_b, alog_g, dtb_g, dsk_g, ng_g)


def _lru_kernel(xl_x, xl_c, gt_x, gt_c, cw, cb, wa, wx, ba, bx, lam, y_x, y_c,
                xr_s, a_s, b_s, y_s, *, sx, sc):
    q = SSD_CHUNK
    stot = sx + sc
    w = xr_s.shape[1]
    for src, seg_len, base in ((xl_c, sc, 0), (xl_x, sx, sc)):
        for ci in range(seg_len // q):
            s = ci * q
            xr_s[base + s:base + s + q, :] = _conv_chunk(src, s, seg_len, cw[...], cb[...])

    ng = stot // SUBLANES
    ng_c = sc // SUBLANES
    sub = lax.broadcasted_iota(jnp.int32, (SUBLANES, w), 0)
    rt = q
    for d in range(2):
        nsp = _softplus(-lam[d:d + 1, :])
        for ci in range(stot // rt):
            s = ci * rt
            xr = xr_s[s:s + rt, :]
            xb = xr.astype(BF16)
            r = _sigmoid(_dot(xb, wa[d, 0]) + ba[d:d + 1, :])
            ig = _sigmoid(_dot(xb, wx[d, 0]) + bx[d:d + 1, :])
            a = jnp.exp(-LRU_C * r * nsp)
            a_s[d, s:s + rt, :] = a
            b_s[d, s:s + rt, :] = jnp.sqrt(1.0 - a * a) * (ig * xr)

    y_s[...] = jnp.zeros_like(y_s)

    def group_body(k, carries):
        new = []
        for d in range(2):
            if d == 0:
                gi = k
            else:
                gi = jnp.where(k < ng_c, ng_c - 1 - k, ng + ng_c - 1 - k)
            r0 = pl.multiple_of(gi * SUBLANES, SUBLANES)
            a = a_s[d, pl.ds(r0, SUBLANES), :]
            b = b_s[d, pl.ds(r0, SUBLANES), :]
            for sh in (1, 2, 4):
                if d == 0:
                    valid = sub >= sh
                    a_sh = pltpu.roll(a, sh, axis=0)
                    b_sh = pltpu.roll(b, sh, axis=0)
                else:
                    valid = sub < SUBLANES - sh
                    a_sh = pltpu.roll(a, SUBLANES - sh, axis=0)
                    b_sh = pltpu.roll(b, SUBLANES - sh, axis=0)
                b = jnp.where(valid, a * b_sh + b, b)
                a = jnp.where(valid, a * a_sh, a)
            h = a * carries[d] + b
            y_s[pl.ds(r0, SUBLANES), :] = y_s[pl.ds(r0, SUBLANES), :] + h
            last = h[SUBLANES - 1:SUBLANES, :] if d == 0 else h[0:1, :]
            new.append(jnp.broadcast_to(last, (SUBLANES, w)))
        return tuple(new)

    zero = jnp.zeros((SUBLANES, w), F32)
    lax.fori_loop(0, ng, group_body, (zero, zero))

    for ci in range(stot // rt):
        s = ci * rt
        if s < sc:
            gate = gt_c[0, s:s + rt, :]
        else:
            gate = gt_x[0, s - sc:s - sc + rt, :]
        out = (y_s[s:s + rt, :] * _gelu_tanh(gate.astype(F32))).astype(BF16)
        if s < sc:
            y_c[0, s:s + rt, :] = out
        else:
            y_x[0, s - sc:s - sc + rt, :] = out


def _lru_mixer(xl_x, xl_c, gt_x, gt_c, conv_w, conv_b, wa_bd, wx_bd, ba, bx, lam):
    bsz, sx, width = xl_x.shape
    sc = xl_c.shape[1]
    w = LRU_TILE
    taps = conv_w.shape[0]

    def seq(s):
        return pl.BlockSpec((1, s, w), lambda b, j: (b, 0, j))

    def par(r):
        return pl.BlockSpec((r, w), lambda b, j: (0, j))

    wspec = pl.BlockSpec((2, 1, w, w), lambda b, j: (0, j, 0, 0))
    stot = sx + sc
    return pl.pallas_call(
        functools.partial(_lru_kernel, sx=sx, sc=sc),
        grid=(bsz, width // w),
        in_specs=[seq(sx), seq(sc), seq(sx), seq(sc), par(taps), par(1), wspec, wspec, par(2), par(2), par(2)],
        out_specs=[seq(sx), seq(sc)],
        out_shape=[jax.ShapeDtypeStruct((bsz, sx, width), BF16), jax.ShapeDtypeStruct((bsz, sc, width), BF16)],
        scratch_shapes=[pltpu.VMEM((stot, w), F32), pltpu.VMEM((2, stot, w), F32), pltpu.VMEM((2, stot, w), F32),
                        pltpu.VMEM((stot, w), F32)],
        compiler_params=_cparams("arbitrary", "arbitrary"),
        name="lru_mixer",
    )(xl_x, xl_c, gt_x, gt_c, conv_w, conv_b, wa_bd, wx_bd, ba, bx, lam)


def _rpb_table_kernel(rpb_ref, o_ref, *, n_dr, n_dc):
    h = pl.program_id(0)
    w = GRID_W
    qcol = lax.broadcasted_iota(jnp.int32, (w, 2 * w), 0)
    lane = lax.broadcasted_iota(jnp.int32, (w, 2 * w), 1)
    kcol = lane % w
    hi = lane >= w
    rel = kcol - qcol + (NA_KW - 1)
    cstart = jnp.clip(qcol - NA_KW // 2, 0, w - NA_KW)
    in_win = (kcol >= cstart) & (kcol < cstart + NA_KW)
    for d in range(n_dr + 1):
        acc = jnp.full((w, 2 * w), NEG_INF, F32)
        for dc in range(n_dc):
            lo = rpb_ref[(h * n_dr + d - 1) * n_dc + dc] if d >= 1 else NEG_INF
            up = rpb_ref[(h * n_dr + d) * n_dc + dc] if d < n_dr else NEG_INF
            acc = jnp.where(rel == dc, jnp.where(hi, up, lo), acc)
        valid = in_win
        if d == 0:
            valid = valid & hi
        if d == n_dr:
            valid = valid & jnp.logical_not(hi)
        o_ref[0, d] = jnp.where(valid, acc, NEG_INF)


def _rpb_table(rpb):
    nh, n_dr, n_dc = rpb.shape
    return pl.pallas_call(
        functools.partial(_rpb_table_kernel, n_dr=n_dr, n_dc=n_dc),
        grid=(nh,),
        in_specs=[pl.BlockSpec(memory_space=pltpu.SMEM)],
        out_specs=pl.BlockSpec((1, n_dr + 1, GRID_W, 2 * GRID_W), lambda h: (h, 0, 0, 0)),
        out_shape=jax.ShapeDtypeStruct((nh, n_dr + 1, GRID_W, 2 * GRID_W), F32),
        compiler_params=_cparams("arbitrary"),
        name="rpb_table",
    )(rpb.reshape(-1))


def _na_kernel(*refs, sx, sc, want_ctx):
    if want_ctx:
        q_x, k_x, v_x, q_c, k_c, v_c, tab, o_x, o_c = refs
    else:
        q_x, k_x, v_x, k_c, v_c, tab, o_x = refs
    w = GRID_W
    rows = sx // w
    qb = NA_QROWS * w
    kb = NA_KROWS * w
    n_blk = rows // NA_QROWS
    scale = NA_HEAD_DIM ** -0.5
    lane = lax.broadcasted_iota(jnp.int32, (1, 2 * NA_HEAD_DIM), 1)
    in_head = (lane < NA_HEAD_DIM, lane >= NA_HEAD_DIM)
    keyrow = lax.broadcasted_iota(jnp.int32, (1, kb), 1) // w
    kc = k_c[0]
    vc = v_c[0]
    n_tab = tab.shape[1]

    def softmax_pv(parts):
        m = parts[0][0].max(axis=-1, keepdims=True)
        for s, _ in parts[1:]:
            m = jnp.maximum(m, s.max(axis=-1, keepdims=True))
        acc, den = None, None
        for s, v in parts:
            e = jnp.exp(s - m)
            den = e.sum(axis=-1, keepdims=True) if den is None else den + e.sum(axis=-1, keepdims=True)
            pv = _dot(e.astype(BF16), v)
            acc = pv if acc is None else acc + pv
        return acc / den

    def block_body(rb, carry):
        ws = jnp.clip(NA_QROWS * rb - NA_KH // 2, 0, rows - NA_KROWS)
        q0 = pl.multiple_of(rb * qb, qb)
        k0 = pl.multiple_of(ws * w, w)
        qblk = q_x[0, pl.ds(q0, qb), :]
        kwin = k_x[0, pl.ds(k0, kb), :]
        vwin = v_x[0, pl.ds(k0, kb), :]
        out = jnp.zeros((qb, 2 * NA_HEAD_DIM), F32)
        for hh in range(2):
            qm = jnp.where(in_head[hh], qblk * scale, jnp.zeros_like(qblk))
            s_loc = _dot_nt(qm, kwin)
            s_ctx = _dot_nt(qm, kc)
            pieces = []
            for rq in range(NA_QROWS):
                r = NA_QROWS * rb + rq
                rs = jnp.clip(r - NA_KH // 2, 0, rows - NA_KH)
                lo = rs - ws
                valid = (keyrow >= lo) & (keyrow < lo + NA_KH)
                blocks = []
                for ip in range(NA_KROWS // 2):
                    dr_lo = ws + 2 * ip - r + NA_KH - 1
                    blocks.append(tab[hh, jnp.clip(dr_lo + 1, 0, n_tab - 1)])
                bias = jnp.concatenate(blocks, axis=1)
                piece = s_loc[rq * w:(rq + 1) * w, :] + bias
                pieces.append(jnp.where(valid, piece, NEG_INF))
            s_loc = jnp.concatenate(pieces, axis=0)
            o = softmax_pv([(s_loc, vwin), (s_ctx, vc)])
            out = jnp.where(in_head[hh], o, out)
        o_x[0, pl.ds(q0, qb), :] = out.astype(o_x.dtype)
        return carry

    lax.fori_loop(0, n_blk, block_body, 0)

    if want_ctx:
        qc = q_c[0]
        out = jnp.zeros((sc, 2 * NA_HEAD_DIM), F32)
        for hh in range(2):
            qm = jnp.where(in_head[hh], qc * scale, jnp.zeros_like(qc))
            o = softmax_pv([(_dot_nt(qm, kc), vc)])
            out = jnp.where(in_head[hh], o, out)
        o_c[0] = out.astype(o_c.dtype)


def _na_attention(q_x, k_x, v_x, q_c, k_c, v_c, table, want_ctx):
    bsz, sx, dim = q_x.shape
    sc = k_c.shape[1]
    pw = 2 * NA_HEAD_DIM
    n_pair = dim // pw

    def seq(s):
        return pl.BlockSpec((1, s, pw), lambda p, b: (b, 0, p))

    tspec = pl.BlockSpec((2,) + table.shape[1:], lambda p, b: (p, 0, 0, 0))
    if want_ctx:
        args = (q_x, k_x, v_x, q_c, k_c, v_c, table)
        in_specs = [seq(sx)] * 3 + [seq(sc)] * 3 + [tspec]
        out_specs = [seq(sx), seq(sc)]
        out_shape = [jax.ShapeDtypeStruct((bsz, sx, dim), BF16), jax.ShapeDtypeStruct((bsz, sc, dim), BF16)]
    else:
        args = (q_x, k_x, v_x, k_c, v_c, table)
        in_specs = [seq(sx)] * 3 + [seq(sc)] * 2 + [tspec]
        out_specs = [seq(sx)]
        out_shape = [jax.ShapeDtypeStruct((bsz, sx, dim), BF16)]
    res = pl.pallas_call(
        functools.partial(_na_kernel, sx=sx, sc=sc, want_ctx=want_ctx),
        grid=(n_pair, bsz),
        in_specs=in_specs, out_specs=out_specs, out_shape=out_shape,
        compiler_params=_cparams("arbitrary", "arbitrary"),
        name="na_attention",
    )(*args)
    return res if want_ctx else (res[0], None)


def _top_rows(s, k, exact, want_rank=True):
    n, tt = s.shape
    top_id = lax.broadcasted_iota(jnp.int32, (k, tt), 0)
    if exact:
        rowid = lax.broadcasted_iota(jnp.int32, (n, tt), 0).astype(F32)
    work = s
    top = jnp.zeros((k, tt), F32)
    rank = jnp.full((n, tt), float(k), F32) if want_rank else None
    for it in range(k):
        m = jnp.max(work, axis=0, keepdims=True)
        sel = work == m
        if exact:
            sel = rowid == jnp.min(jnp.where(sel, rowid, float(n)), axis=0, keepdims=True)
        top = jnp.where(top_id == it, m, top)
        if want_rank:
            rank = jnp.where(sel, float(it), rank)
        work = jnp.where(sel, -jnp.inf, work)
    picked = work == -jnp.inf
    n_sel = jnp.sum(jnp.where(picked, 1.0, 0.0), axis=0, keepdims=True)
    return top, rank, picked, n_sel


def _peer_route_kernel(h_ref, wq_ref, keys_ref, cnt_ref, rk_ref, e0_ref, e1_ref, s_s, top_s, rank_s, cnti_s, z_s):
    nk, k, ic = PEER_KEYS, PEER_TOPK, PEER_ICHUNK
    tt = h_ref.shape[1]
    sw = top_s.shape[2]
    n_strip = tt // sw
    n_unit = PEER_HEADS * n_strip

    def scores(u):
        h, si = u // n_strip, u % n_strip
        rows = pl.ds(pl.multiple_of(si * sw, sw), sw)
        q = _dot(h_ref[0, rows, :], wq_ref[h]).astype(BF16)
        for z in range(2):
            s_s[u % 2, z] = _dot_nt(keys_ref[2 * h + z], q[:, z * nk:(z + 1) * nk])

    def pair_stage(t0, t1, exact):
        cand = jnp.concatenate([t0[0:1] + t1] + [t0[a:a + 1] + t1[0:8] for a in range(1, 8)]
                               + [t0[8:16] + t1[0:1]], axis=0)
        _, _, picked, n_sel = _top_rows(cand, k, exact, want_rank=False)
        pf = jnp.where(picked, 1.0, 0.0)
        z_sum = jnp.sum(pf * jnp.exp(cand - cand[0:1]), axis=0, keepdims=True)
        cnts = [jnp.sum(pf[0:k], axis=0, keepdims=True)]
        cnts += [jnp.sum(pf[k + 8 * (a - 1):k + 8 * a], axis=0, keepdims=True) for a in range(1, 8)]
        cnts += [pf[k + 56 + a:k + 57 + a] for a in range(8)]
        return cnts, z_sum, n_sel

    def route(u, prefetch):
        h, si = u // n_strip, u % n_strip
        cols = pl.ds(pl.multiple_of(si * sw, sw), sw)
        s0, s1 = s_s[u % 2, 0], s_s[u % 2, 1]
        if prefetch:
            scores(u + 1)

        t0, _, _, n0 = _top_rows(s0, k, False, want_rank=False)
        t1, rank1, _, n1 = _top_rows(s1, k, False)
        cnts, z_sum, n2 = pair_stage(t0, t1, False)
        cnt_i = jnp.zeros_like(s0)
        for a in range(k):
            cnt_i = jnp.where(s0 == t0[a:a + 1], cnts[a], cnt_i)
        top_s[0], top_s[1] = t0, t1
        rank_s[...] = rank1
        cnti_s[...] = cnt_i
        z_s[...] = z_sum
        ties = jnp.max(jnp.abs(n0 - float(k)) + jnp.abs(n1 - float(k)) + jnp.abs(n2 - float(k)))

        @pl.when(ties > 0.5)
        def _():
            t0, rank0, _, _ = _top_rows(s0, k, True)
            t1, rank1, _, _ = _top_rows(s1, k, True)
            cnts, z_sum, _ = pair_stage(t0, t1, True)
            cnt_i = jnp.zeros_like(s0)
            for a in range(k):
                cnt_i = jnp.where(rank0 == float(a), cnts[a], cnt_i)
            top_s[0], top_s[1] = t0, t1
            rank_s[...] = rank1
            cnti_s[...] = cnt_i
            z_s[...] = z_sum

        cnt_i = cnti_s[...]
        e0 = jnp.exp(s0 - top_s[0, 0:1]) / z_s[...]
        hi = pl.ds(pl.multiple_of(h * ic, ic), ic)
        hj = pl.ds(pl.multiple_of(h * nk, nk), nk)
        for c in range(nk // ic):
            cnt_ref[0, c, hi, cols] = cnt_i[c * ic:(c + 1) * ic]
            e0_ref[0, c, hi, cols] = e0[c * ic:(c + 1) * ic]
        rk_ref[0, hj, cols] = rank_s[...].astype(rk_ref.dtype)
        e1_ref[0, hj, cols] = jnp.exp(s1 - top_s[1, 0:1]).astype(e1_ref.dtype)

    scores(jnp.int32(0))

    def unit(u, carry):
        route(u, True)
        return carry

    lax.fori_loop(0, n_unit - 1, unit, 0)
    route(jnp.int32(n_unit - 1), False)


ROUTE_STRIP = 256


def _peer_route(hx, w_q, keys, tt):
    bn, sn, d = hx.shape
    nk, nh, ic = PEER_KEYS, PEER_HEADS, PEER_ICHUNK
    spec_i = pl.BlockSpec((1, nk // ic, nh * ic, tt), lambda b, i: (b, 0, 0, i))
    spec_j = pl.BlockSpec((1, nh * nk, tt), lambda b, i: (b, 0, i))
    shape_i = jax.ShapeDtypeStruct((bn, nk // ic, nh * ic, sn), F32)
    shape_j = jax.ShapeDtypeStruct((bn, nh * nk, sn), BF16)
    sw = min(ROUTE_STRIP, tt)
    return pl.pallas_call(
        _peer_route_kernel,
        grid=(bn, sn // tt),
        in_specs=[pl.BlockSpec((1, tt, d), lambda b, i: (b, i, 0)),
                  pl.BlockSpec(w_q.shape, lambda b, i: (0, 0, 0)),
                  pl.BlockSpec(keys.shape, lambda b, i: (0, 0, 0))],
        out_specs=[spec_i, spec_j, spec_i, spec_j],
        out_shape=[shape_i, shape_j, shape_i, shape_j],
        scratch_shapes=[pltpu.VMEM((2, 2, nk, sw), F32), pltpu.VMEM((2, PEER_TOPK, sw), F32),
                        pltpu.VMEM((nk, sw), F32), pltpu.VMEM((nk, sw), F32), pltpu.VMEM((1, sw), F32)],
        compiler_params=_cparams("arbitrary", "arbitrary"),
        name="peer_route",
    )(hx, w_q, keys)


def _peer_dense_kernel(h_ref, cnt_ref, rk_ref, e0_ref, e1_ref, u_ref, vt_ref, x_ref, g_ref, o_ref,
                       acc_ref, act_ref, p_ref, hx_s, rk_s, e1_s, cnt_s, e0_s):
    nk, ic = PEER_KEYS, PEER_ICHUNK
    ck = pl.program_id(2)
    n_slab = act_ref.shape[0]
    sw = min(PEER_STRIP, n_slab * LANES)
    per = sw // LANES
    n_strip = n_slab // per
    il_group, j_group = 4, 2
    pk = _rows_per_word(BF16)

    @pl.when(ck == 0)
    def _():
        acc_ref[...] = jnp.zeros_like(acc_ref)
        hx_s[...] = h_ref[0]
        for t in range(n_slab):
            rk_s[t] = _to_words(rk_ref[0, :, t * LANES:(t + 1) * LANES])
            e1_s[t] = _to_words(e1_ref[0, :, t * LANES:(t + 1) * LANES])

    for t in range(n_slab):
        cnt_s[t] = cnt_ref[0, 0, :, t * LANES:(t + 1) * LANES]
        e0_s[t] = e0_ref[0, 0, :, t * LANES:(t + 1) * LANES]

    def activations(s):
        r0 = pl.multiple_of(s * sw, sw)
        a = _gelu_tanh(_dot_nt(_from_words(u_ref[...], BF16), hx_s[pl.ds(r0, sw), :])).astype(BF16)
        for k in range(per):
            act_ref[s * per + k] = _to_words(a[:, k * LANES:(k + 1) * LANES])

    def gate_weights(t, ig):
        jr = nk // j_group
        wgt = [[None] * il_group for _ in range(j_group)]
        for h in range(PEER_HEADS):
            cnt8 = cnt_s[t, h * ic:(h + 1) * ic, :]
            e08 = e0_s[t, h * ic:(h + 1) * ic, :]
            rows = []
            for g in range(il_group):
                il = ig * il_group + g
                rows.append((jnp.broadcast_to(cnt8[il:il + 1], (jr, LANES)).astype(BF16),
                             jnp.broadcast_to(e08[il:il + 1], (jr, LANES)).astype(BF16)))
            for jg in range(j_group):
                j0 = (h * nk + jg * jr) // pk
                rk = _from_words(rk_s[t, j0:j0 + jr // pk, :], BF16)
                e1 = _from_words(e1_s[t, j0:j0 + jr // pk, :], BF16)
                for g in range(il_group):
                    term = jnp.where(rk < rows[g][0], e1 * rows[g][1], jnp.zeros_like(e1))
                    wgt[jg][g] = term if wgt[jg][g] is None else wgt[jg][g] + term
        for jg in range(j_group):
            for g in range(il_group):
                r0 = ((ig * il_group + g) * nk + jg * jr) // pk
                act = _from_words(act_ref[t, r0:r0 + jr // pk, :], BF16)
                p_ref[t, r0:r0 + jr // pk, :] = _to_words(wgt[jg][g] * act)

    def combine(s):
        for k in range(per):
            for ig in range(ic // il_group):
                gate_weights(s * per + k, ig)
        p = _from_words(jnp.concatenate([p_ref[s * per + k] for k in range(per)], axis=1), BF16)
        y = _dot(_from_words(vt_ref[...], BF16), p)
        for k in range(per):
            acc_ref[s * per + k] += y[:, k * LANES:(k + 1) * LANES]

    activations(0)

    def strip(s, carry):
        activations(s + 1)
        combine(s)
        return carry

    lax.fori_loop(0, n_strip - 1, strip, 0)
    combine(n_strip - 1)

    @pl.when(ck == pl.num_programs(2) - 1)
    def _():
        for t in range(n_slab):
            rows = slice(t * LANES, (t + 1) * LANES)
            o_ref[0, rows, :] = x_ref[0, rows, :] + g_ref[0] * acc_ref[t].T


PEER_STRIP = 256


def _rows_per_word(dt):
    return 4 // jnp.dtype(dt).itemsize


def _to_words(x):
    return pltpu.bitcast(x, jnp.uint32) if x.dtype.itemsize == 2 else x


def _from_words(x, dt):
    return pltpu.bitcast(x, dt) if jnp.dtype(dt).itemsize == 2 else x


def _pack_weight_kernel(w_ref, o_ref, *, transpose):
    w = w_ref[0].T if transpose else w_ref[0]
    o_ref[...] = _to_words(w.astype(BF16))


def _pack_weight(w, layer, transpose):
    _, rows, cols = w.shape
    pk = _rows_per_word(BF16)
    blk = 1024
    if transpose:
        out_shape, out_spec = (cols // pk, rows), pl.BlockSpec((cols // pk, blk), lambda i: (0, i))
    else:
        out_shape, out_spec = (rows // pk, cols), pl.BlockSpec((blk // pk, cols), lambda i: (i, 0))
    return pl.pallas_call(
        functools.partial(_pack_weight_kernel, transpose=transpose),
        grid=(rows // blk,),
        in_specs=[pl.BlockSpec((1, blk, cols), lambda i: (layer, i, 0))],
        out_specs=out_spec,
        out_shape=jax.ShapeDtypeStruct(out_shape, jnp.uint32 if pk == 2 else BF16),
        compiler_params=_cparams("arbitrary"),
        name="pack_weight",
    )(w)


def _peer_dense(hx, route, u, v_t, x, gate, tt):
    bn, sn, d = hx.shape
    nk, nh, ic = PEER_KEYS, PEER_HEADS, PEER_ICHUNK
    ne = ic * nk
    pk = _rows_per_word(BF16)
    wdt = jnp.uint32 if pk == 2 else BF16
    n_chunk = u.shape[0] * pk // ne
    spec_i = pl.BlockSpec((1, 1, nh * ic, tt), lambda b, i, c: (b, c, 0, i))
    spec_j = pl.BlockSpec((1, nh * nk, tt), lambda b, i, c: (b, 0, i))
    tok = pl.BlockSpec((1, tt, d), lambda b, i, c: (b, i, 0))
    n_slab = tt // LANES
    return pl.pallas_call(
        _peer_dense_kernel,
        grid=(bn, sn // tt, n_chunk),
        in_specs=[tok, spec_i, spec_j, spec_i, spec_j,
                  pl.BlockSpec((ne // pk, d), lambda b, i, c: (c, 0)),
                  pl.BlockSpec((d // pk, ne), lambda b, i, c: (0, c)),
                  pl.BlockSpec((1, tt, d), lambda b, i, c: (b, i, 0), pipeline_mode=pl.Buffered(1)),
                  pl.BlockSpec((1, 1, d), lambda b, i, c: (b, 0, 0))],
        out_specs=tok,
        out_shape=jax.ShapeDtypeStruct((bn, sn, d), F32),
        scratch_shapes=[pltpu.VMEM((n_slab, d, LANES), F32), pltpu.VMEM((n_slab, ne // pk, LANES), wdt),
                        pltpu.VMEM((n_slab, ne // pk, LANES), wdt), pltpu.VMEM((tt, d), BF16),
                        pltpu.VMEM((n_slab, nh * nk // pk, LANES), wdt),
                        pltpu.VMEM((n_slab, nh * nk // pk, LANES), wdt),
                        pltpu.VMEM((n_slab, nh * ic, LANES), F32), pltpu.VMEM((n_slab, nh * ic, LANES), F32)],
        compiler_params=_cparams("arbitrary", "arbitrary", "arbitrary"),
        name="peer_dense",
    )(hx, *route, u, v_t, x, gate)


def _peer(hx, x, gate, w_q, keys, u, v_t, tt=1024):
    tt = min(tt, hx.shape[1])
    route = _peer_route(hx, w_q, keys, tt)
    return _peer_dense(hx, route, u, v_t, x, gate, tt)


def _group_lanes(p, width):
    g = p.reshape(2, SSD_GROUPS, SSD_HPG).transpose(1, 0, 2).reshape(SSD_GROUPS, 2 * SSD_HPG)
    return jnp.pad(g, ((0, 0), (0, width - 2 * SSD_HPG))).reshape(SSD_GROUPS, 1, width)


def _block_diag(w, tile):
    two, nb, bd, _ = w.shape
    per = tile // bd
    w = w.reshape(two, nb // per, per, bd, bd)
    eye = jnp.eye(per, dtype=w.dtype)
    return jnp.einsum("dtpij,pq->dtpiqj", w, eye).reshape(two, nb // per, tile, tile)


def kernel(x, c, ctx, c_ctx, ada_w, ada_b, norm1_g, norm2_g, ev_w_in, ev_conv_w, ev_conv_b, ev_a_log,
           ev_dt_bias, ev_d, ev_ssd_norm_g, ev_lru_conv_w, ev_lru_conv_b, ev_lru_wa, ev_lru_ba, ev_lru_wx,
           ev_lru_bx, ev_lru_lam, ev_w_out, od_w_qkv, od_q_norm_g, od_k_norm_g, od_rpb, od_w_o,
           pe_w_q, pe_keys, pe_u, pe_v):
    bsz, sx, d = x.shape
    sc = ctx.shape[1]
    depth = ada_w.shape[0]

    n_c = bsz + 1
    rows = -(-n_c // SUBLANES) * SUBLANES
    c_all = jnp.concatenate([c, c_ctx[None], jnp.zeros((rows - n_c, d), F32)], axis=0)
    mods = _ada_mods(c_all, ada_w, ada_b).reshape(depth, rows, 6, d)

    ctx = ctx.reshape(1, bsz * sc, d)

    def per_batch(t):
        return t.reshape(bsz, sc, t.shape[-1])

    for layer in range(depth):
        last = layer == depth - 1
        j = layer // 2
        mod_x = mods[layer, :bsz]
        mod_c = mods[layer, bsz:bsz + 1]
        g1 = norm1_g[layer][None]
        g2 = norm2_g[layer][None]
        want_ctx = not last

        if layer % 2 == 0:
            w_in = ev_w_in[j]
            o_dt, o_xl = SSD_XBC, SSD_XBC + 2 * SSD_HEADS
            o_z = o_xl + LRU_WIDTH
            o_gate = o_z + SSD_INNER
            w_dt = w_in[:, o_dt:o_xl].reshape(d, 2, SSD_GROUPS, SSD_HPG).transpose(0, 2, 1, 3)
            w_dt = jnp.pad(w_dt.reshape(d, SSD_GROUPS, 2 * SSD_HPG), ((0, 0), (0, 0), (0, LANES - 2 * SSD_HPG)))
            ws = [w_in[:, :o_dt].astype(BF16), w_dt.reshape(d, SSD_GROUPS * LANES).astype(BF16),
                  w_in[:, o_xl:o_z].astype(BF16), w_in[:, o_z:o_gate].astype(BF16), w_in[:, o_gate:].astype(BF16)]
            dts = [BF16, F32, BF16, BF16, BF16]
            px = _nm_linear(x, g1, mod_x[:, 0:2], ws, dts)
            pc = [per_batch(t) for t in _nm_linear(ctx, g1, mod_c[:, 0:2], ws, dts)]
            y_ssd_x, y_ssd_c = _ssd_mixer(
                px[0], pc[0], px[1], pc[1], px[3], pc[3], ev_conv_w[j], ev_conv_b[j][None],
                _group_lanes(ev_a_log[j], LANES), _group_lanes(ev_dt_bias[j], LANES),
                jnp.repeat(ev_d[j], SSD_HEAD_DIM).reshape(SSD_GROUPS, 1, SSD_GW),
                ev_ssd_norm_g[j].reshape(SSD_GROUPS, 1, SSD_GW))
            y_lru_x, y_lru_c = _lru_mixer(
                px[2], pc[2], px[4], pc[4], ev_lru_conv_w[j], ev_lru_conv_b[j][None],
                _block_diag(ev_lru_wa[j], LRU_TILE).astype(BF16), _block_diag(ev_lru_wx[j], LRU_TILE).astype(BF16),
                ev_lru_ba[j], ev_lru_bx[j], ev_lru_lam[j])
            w_out = ev_w_out[j].astype(BF16)
            w_outs = [w_out[:SSD_INNER], w_out[SSD_INNER:]]
            acts_x = [y_ssd_x, y_lru_x]
            acts_c = [y_ssd_c.reshape(1, bsz * sc, -1), y_lru_c.reshape(1, bsz * sc, -1)]
        else:
            w_qkv = od_w_qkv[j].astype(BF16)
            nd = w_qkv.shape[1] // 3
            ws = [w_qkv[:, :nd], w_qkv[:, nd:2 * nd], w_qkv[:, 2 * nd:]]
            gains = [jnp.tile(od_q_norm_g[j], NA_HEADS)[None], jnp.tile(od_k_norm_g[j], NA_HEADS)[None], None]
            q_x, k_x, v_x = _nm_linear(x, g1, mod_x[:, 0:2], ws, [BF16] * 3, gains)
            q_c, k_c, v_c = [per_batch(t) for t in _nm_linear(ctx, g1, mod_c[:, 0:2], ws, [BF16] * 3, gains)]
            table = _rpb_table(od_rpb[j])
            o_x, o_c = _na_attention(q_x, k_x, v_x, q_c, k_c, v_c, table, want_ctx)
            w_outs = [od_w_o[j].astype(BF16)]
            acts_x = [o_x]
            acts_c = [o_c.reshape(1, bsz * sc, -1)] if want_ctx else None

        w_q = pe_w_q[layer].reshape(d, PEER_HEADS, 2 * PEER_KEYS).transpose(1, 0, 2).astype(BF16)
        keys = pe_keys[layer].reshape(2 * PEER_HEADS, PEER_KEYS, -1).astype(BF16)
        u = _pack_weight(pe_u, layer, transpose=False)
        v_t = _pack_weight(pe_v, layer, transpose=True)

        x, hx = _out_linear(acts_x, w_outs, x, mod_x[:, 2:5], g2)
        x = _peer(hx, x, mod_x[:, 5:6], w_q, keys, u, v_t)
        if want_ctx:
            ctx, hc = _out_linear(acts_c, w_outs, ctx, mod_c[:, 2:5], g2)
            ctx = _peer(hc, ctx, mod_c[:, 5:6], w_q, keys, u, v_t)
    return x
```

```python
import functools
import math

import jax
import jax.numpy as jnp
from jax import lax
from jax.experimental import pallas as pl
from jax.experimental.pallas import tpu as pltpu

F32 = jnp.float32
BF16 = jnp.bfloat16
HIGHEST = lax.Precision.HIGHEST

EPS = 1e-6
NEG_INF = -1e30

GRID_W = 64
SSD_HEADS = 16
SSD_HEAD_DIM = 64
SSD_GROUPS = 4
SSD_HPG = SSD_HEADS // SSD_GROUPS
SSD_STATE = 128
SSD_CHUNK = 128
SSD_INNER = SSD_HEADS * SSD_HEAD_DIM
SSD_GW = SSD_INNER // SSD_GROUPS
SSD_XBC = SSD_INNER + 2 * SSD_GROUPS * SSD_STATE
LRU_WIDTH = 1024
LRU_BLOCKS = 16
LRU_BLOCK_DIM = LRU_WIDTH // LRU_BLOCKS
LRU_C = 8.0
LRU_TILE = 256
NA_HEADS = 16
NA_HEAD_DIM = 64
NA_KH = 8
NA_KW = 16
NA_QROWS = 4
NA_KROWS = NA_KH + NA_QROWS
PEER_HEADS = 8
PEER_KEYS = 128
PEER_TOPK = 16
PEER_ICHUNK = 8

LANES = 128
SUBLANES = 8
VMEM_LIMIT_BYTES = 56 * 1024 * 1024


def _cparams(*sem):
    return pltpu.CompilerParams(dimension_semantics=sem, vmem_limit_bytes=VMEM_LIMIT_BYTES)


def _silu(x):
    return x * (1.0 / (1.0 + jnp.exp(-x)))


def _sigmoid(x):
    return 1.0 / (1.0 + jnp.exp(-x))


def _softplus(x):
    return jnp.maximum(x, 0.0) + jnp.log(1.0 + jnp.exp(-jnp.abs(x)))


def _gelu_tanh(x):
    k0 = -2.0 * math.sqrt(2.0 / math.pi)
    return x / (1.0 + jnp.exp(x * (k0 + (k0 * 0.044715) * (x * x))))


def _rms_mod(x, g, shift, scale):
    ms = jnp.mean(x * x, axis=-1, keepdims=True)
    y = x * lax.rsqrt(ms + EPS) * g
    return y * (1.0 + scale) + shift


def _dot(a, b):
    return jnp.dot(a, b, preferred_element_type=F32)


def _dot_nt(a, b):
    return lax.dot_general(a, b, (((1,), (1,)), ((), ())), preferred_element_type=F32)


def _ada_kernel(c_ref, w_ref, b_ref, o_ref):
    s = _silu(c_ref[...])
    o_ref[0] = jnp.dot(s, w_ref[0], preferred_element_type=F32, precision=HIGHEST) + b_ref[0]


def _ada_mods(c_all, ada_w, ada_b):
    depth, d, n = ada_w.shape
    rows = c_all.shape[0]
    tn = 1536
    return pl.pallas_call(
        _ada_kernel,
        grid=(depth, n // tn),
        in_specs=[pl.BlockSpec((rows, d), lambda l, j: (0, 0)),
                  pl.BlockSpec((1, d, tn), lambda l, j: (l, 0, j)),
                  pl.BlockSpec((1, 1, tn), lambda l, j: (l, 0, j))],
        out_specs=pl.BlockSpec((1, rows, tn), lambda l, j: (l, 0, j)),
        out_shape=jax.ShapeDtypeStruct((depth, rows, n), F32),
        compiler_params=_cparams("arbitrary", "arbitrary"),
        name="ada_mods",
    )(c_all, ada_w, ada_b.reshape(depth, 1, n))


def _head_block_ones(n):
    r = lax.broadcasted_iota(jnp.int32, (n, n), 0) // NA_HEAD_DIM
    c = lax.broadcasted_iota(jnp.int32, (n, n), 1) // NA_HEAD_DIM
    return (r == c).astype(F32)


def _nm_linear_kernel(*refs, n_out, head_norm, tn):
    x_ref, g_ref, mod_ref = refs[:3]
    w_refs = refs[3:3 + n_out]
    hg_refs = refs[3 + n_out:3 + n_out + sum(head_norm)]
    o_refs = refs[3 + n_out + sum(head_norm):]
    h = _rms_mod(x_ref[0], g_ref[...], mod_ref[0, 0:1, :], mod_ref[0, 1:2, :]).astype(BF16)
    hg_i = 0
    for w_ref, o_ref, hn in zip(w_refs, o_refs, head_norm):
        n = w_ref.shape[1]
        for j in range(n // tn):
            y = _dot(h, w_ref[:, j * tn:(j + 1) * tn])
            if hn:
                ss = jnp.dot(y * y, _head_block_ones(tn), preferred_element_type=F32, precision=HIGHEST)
                y = y * lax.rsqrt(ss * (1.0 / NA_HEAD_DIM) + EPS) * hg_refs[hg_i][:, j * tn:(j + 1) * tn]
            o_ref[0, :, j * tn:(j + 1) * tn] = y.astype(o_ref.dtype)
        hg_i += hn


def _nm_linear(x, g, mod, ws, out_dtypes, head_gains=None, tm=512, tn=256):
    bn, sn, d = x.shape
    tm = min(tm, sn)
    n_out = len(ws)
    head_gains = head_gains or [None] * n_out
    head_norm = tuple(hg is not None for hg in head_gains)
    hgs = [hg for hg in head_gains if hg is not None]
    in_specs = [pl.BlockSpec((1, tm, d), lambda b, i: (b, i, 0)),
                pl.BlockSpec((1, d), lambda b, i: (0, 0)),
                pl.BlockSpec((1, 2, d), lambda b, i: (b, 0, 0))]
    in_specs += [pl.BlockSpec(w.shape, lambda b, i: (0, 0)) for w in ws]
    in_specs += [pl.BlockSpec(hg.shape, lambda b, i: (0, 0)) for hg in hgs]
    out_specs = [pl.BlockSpec((1, tm, w.shape[1]), lambda b, i: (b, i, 0)) for w in ws]
    out_shape = [jax.ShapeDtypeStruct((bn, sn, w.shape[1]), dt) for w, dt in zip(ws, out_dtypes)]
    return pl.pallas_call(
        functools.partial(_nm_linear_kernel, n_out=n_out, head_norm=head_norm, tn=tn),
        grid=(bn, sn // tm),
        in_specs=in_specs, out_specs=out_specs, out_shape=out_shape,
        compiler_params=_cparams("arbitrary", "arbitrary"),
        name="nm_linear",
    )(x, g, mod, *ws, *hgs)


def _out_linear_kernel(*refs, n_in):
    a_refs = refs[:n_in]
    w_refs = refs[n_in:2 * n_in]
    x_ref, mod_ref, g_ref, xo_ref, ho_ref = refs[2 * n_in:]
    y = _dot(a_refs[0][0], w_refs[0][...])
    for a_ref, w_ref in zip(a_refs[1:], w_refs[1:]):
        y = y + _dot(a_ref[0], w_ref[...])
    xn = x_ref[0] + mod_ref[0, 0:1, :] * y
    xo_ref[0] = xn
    ho_ref[0] = _rms_mod(xn, g_ref[...], mod_ref[0, 1:2, :], mod_ref[0, 2:3, :]).astype(BF16)


def _out_linear(acts, ws, x, mod, g, tm=512):
    bn, sn, d = x.shape
    tm = min(tm, sn)
    n_in = len(acts)
    in_specs = [pl.BlockSpec((1, tm, a.shape[2]), lambda b, i: (b, i, 0)) for a in acts]
    in_specs += [pl.BlockSpec(w.shape, lambda b, i: (0, 0)) for w in ws]
    in_specs += [pl.BlockSpec((1, tm, d), lambda b, i: (b, i, 0)),
                 pl.BlockSpec((1, 3, d), lambda b, i: (b, 0, 0)),
                 pl.BlockSpec((1, d), lambda b, i: (0, 0))]
    return pl.pallas_call(
        functools.partial(_out_linear_kernel, n_in=n_in),
        grid=(bn, sn // tm),
        in_specs=in_specs,
        out_specs=[pl.BlockSpec((1, tm, d), lambda b, i: (b, i, 0))] * 2,
        out_shape=[jax.ShapeDtypeStruct((bn, sn, d), F32), jax.ShapeDtypeStruct((bn, sn, d), BF16)],
        compiler_params=_cparams("arbitrary", "arbitrary"),
        name="out_linear",
    )(*acts, *ws, x, mod, g)


CONV_HALO = 16


def _conv_chunk(src_ref, s, seg_len, w, bias, rows=SSD_CHUNK):
    ncol = src_ref.shape[2]
    if s > 0:
        prev = src_ref[0, s - CONV_HALO:s, :].astype(F32)
    else:
        prev = jnp.zeros((CONV_HALO, ncol), F32)
    cur = src_ref[0, s:s + rows, :].astype(F32)
    if s + rows < seg_len:
        nxt = src_ref[0, s + rows:s + rows + CONV_HALO, :].astype(F32)
    else:
        nxt = jnp.zeros((CONV_HALO, ncol), F32)
    win = jnp.concatenate([prev, cur, nxt], axis=0)
    taps = w.shape[0]
    acc = bias
    for k in range(taps):
        off = CONV_HALO - taps // 2 + k
        acc = acc + win[off:off + rows, :] * w[k:k + 1, :]
    return acc


def _lane_head_expand(cols, width):
    nh = len(cols)
    hd = width // nh
    rows = cols[0].shape[0]
    lane_head = lax.broadcasted_iota(jnp.int32, (rows, width), 1) // hd
    out = jnp.broadcast_to(cols[nh - 1], (rows, width))
    for h in range(nh - 2, -1, -1):
        out = jnp.where(lane_head == h, jnp.broadcast_to(cols[h], (rows, width)), out)
    return out


def _ssd_kernel(xs_x, bm_x, cm_x, xs_c, bm_c, cm_c, dt_x, dt_c, z_x, z_c,
                cw_xs, cw_b, cw_c, cb_xs, cb_b, cb_c, alog_ref, dtb_ref, dsk_ref, ng_ref,
                y_x, y_c,
                xs_s, bm_s, cm_s, dt_s, y_s, st_s, cs_s, cst_s, *, sx, sc):
    q = SSD_CHUNK
    nc_c, nc_x = sc // q, sx // q
    nc = nc_c + nc_x
    gw = xs_s.shape[1]

    for seg_ref3, seg_len, base in (((xs_c, bm_c, cm_c), sc, 0), ((xs_x, bm_x, cm_x), sx, sc)):
        for ci in range(seg_len // q):
            s = ci * q
            for src, dst, w_ref, b_ref in zip(seg_ref3, (xs_s, bm_s, cm_s), (cw_xs, cw_b, cw_c),
                                              (cb_xs, cb_b, cb_c)):
                dst[base + s:base + s + q, :] = _silu(_conv_chunk(src, s, seg_len, w_ref[...], b_ref[...]))
    dt_s[0:sc, :] = _softplus(dt_c[0] + dtb_ref[0])
    dt_s[sc:sc + sx, :] = _softplus(dt_x[0] + dtb_ref[0])

    a_neg = -jnp.exp(alog_ref[0])
    row = lax.broadcasted_iota(jnp.int32, (q, q), 0)
    col = lax.broadcasted_iota(jnp.int32, (q, q), 1)
    tri = ((col <= row).astype(F32), (col >= row).astype(F32))
    keep = (col <= row, col >= row)
    lane_head = lax.broadcasted_iota(jnp.int32, (q, gw), 1) // SSD_HEAD_DIM

    st_s[...] = jnp.zeros_like(st_s)
    y_s[...] = jnp.zeros_like(y_s)

    for ci in range(nc):
        la = dt_s[ci * q:(ci + 1) * q, :] * a_neg
        for d in range(2):
            cs = jnp.dot(tri[d], la, preferred_element_type=F32, precision=HIGHEST)
            cs_s[d, ci * q:(ci + 1) * q, :] = cs
            cst_s[d, ci * q:(ci + 1) * q, :] = cs.T

    def chunk_body(i, carry):
        for d in range(2):
            if d == 0:
                ci = i
            else:
                ci = jnp.where(i < nc_c, nc_c - 1 - i, nc + nc_c - 1 - i)
            r0 = pl.multiple_of(ci * q, q)
            xs = xs_s[pl.ds(r0, q), :]
            bm = bm_s[pl.ds(r0, q), :]
            cm = cm_s[pl.ds(r0, q), :]
            dt = dt_s[pl.ds(r0, q), :]
            cs = cs_s[d, pl.ds(r0, q), :]
            cs_t = cst_s[d, pl.ds(r0, q), :]
            cb = _dot_nt(cm.astype(BF16), bm.astype(BF16))
            heads = [d * SSD_HPG + h for h in range(SSD_HPG)]
            dt_mat = _lane_head_expand([dt[:, c:c + 1] for c in heads], gw)
            cs_mat = _lane_head_expand([cs[:, c:c + 1] for c in heads], gw)
            xd = xs * dt_mat
            xd_b = xd.astype(BF16)
            y = jnp.zeros((q, gw), F32)
            for h, c in enumerate(heads):
                diff = cs[:, c:c + 1] - cs_t[c:c + 1, :]
                lmat = jnp.exp(jnp.where(keep[d], diff, NEG_INF))
                y = jnp.where(lane_head == h, _dot((cb * lmat).astype(BF16), xd_b), y)
            st = st_s[d]
            y = y + _dot(cm.astype(BF16), st.astype(BF16)) * jnp.exp(cs_mat)
            end = q - 1 if d == 0 else 0
            cs_end = cs_mat[end:end + 1, :]
            s_new = _dot(bm.T.astype(BF16), (xd * jnp.exp(cs_end - cs_mat)).astype(BF16))
            st_s[d] = st * jnp.exp(cs_end) + s_new
            y_s[pl.ds(r0, q), :] = y_s[pl.ds(r0, q), :] + y
        return carry

    lax.fori_loop(0, nc, chunk_body, 0)

    for ci in range(nc):
        s = ci * q
        if ci < nc_c:
            z = z_c[0, s:s + q, :]
        else:
            z = z_x[0, s - sc:s - sc + q, :]
        y = (y_s[s:s + q, :] + dsk_ref[0] * xs_s[s:s + q, :]) * _silu(z.astype(F32))
        ms = jnp.mean(y * y, axis=-1, keepdims=True)
        out = (y * lax.rsqrt(ms + EPS) * ng_ref[0]).astype(BF16)
        if ci < nc_c:
            y_c[0, s:s + q, :] = out
        else:
            y_x[0, s - sc:s - sc + q, :] = out


def _ssd_mixer(xbc_x, xbc_c, dt_x, dt_c, z_x, z_c, conv_w, conv_b, alog_g, dtb_g, dsk_g, ng_g):
    bsz, sx, _ = xbc_x.shape
    sc = xbc_c.shape[1]
    g, gw, n = SSD_GROUPS, SSD_GW, SSD_STATE
    nb = SSD_INNER // n
    taps = conv_w.shape[0]

    def seq(s, w, off):
        return pl.BlockSpec((1, s, w), lambda b, j, off=off: (b, 0, off + j))

    def par(r, w, off):
        return pl.BlockSpec((r, w), lambda b, j, off=off: (0, off + j))

    def grp(w):
        return pl.BlockSpec((1, 1, w), lambda b, j: (j, 0, 0))

    in_specs = [seq(sx, gw, 0), seq(sx, n, nb), seq(sx, n, nb + g),
                seq(sc, gw, 0), seq(sc, n, nb), seq(sc, n, nb + g),
                seq(sx, LANES, 0), seq(sc, LANES, 0), seq(sx, gw, 0), seq(sc, gw, 0),
                par(taps, gw, 0), par(taps, n, nb), par(taps, n, nb + g),
                par(1, gw, 0), par(1, n, nb), par(1, n, nb + g),
                grp(LANES), grp(LANES), grp(gw), grp(gw)]
    stot = sx + sc
    return pl.pallas_call(
        functools.partial(_ssd_kernel, sx=sx, sc=sc),
        grid=(bsz, g),
        in_specs=in_specs,
        out_specs=[seq(sx, gw, 0), seq(sc, gw, 0)],
        out_shape=[jax.ShapeDtypeStruct((bsz, sx, SSD_INNER), BF16),
                   jax.ShapeDtypeStruct((bsz, sc, SSD_INNER), BF16)],
        scratch_shapes=[pltpu.VMEM((stot, gw), F32), pltpu.VMEM((stot, n), F32), pltpu.VMEM((stot, n), F32),
                        pltpu.VMEM((stot, LANES), F32), pltpu.VMEM((stot, gw), F32), pltpu.VMEM((2, n, gw), F32),
                        pltpu.VMEM((2, stot, LANES), F32), pltpu.VMEM((2, stot, LANES), F32)],
        compiler_params=_cparams("arbitrary", "arbitrary"),
        name="ssd_mixer",
    )(xbc_x, xbc_x, xbc_x, xbc_c, xbc_c, xbc_c, dt_x, dt_c, z_x, z_c,
      conv_w, conv_w, conv_w, conv_b, conv_b, conv_b, alog_g, dtb_g, dsk_g, ng_g)


def _lru_kernel(xl_x, xl_c, gt_x, gt_c, cw, cb, wa, wx, ba, bx, lam, y_x, y_c,
                xr_s, a_s, b_s, y_s, *, sx, sc):
    q = SSD_CHUNK
    stot = sx + sc
    w = xr_s.shape[1]
    for src, seg_len, base in ((xl_c, sc, 0), (xl_x, sx, sc)):
        for ci in range(seg_len // q):
            s = ci * q
            xr_s[base + s:base + s + q, :] = _conv_chunk(src, s, seg_len, cw[...], cb[...])

    ng = stot // SUBLANES
    ng_c = sc // SUBLANES
    sub = lax.broadcasted_iota(jnp.int32, (SUBLANES, w), 0)
    rt = q
    for d in range(2):
        nsp = _softplus(-lam[d:d + 1, :])
        for ci in range(stot // rt):
            s = ci * rt
            xr = xr_s[s:s + rt, :]
            xb = xr.astype(BF16)
            r = _sigmoid(_dot(xb, wa[d, 0]) + ba[d:d + 1, :])
            ig = _sigmoid(_dot(xb, wx[d, 0]) + bx[d:d + 1, :])
            a = jnp.exp(-LRU_C * r * nsp)
            a_s[d, s:s + rt, :] = a
            b_s[d, s:s + rt, :] = jnp.sqrt(1.0 - a * a) * (ig * xr)

    y_s[...] = jnp.zeros_like(y_s)

    def group_body(k, carries):
        new = []
        for d in range(2):
            if d == 0:
                gi = k
            else:
                gi = jnp.where(k < ng_c, ng_c - 1 - k, ng + ng_c - 1 - k)
            r0 = pl.multiple_of(gi * SUBLANES, SUBLANES)
            a = a_s[d, pl.ds(r0, SUBLANES), :]
            b = b_s[d, pl.ds(r0, SUBLANES), :]
            for sh in (1, 2, 4):
                if d == 0:
                    valid = sub >= sh
                    a_sh = pltpu.roll(a, sh, axis=0)
                    b_sh = pltpu.roll(b, sh, axis=0)
                else:
                    valid = sub < SUBLANES - sh
                    a_sh = pltpu.roll(a, SUBLANES - sh, axis=0)
                    b_sh = pltpu.roll(b, SUBLANES - sh, axis=0)
                b = jnp.where(valid, a * b_sh + b, b)
                a = jnp.where(valid, a * a_sh, a)
            h = a * carries[d] + b
            y_s[pl.ds(r0, SUBLANES), :] = y_s[pl.ds(r0, SUBLANES), :] + h
            last = h[SUBLANES - 1:SUBLANES, :] if d == 0 else h[0:1, :]
            new.append(jnp.broadcast_to(last, (SUBLANES, w)))
        return tuple(new)

    zero = jnp.zeros((SUBLANES, w), F32)
    lax.fori_loop(0, ng, group_body, (zero, zero))

    for ci in range(stot // rt):
        s = ci * rt
        if s < sc:
            gate = gt_c[0, s:s + rt, :]
        else:
            gate = gt_x[0, s - sc:s - sc + rt, :]
        out = (y_s[s:s + rt, :] * _gelu_tanh(gate.astype(F32))).astype(BF16)
        if s < sc:
            y_c[0, s:s + rt, :] = out
        else:
            y_x[0, s - sc:s - sc + rt, :] = out


def _lru_mixer(xl_x, xl_c, gt_x, gt_c, conv_w, conv_b, wa_bd, wx_bd, ba, bx, lam):
    bsz, sx, width = xl_x.shape
    sc = xl_c.shape[1]
    w = LRU_TILE
    taps = conv_w.shape[0]

    def seq(s):
        return pl.BlockSpec((1, s, w), lambda b, j: (b, 0, j))

    def par(r):
        return pl.BlockSpec((r, w), lambda b, j: (0, j))

    wspec = pl.BlockSpec((2, 1, w, w), lambda b, j: (0, j, 0, 0))
    stot = sx + sc
    return pl.pallas_call(
        functools.partial(_lru_kernel, sx=sx, sc=sc),
        grid=(bsz, width // w),
        in_specs=[seq(sx), seq(sc), seq(sx), seq(sc), par(taps), par(1), wspec, wspec, par(2), par(2), par(2)],
        out_specs=[seq(sx), seq(sc)],
        out_shape=[jax.ShapeDtypeStruct((bsz, sx, width), BF16), jax.ShapeDtypeStruct((bsz, sc, width), BF16)],
        scratch_shapes=[pltpu.VMEM((stot, w), F32), pltpu.VMEM((2, stot, w), F32), pltpu.VMEM((2, stot, w), F32),
                        pltpu.VMEM((stot, w), F32)],
        compiler_params=_cparams("arbitrary", "arbitrary"),
        name="lru_mixer",
    )(xl_x, xl_c, gt_x, gt_c, conv_w, conv_b, wa_bd, wx_bd, ba, bx, lam)


def _rpb_table_kernel(rpb_ref, o_ref, *, n_dr, n_dc):
    h = pl.program_id(0)
    w = GRID_W
    qcol = lax.broadcasted_iota(jnp.int32, (w, 2 * w), 0)
    lane = lax.broadcasted_iota(jnp.int32, (w, 2 * w), 1)
    kcol = lane % w
    hi = lane >= w
    rel = kcol - qcol + (NA_KW - 1)
    cstart = jnp.clip(qcol - NA_KW // 2, 0, w - NA_KW)
    in_win = (kcol >= cstart) & (kcol < cstart + NA_KW)
    for d in range(n_dr + 1):
        acc = jnp.full((w, 2 * w), NEG_INF, F32)
        for dc in range(n_dc):
            lo = rpb_ref[(h * n_dr + d - 1) * n_dc + dc] if d >= 1 else NEG_INF
            up = rpb_ref[(h * n_dr + d) * n_dc + dc] if d < n_dr else NEG_INF
            acc = jnp.where(rel == dc, jnp.where(hi, up, lo), acc)
        valid = in_win
        if d == 0:
            valid = valid & hi
        if d == n_dr:
            valid = valid & jnp.logical_not(hi)
        o_ref[0, d] = jnp.where(valid, acc, NEG_INF)


def _rpb_table(rpb):
    nh, n_dr, n_dc = rpb.shape
    return pl.pallas_call(
        functools.partial(_rpb_table_kernel, n_dr=n_dr, n_dc=n_dc),
        grid=(nh,),
        in_specs=[pl.BlockSpec(memory_space=pltpu.SMEM)],
        out_specs=pl.BlockSpec((1, n_dr + 1, GRID_W, 2 * GRID_W), lambda h: (h, 0, 0, 0)),
        out_shape=jax.ShapeDtypeStruct((nh, n_dr + 1, GRID_W, 2 * GRID_W), F32),
        compiler_params=_cparams("arbitrary"),
        name="rpb_table",
    )(rpb.reshape(-1))


def _na_kernel(*refs, sx, sc, want_ctx):
    if want_ctx:
        q_x, k_x, v_x, q_c, k_c, v_c, tab, o_x, o_c = refs
    else:
        q_x, k_x, v_x, k_c, v_c, tab, o_x = refs
    w = GRID_W
    rows = sx // w
    qb = NA_QROWS * w
    kb = NA_KROWS * w
    n_blk = rows // NA_QROWS
    scale = NA_HEAD_DIM ** -0.5
    lane = lax.broadcasted_iota(jnp.int32, (1, 2 * NA_HEAD_DIM), 1)
    in_head = (lane < NA_HEAD_DIM, lane >= NA_HEAD_DIM)
    keyrow = lax.broadcasted_iota(jnp.int32, (1, kb), 1) // w
    kc = k_c[0]
    vc = v_c[0]
    n_tab = tab.shape[1]

    def softmax_pv(parts):
        m = parts[0][0].max(axis=-1, keepdims=True)
        for s, _ in parts[1:]:
            m = jnp.maximum(m, s.max(axis=-1, keepdims=True))
        acc, den = None, None
        for s, v in parts:
            e = jnp.exp(s - m)
            den = e.sum(axis=-1, keepdims=True) if den is None else den + e.sum(axis=-1, keepdims=True)
            pv = _dot(e.astype(BF16), v)
            acc = pv if acc is None else acc + pv
        return acc / den

    def block_body(rb, carry):
        ws = jnp.clip(NA_QROWS * rb - NA_KH // 2, 0, rows - NA_KROWS)
        q0 = pl.multiple_of(rb * qb, qb)
        k0 = pl.multiple_of(ws * w, w)
        qblk = q_x[0, pl.ds(q0, qb), :]
        kwin = k_x[0, pl.ds(k0, kb), :]
        vwin = v_x[0, pl.ds(k0, kb), :]
        out = jnp.zeros((qb, 2 * NA_HEAD_DIM), F32)
        for hh in range(2):
            qm = jnp.where(in_head[hh], qblk * scale, jnp.zeros_like(qblk))
            s_loc = _dot_nt(qm, kwin)
            s_ctx = _dot_nt(qm, kc)
            pieces = []
            for rq in range(NA_QROWS):
                r = NA_QROWS * rb + rq
                rs = jnp.clip(r - NA_KH // 2, 0, rows - NA_KH)
                lo = rs - ws
                valid = (keyrow >= lo) & (keyrow < lo + NA_KH)
                blocks = []
                for ip in range(NA_KROWS // 2):
                    dr_lo = ws + 2 * ip - r + NA_KH - 1
                    blocks.append(tab[hh, jnp.clip(dr_lo + 1, 0, n_tab - 1)])
                bias = jnp.concatenate(blocks, axis=1)
                piece = s_loc[rq * w:(rq + 1) * w, :] + bias
                pieces.append(jnp.where(valid, piece, NEG_INF))
            s_loc = jnp.concatenate(pieces, axis=0)
            o = softmax_pv([(s_loc, vwin), (s_ctx, vc)])
            out = jnp.where(in_head[hh], o, out)
        o_x[0, pl.ds(q0, qb), :] = out.astype(o_x.dtype)
        return carry

    lax.fori_loop(0, n_blk, block_body, 0)

    if want_ctx:
        qc = q_c[0]
        out = jnp.zeros((sc, 2 * NA_HEAD_DIM), F32)
        for hh in range(2):
            qm = jnp.where(in_head[hh], qc * scale, jnp.zeros_like(qc))
            o = softmax_pv([(_dot_nt(qm, kc), vc)])
            out = jnp.where(in_head[hh], o, out)
        o_c[0] = out.astype(o_c.dtype)


def _na_attention(q_x, k_x, v_x, q_c, k_c, v_c, table, want_ctx):
    bsz, sx, dim = q_x.shape
    sc = k_c.shape[1]
    pw = 2 * NA_HEAD_DIM
    n_pair = dim // pw

    def seq(s):
        return pl.BlockSpec((1, s, pw), lambda p, b: (b, 0, p))

    tspec = pl.BlockSpec((2,) + table.shape[1:], lambda p, b: (p, 0, 0, 0))
    if want_ctx:
        args = (q_x, k_x, v_x, q_c, k_c, v_c, table)
        in_specs = [seq(sx)] * 3 + [seq(sc)] * 3 + [tspec]
        out_specs = [seq(sx), seq(sc)]
        out_shape = [jax.ShapeDtypeStruct((bsz, sx, dim), BF16), jax.ShapeDtypeStruct((bsz, sc, dim), BF16)]
    else:
        args = (q_x, k_x, v_x, k_c, v_c, table)
        in_specs = [seq(sx)] * 3 + [seq(sc)] * 2 + [tspec]
        out_specs = [seq(sx)]
        out_shape = [jax.ShapeDtypeStruct((bsz, sx, dim), BF16)]
    res = pl.pallas_call(
        functools.partial(_na_kernel, sx=sx, sc=sc, want_ctx=want_ctx),
        grid=(n_pair, bsz),
        in_specs=in_specs, out_specs=out_specs, out_shape=out_shape,
        compiler_params=_cparams("arbitrary", "arbitrary"),
        name="na_attention",
    )(*args)
    return res if want_ctx else (res[0], None)


def _top_rows(s, k, exact, want_rank=True):
    n, tt = s.shape
    top_id = lax.broadcasted_iota(jnp.int32, (k, tt), 0)
    if exact:
        rowid = lax.broadcasted_iota(jnp.int32, (n, tt), 0).astype(F32)
    work = s
    top = jnp.zeros((k, tt), F32)
    rank = jnp.full((n, tt), float(k), F32) if want_rank else None
    for it in range(k):
        m = jnp.max(work, axis=0, keepdims=True)
        sel = work == m
        if exact:
            sel = rowid == jnp.min(jnp.where(sel, rowid, float(n)), axis=0, keepdims=True)
        top = jnp.where(top_id == it, m, top)
        if want_rank:
            rank = jnp.where(sel, float(it), rank)
        work = jnp.where(sel, -jnp.inf, work)
    picked = work == -jnp.inf
    n_sel = jnp.sum(jnp.where(picked, 1.0, 0.0), axis=0, keepdims=True)
    return top, rank, picked, n_sel


def _peer_route_kernel(h_ref, wq_ref, keys_ref, cnt_ref, rk_ref, e0_ref, e1_ref, s_s):
    nk, k, ic = PEER_KEYS, PEER_TOPK, PEER_ICHUNK
    tt = h_ref.shape[1]
    sw = s_s.shape[3]
    n_strip = tt // sw
    n_unit = PEER_HEADS * n_strip

    def scores(u):
        h, si = u // n_strip, u % n_strip
        rows = pl.ds(pl.multiple_of(si * sw, sw), sw)
        q = _dot(h_ref[0, rows, :], wq_ref[h]).astype(BF16)
        for z in range(2):
            s_s[u % 2, z] = _dot_nt(keys_ref[2 * h + z], q[:, z * nk:(z + 1) * nk])

    def pair_stage(t0, t1, exact):
        cand = jnp.concatenate([t0[0:1] + t1] + [t0[a:a + 1] + t1[0:8] for a in range(1, 8)]
                               + [t0[8:16] + t1[0:1]], axis=0)
        _, _, picked, n_sel = _top_rows(cand, k, exact, want_rank=False)
        pf = jnp.where(picked, 1.0, 0.0)
        z_sum = jnp.sum(pf * jnp.exp(cand - cand[0:1]), axis=0, keepdims=True)
        cnts = [jnp.sum(pf[0:k], axis=0, keepdims=True)]
        cnts += [jnp.sum(pf[k + 8 * (a - 1):k + 8 * a], axis=0, keepdims=True) for a in range(1, 8)]
        cnts += [pf[k + 56 + a:k + 57 + a] for a in range(8)]
        return cnts, z_sum, n_sel

    def route(u, prefetch, exact):
        h, si = u // n_strip, u % n_strip
        cols = pl.ds(pl.multiple_of(si * sw, sw), sw)
        s0, s1 = s_s[u % 2, 0], s_s[u % 2, 1]
        if prefetch:
            scores(u + 1)
        t0, rank0, _, n0 = _top_rows(s0, k, exact, want_rank=exact)
        t1, rank1, _, n1 = _top_rows(s1, k, exact)
        cnts, z_sum, n2 = pair_stage(t0, t1, exact)
        cnt_i = jnp.zeros_like(s0)
        for a in range(k):
            hit = (rank0 == float(a)) if exact else (s0 == t0[a:a + 1])
            cnt_i = jnp.where(hit, cnts[a], cnt_i)
        e0 = jnp.exp(s0 - t0[0:1]) / z_sum
        hi = pl.ds(pl.multiple_of(h * ic, ic), ic)
        hj = pl.ds(pl.multiple_of(h * nk, nk), nk)
        for c in range(nk // ic):
            cnt_ref[0, c, hi, cols] = cnt_i[c * ic:(c + 1) * ic]
            e0_ref[0, c, hi, cols] = e0[c * ic:(c + 1) * ic]
        rk_ref[0, hj, cols] = rank1.astype(rk_ref.dtype)
        e1_ref[0, hj, cols] = jnp.exp(s1 - t1[0:1]).astype(e1_ref.dtype)
        return jnp.abs(n0 - float(k)) + jnp.abs(n1 - float(k)) + jnp.abs(n2 - float(k))

    def route_tile(exact):
        scores(jnp.int32(0))
        ties = lax.fori_loop(0, n_unit - 1, lambda u, t: jnp.maximum(t, route(u, True, exact)),
                             jnp.zeros((1, sw), F32))
        return jnp.maximum(ties, route(jnp.int32(n_unit - 1), False, exact))

    ties = route_tile(False)

    @pl.when(jnp.max(ties) > 0.5)
    def _():
        route_tile(True)


ROUTE_STRIP = 256


def _peer_route(hx, w_q, keys, tt):
    bn, sn, d = hx.shape
    nk, nh, ic = PEER_KEYS, PEER_HEADS, PEER_ICHUNK
    spec_i = pl.BlockSpec((1, nk // ic, nh * ic, tt), lambda b, i: (b, 0, 0, i))
    spec_j = pl.BlockSpec((1, nh * nk, tt), lambda b, i: (b, 0, i))
    shape_i = jax.ShapeDtypeStruct((bn, nk // ic, nh * ic, sn), F32)
    shape_j = jax.ShapeDtypeStruct((bn, nh * nk, sn), BF16)
    sw = min(ROUTE_STRIP, tt)
    return pl.pallas_call(
        _peer_route_kernel,
        grid=(bn, sn // tt),
        in_specs=[pl.BlockSpec((1, tt, d), lambda b, i: (b, i, 0)),
                  pl.BlockSpec(w_q.shape, lambda b, i: (0, 0, 0)),
                  pl.BlockSpec(keys.shape, lambda b, i: (0, 0, 0))],
        out_specs=[spec_i, spec_j, spec_i, spec_j],
        out_shape=[shape_i, shape_j, shape_i, shape_j],
        scratch_shapes=[pltpu.VMEM((2, 2, nk, sw), F32)],
        compiler_params=_cparams("arbitrary", "arbitrary"),
        name="peer_route",
    )(hx, w_q, keys)


def _peer_dense_kernel(h_ref, cnt_ref, rk_ref, e0_ref, e1_ref, u_ref, vt_ref, x_ref, g_ref, o_ref,
                       acc_ref, act_ref, p_ref, hx_s, rk_s, e1_s, cnt_s, e0_s):
    nk, ic = PEER_KEYS, PEER_ICHUNK
    ck = pl.program_id(2)
    n_slab = act_ref.shape[0]
    sw = min(PEER_STRIP, n_slab * LANES)
    per = sw // LANES
    n_strip = n_slab // per
    il_group, j_group = 4, 2
    pk = _rows_per_word(BF16)

    @pl.when(ck == 0)
    def _():
        acc_ref[...] = jnp.zeros_like(acc_ref)
        hx_s[...] = h_ref[0]
        for t in range(n_slab):
            rk_s[t] = _to_words(rk_ref[0, :, t * LANES:(t + 1) * LANES])
            e1_s[t] = _to_words(e1_ref[0, :, t * LANES:(t + 1) * LANES])

    for t in range(n_slab):
        cnt_s[t] = cnt_ref[0, 0, :, t * LANES:(t + 1) * LANES]
        e0_s[t] = e0_ref[0, 0, :, t * LANES:(t + 1) * LANES]

    def activations(s):
        r0 = pl.multiple_of(s * sw, sw)
        a = _gelu_tanh(_dot_nt(_from_words(u_ref[...], BF16), hx_s[pl.ds(r0, sw), :])).astype(BF16)
        for k in range(per):
            act_ref[s * per + k] = _to_words(a[:, k * LANES:(k + 1) * LANES])

    def gate_weights(t, ig):
        jr = nk // j_group
        wgt = [[None] * il_group for _ in range(j_group)]
        for h in range(PEER_HEADS):
            cnt8 = cnt_s[t, h * ic:(h + 1) * ic, :]
            e08 = e0_s[t, h * ic:(h + 1) * ic, :]
            rows = []
            for g in range(il_group):
                il = ig * il_group + g
                rows.append((jnp.broadcast_to(cnt8[il:il + 1], (jr, LANES)).astype(BF16),
                             jnp.broadcast_to(e08[il:il + 1], (jr, LANES)).astype(BF16)))
            for jg in range(j_group):
                j0 = (h * nk + jg * jr) // pk
                rk = _from_words(rk_s[t, j0:j0 + jr // pk, :], BF16)
                e1 = _from_words(e1_s[t, j0:j0 + jr // pk, :], BF16)
                for g in range(il_group):
                    term = jnp.where(rk < rows[g][0], e1 * rows[g][1], jnp.zeros_like(e1))
                    wgt[jg][g] = term if wgt[jg][g] is None else wgt[jg][g] + term
        for jg in range(j_group):
            for g in range(il_group):
                r0 = ((ig * il_group + g) * nk + jg * jr) // pk
                act = _from_words(act_ref[t, r0:r0 + jr // pk, :], BF16)
                p_ref[t, r0:r0 + jr // pk, :] = _to_words(wgt[jg][g] * act)

    def combine(s):
        for k in range(per):
            for ig in range(ic // il_group):
                gate_weights(s * per + k, ig)
        p = _from_words(jnp.concatenate([p_ref[s * per + k] for k in range(per)], axis=1), BF16)
        y = _dot(_from_words(vt_ref[...], BF16), p)
        for k in range(per):
            acc_ref[s * per + k] += y[:, k * LANES:(k + 1) * LANES]

    activations(0)

    def strip(s, carry):
        activations(s + 1)
        combine(s)
        return carry

    lax.fori_loop(0, n_strip - 1, strip, 0)
    combine(n_strip - 1)

    @pl.when(ck == pl.num_programs(2) - 1)
    def _():
        for t in range(n_slab):
            rows = slice(t * LANES, (t + 1) * LANES)
            o_ref[0, rows, :] = x_ref[0, rows, :] + g_ref[0] * acc_ref[t].T


PEER_STRIP = 256


def _rows_per_word(dt):
    return 4 // jnp.dtype(dt).itemsize


def _to_words(x):
    return pltpu.bitcast(x, jnp.uint32) if x.dtype.itemsize == 2 else x


def _from_words(x, dt):
    return pltpu.bitcast(x, dt) if jnp.dtype(dt).itemsize == 2 else x


def _pack_weight_kernel(w_ref, o_ref, *, transpose):
    w = w_ref[0].T if transpose else w_ref[0]
    o_ref[...] = _to_words(w.astype(BF16))


def _pack_weight(w, layer, transpose):
    _, rows, cols = w.shape
    pk = _rows_per_word(BF16)
    blk = 1024
    if transpose:
        out_shape, out_spec = (cols // pk, rows), pl.BlockSpec((cols // pk, blk), lambda i: (0, i))
    else:
        out_shape, out_spec = (rows // pk, cols), pl.BlockSpec((blk // pk, cols), lambda i: (i, 0))
    return pl.pallas_call(
        functools.partial(_pack_weight_kernel, transpose=transpose),
        grid=(rows // blk,),
        in_specs=[pl.BlockSpec((1, blk, cols), lambda i: (layer, i, 0))],
        out_specs=out_spec,
        out_shape=jax.ShapeDtypeStruct(out_shape, jnp.uint32 if pk == 2 else BF16),
        compiler_params=_cparams("arbitrary"),
        name="pack_weight",
    )(w)


def _peer_dense(hx, route, u, v_t, x, gate, tt):
    bn, sn, d = hx.shape
    nk, nh, ic = PEER_KEYS, PEER_HEADS, PEER_ICHUNK
    ne = ic * nk
    pk = _rows_per_word(BF16)
    wdt = jnp.uint32 if pk == 2 else BF16
    n_chunk = u.shape[0] * pk // ne
    spec_i = pl.BlockSpec((1, 1, nh * ic, tt), lambda b, i, c: (b, c, 0, i))
    spec_j = pl.BlockSpec((1, nh * nk, tt), lambda b, i, c: (b, 0, i))
    tok = pl.BlockSpec((1, tt, d), lambda b, i, c: (b, i, 0))
    n_slab = tt // LANES
    return pl.pallas_call(
        _peer_dense_kernel,
        grid=(bn, sn // tt, n_chunk),
        in_specs=[tok, spec_i, spec_j, spec_i, spec_j,
                  pl.BlockSpec((ne // pk, d), lambda b, i, c: (c, 0)),
                  pl.BlockSpec((d // pk, ne), lambda b, i, c: (0, c)),
                  pl.BlockSpec((1, tt, d), lambda b, i, c: (b, i, 0), pipeline_mode=pl.Buffered(1)),
                  pl.BlockSpec((1, 1, d), lambda b, i, c: (b, 0, 0))],
        out_specs=tok,
        out_shape=jax.ShapeDtypeStruct((bn, sn, d), F32),
        scratch_shapes=[pltpu.VMEM((n_slab, d, LANES), F32), pltpu.VMEM((n_slab, ne // pk, LANES), wdt),
                        pltpu.VMEM((n_slab, ne // pk, LANES), wdt), pltpu.VMEM((tt, d), BF16),
                        pltpu.VMEM((n_slab, nh * nk // pk, LANES), wdt),
                        pltpu.VMEM((n_slab, nh * nk // pk, LANES), wdt),
                        pltpu.VMEM((n_slab, nh * ic, LANES), F32), pltpu.VMEM((n_slab, nh * ic, LANES), F32)],
        compiler_params=_cparams("arbitrary", "arbitrary", "arbitrary"),
        name="peer_dense",
    )(hx, *route, u, v_t, x, gate)


def _peer(hx, x, gate, w_q, keys, u, v_t, tt=1024):
    tt = min(tt, hx.shape[1])
    route = _peer_route(hx, w_q, keys, tt)
    return _peer_dense(hx, route, u, v_t, x, gate, tt)


def _group_lanes(p, width):
    g = p.reshape(2, SSD_GROUPS, SSD_HPG).transpose(1, 0, 2).reshape(SSD_GROUPS, 2 * SSD_HPG)
    return jnp.pad(g, ((0, 0), (0, width - 2 * SSD_HPG))).reshape(SSD_GROUPS, 1, width)


def _block_diag(w, tile):
    two, nb, bd, _ = w.shape
    per = tile // bd
    w = w.reshape(two, nb // per, per, bd, bd)
    eye = jnp.eye(per, dtype=w.dtype)
    return jnp.einsum("dtpij,pq->dtpiqj", w, eye).reshape(two, nb // per, tile, tile)


def kernel(x, c, ctx, c_ctx, ada_w, ada_b, norm1_g, norm2_g, ev_w_in, ev_conv_w, ev_conv_b, ev_a_log,
           ev_dt_bias, ev_d, ev_ssd_norm_g, ev_lru_conv_w, ev_lru_conv_b, ev_lru_wa, ev_lru_ba, ev_lru_wx,
           ev_lru_bx, ev_lru_lam, ev_w_out, od_w_qkv, od_q_norm_g, od_k_norm_g, od_rpb, od_w_o,
           pe_w_q, pe_keys, pe_u, pe_v):
    bsz, sx, d = x.shape
    sc = ctx.shape[1]
    depth = ada_w.shape[0]

    n_c = bsz + 1
    rows = -(-n_c // SUBLANES) * SUBLANES
    c_all = jnp.concatenate([c, c_ctx[None], jnp.zeros((rows - n_c, d), F32)], axis=0)
    mods = _ada_mods(c_all, ada_w, ada_b).reshape(depth, rows, 6, d)

    ctx = ctx.reshape(1, bsz * sc, d)

    def per_batch(t):
        return t.reshape(bsz, sc, t.shape[-1])

    for layer in range(depth):
        last = layer == depth - 1
        j = layer // 2
        mod_x = mods[layer, :bsz]
        mod_c = mods[layer, bsz:bsz + 1]
        g1 = norm1_g[layer][None]
        g2 = norm2_g[layer][None]
        want_ctx = not last

        if layer % 2 == 0:
            w_in = ev_w_in[j]
            o_dt, o_xl = SSD_XBC, SSD_XBC + 2 * SSD_HEADS
            o_z = o_xl + LRU_WIDTH
            o_gate = o_z + SSD_INNER
            w_dt = w_in[:, o_dt:o_xl].reshape(d, 2, SSD_GROUPS, SSD_HPG).transpose(0, 2, 1, 3)
            w_dt = jnp.pad(w_dt.reshape(d, SSD_GROUPS, 2 * SSD_HPG), ((0, 0), (0, 0), (0, LANES - 2 * SSD_HPG)))
            ws = [w_in[:, :o_dt].astype(BF16), w_dt.reshape(d, SSD_GROUPS * LANES).astype(BF16),
                  w_in[:, o_xl:o_z].astype(BF16), w_in[:, o_z:o_gate].astype(BF16), w_in[:, o_gate:].astype(BF16)]
            dts = [BF16, F32, BF16, BF16, BF16]
            px = _nm_linear(x, g1, mod_x[:, 0:2], ws, dts)
            pc = [per_batch(t) for t in _nm_linear(ctx, g1, mod_c[:, 0:2], ws, dts)]
            y_ssd_x, y_ssd_c = _ssd_mixer(
                px[0], pc[0], px[1], pc[1], px[3], pc[3], ev_conv_w[j], ev_conv_b[j][None],
                _group_lanes(ev_a_log[j], LANES), _group_lanes(ev_dt_bias[j], LANES),
                jnp.repeat(ev_d[j], SSD_HEAD_DIM).reshape(SSD_GROUPS, 1, SSD_GW),
                ev_ssd_norm_g[j].reshape(SSD_GROUPS, 1, SSD_GW))
            y_lru_x, y_lru_c = _lru_mixer(
                px[2], pc[2], px[4], pc[4], ev_lru_conv_w[j], ev_lru_conv_b[j][None],
                _block_diag(ev_lru_wa[j], LRU_TILE).astype(BF16), _block_diag(ev_lru_wx[j], LRU_TILE).astype(BF16),
                ev_lru_ba[j], ev_lru_bx[j], ev_lru_lam[j])
            w_out = ev_w_out[j].astype(BF16)
            w_outs = [w_out[:SSD_INNER], w_out[SSD_INNER:]]
            acts_x = [y_ssd_x, y_lru_x]
            acts_c = [y_ssd_c.reshape(1, bsz * sc, -1), y_lru_c.reshape(1, bsz * sc, -1)]
        else:
            w_qkv = od_w_qkv[j].astype(BF16)
            nd = w_qkv.shape[1] // 3
            ws = [w_qkv[:, :nd], w_qkv[:, nd:2 * nd], w_qkv[:, 2 * nd:]]
            gains = [jnp.tile(od_q_norm_g[j], NA_HEADS)[None], jnp.tile(od_k_norm_g[j], NA_HEADS)[None], None]
            q_x, k_x, v_x = _nm_linear(x, g1, mod_x[:, 0:2], ws, [BF16] * 3, gains)
            q_c, k_c, v_c = [per_batch(t) for t in _nm_linear(ctx, g1, mod_c[:, 0:2], ws, [BF16] * 3, gains)]
            table = _rpb_table(od_rpb[j])
            o_x, o_c = _na_attention(q_x, k_x, v_x, q_c, k_c, v_c, table, want_ctx)
            w_outs = [od_w_o[j].astype(BF16)]
            acts_x = [o_x]
            acts_c = [o_c.reshape(1, bsz * sc, -1)] if want_ctx else None

        w_q = pe_w_q[layer].reshape(d, PEER_HEADS, 2 * PEER_KEYS).transpose(1, 0, 2).astype(BF16)
        keys = pe_keys[layer].reshape(2 * PEER_HEADS, PEER_KEYS, -1).astype(BF16)
        u = _pack_weight(pe_u, layer, transpose=False)
        v_t = _pack_weight(pe_v, layer, transpose=True)

        x, hx = _out_linear(acts_x, w_outs, x, mod_x[:, 2:5], g2)
        x = _peer(hx, x, mod_x[:, 5:6], w_q, keys, u, v_t)
        if want_ctx:
            ctx, hc = _out_linear(acts_c, w_outs, ctx, mod_c[:, 2:5], g2)
            ctx = _peer(hc, ctx, mod_c[:, 5:6], w_q, keys, u, v_t)
    return x
```

```python
import functools
import math

import jax
import jax.numpy as jnp
from jax import lax
from jax.experimental import pallas as pl
from jax.experimental.pallas import tpu as pltpu

F32 = jnp.float32
BF16 = jnp.bfloat16
HIGHEST = lax.Precision.HIGHEST

EPS = 1e-6
NEG_INF = -1e30

GRID_W = 64
SSD_HEADS = 16
SSD_HEAD_DIM = 64
SSD_GROUPS = 4
SSD_HPG = SSD_HEADS // SSD_GROUPS
SSD_STATE = 128
SSD_CHUNK = 128
SSD_INNER = SSD_HEADS * SSD_HEAD_DIM
SSD_GW = SSD_INNER // SSD_GROUPS
SSD_XBC = SSD_INNER + 2 * SSD_GROUPS * SSD_STATE
LRU_WIDTH = 1024
LRU_BLOCKS = 16
LRU_BLOCK_DIM = LRU_WIDTH // LRU_BLOCKS
LRU_C = 8.0
LRU_TILE = 256
NA_HEADS = 16
NA_HEAD_DIM = 64
NA_KH = 8
NA_KW = 16
NA_QROWS = 4
NA_KROWS = NA_KH + NA_QROWS
PEER_HEADS = 8
PEER_KEYS = 128
PEER_TOPK = 16
PEER_ICHUNK = 8

LANES = 128
SUBLANES = 8
VMEM_LIMIT_BYTES = 56 * 1024 * 1024


def _cparams(*sem):
    return pltpu.CompilerParams(dimension_semantics=sem, vmem_limit_bytes=VMEM_LIMIT_BYTES)


def _silu(x):
    return x * (1.0 / (1.0 + jnp.exp(-x)))


def _sigmoid(x):
    return 1.0 / (1.0 + jnp.exp(-x))


def _softplus(x):
    return jnp.maximum(x, 0.0) + jnp.log(1.0 + jnp.exp(-jnp.abs(x)))


def _gelu_tanh(x):
    k0 = -2.0 * math.sqrt(2.0 / math.pi)
    return x / (1.0 + jnp.exp(x * (k0 + (k0 * 0.044715) * (x * x))))


def _rms_mod(x, g, shift, scale):
    ms = jnp.mean(x * x, axis=-1, keepdims=True)
    y = x * lax.rsqrt(ms + EPS) * g
    return y * (1.0 + scale) + shift


def _dot(a, b):
    return jnp.dot(a, b, preferred_element_type=F32)


def _dot_nt(a, b):
    return lax.dot_general(a, b, (((1,), (1,)), ((), ())), preferred_element_type=F32)


def _ada_kernel(c_ref, w_ref, b_ref, o_ref):
    s = _silu(c_ref[...])
    o_ref[0] = jnp.dot(s, w_ref[0], preferred_element_type=F32, precision=HIGHEST) + b_ref[0]


def _ada_mods(c_all, ada_w, ada_b):
    depth, d, n = ada_w.shape
    rows = c_all.shape[0]
    tn = 1536
    return pl.pallas_call(
        _ada_kernel,
        grid=(depth, n // tn),
        in_specs=[pl.BlockSpec((rows, d), lambda l, j: (0, 0)),
                  pl.BlockSpec((1, d, tn), lambda l, j: (l, 0, j)),
                  pl.BlockSpec((1, 1, tn), lambda l, j: (l, 0, j))],
        out_specs=pl.BlockSpec((1, rows, tn), lambda l, j: (l, 0, j)),
        out_shape=jax.ShapeDtypeStruct((depth, rows, n), F32),
        compiler_params=_cparams("arbitrary", "arbitrary"),
        name="ada_mods",
    )(c_all, ada_w, ada_b.reshape(depth, 1, n))


def _head_block_ones(n):
    r = lax.broadcasted_iota(jnp.int32, (n, n), 0) // NA_HEAD_DIM
    c = lax.broadcasted_iota(jnp.int32, (n, n), 1) // NA_HEAD_DIM
    return (r == c).astype(F32)


def _nm_linear_kernel(*refs, n_out, head_norm, tn):
    x_ref, g_ref, mod_ref = refs[:3]
    w_refs = refs[3:3 + n_out]
    hg_refs = refs[3 + n_out:3 + n_out + sum(head_norm)]
    o_refs = refs[3 + n_out + sum(head_norm):]
    h = _rms_mod(x_ref[0], g_ref[...], mod_ref[0, 0:1, :], mod_ref[0, 1:2, :]).astype(BF16)
    hg_i = 0
    for w_ref, o_ref, hn in zip(w_refs, o_refs, head_norm):
        n = w_ref.shape[1]
        for j in range(n // tn):
            y = _dot(h, w_ref[:, j * tn:(j + 1) * tn])
            if hn:
                ss = jnp.dot(y * y, _head_block_ones(tn), preferred_element_type=F32, precision=HIGHEST)
                y = y * lax.rsqrt(ss * (1.0 / NA_HEAD_DIM) + EPS) * hg_refs[hg_i][:, j * tn:(j + 1) * tn]
            o_ref[0, :, j * tn:(j + 1) * tn] = y.astype(o_ref.dtype)
        hg_i += hn


def _nm_linear(x, g, mod, ws, out_dtypes, head_gains=None, tm=512, tn=256):
    bn, sn, d = x.shape
    tm = min(tm, sn)
    n_out = len(ws)
    head_gains = head_gains or [None] * n_out
    head_norm = tuple(hg is not None for hg in head_gains)
    hgs = [hg for hg in head_gains if hg is not None]
    in_specs = [pl.BlockSpec((1, tm, d), lambda b, i: (b, i, 0)),
                pl.BlockSpec((1, d), lambda b, i: (0, 0)),
                pl.BlockSpec((1, 2, d), lambda b, i: (b, 0, 0))]
    in_specs += [pl.BlockSpec(w.shape, lambda b, i: (0, 0)) for w in ws]
    in_specs += [pl.BlockSpec(hg.shape, lambda b, i: (0, 0)) for hg in hgs]
    out_specs = [pl.BlockSpec((1, tm, w.shape[1]), lambda b, i: (b, i, 0)) for w in ws]
    out_shape = [jax.ShapeDtypeStruct((bn, sn, w.shape[1]), dt) for w, dt in zip(ws, out_dtypes)]
    return pl.pallas_call(
        functools.partial(_nm_linear_kernel, n_out=n_out, head_norm=head_norm, tn=tn),
        grid=(bn, sn // tm),
        in_specs=in_specs, out_specs=out_specs, out_shape=out_shape,
        compiler_params=_cparams("arbitrary", "arbitrary"),
        name="nm_linear",
    )(x, g, mod, *ws, *hgs)


def _out_linear_kernel(*refs, n_in):
    a_refs = refs[:n_in]
    w_refs = refs[n_in:2 * n_in]
    x_ref, mod_ref, g_ref, xo_ref, ho_ref = refs[2 * n_in:]
    y = _dot(a_refs[0][0], w_refs[0][...])
    for a_ref, w_ref in zip(a_refs[1:], w_refs[1:]):
        y = y + _dot(a_ref[0], w_ref[...])
    xn = x_ref[0] + mod_ref[0, 0:1, :] * y
    xo_ref[0] = xn
    ho_ref[0] = _rms_mod(xn, g_ref[...], mod_ref[0, 1:2, :], mod_ref[0, 2:3, :]).astype(BF16)


def _out_linear(acts, ws, x, mod, g, tm=512):
    bn, sn, d = x.shape
    tm = min(tm, sn)
    n_in = len(acts)
    in_specs = [pl.BlockSpec((1, tm, a.shape[2]), lambda b, i: (b, i, 0)) for a in acts]
    in_specs += [pl.BlockSpec(w.shape, lambda b, i: (0, 0)) for w in ws]
    in_specs += [pl.BlockSpec((1, tm, d), lambda b, i: (b, i, 0)),
                 pl.BlockSpec((1, 3, d), lambda b, i: (b, 0, 0)),
                 pl.BlockSpec((1, d), lambda b, i: (0, 0))]
    return pl.pallas_call(
        functools.partial(_out_linear_kernel, n_in=n_in),
        grid=(bn, sn // tm),
        in_specs=in_specs,
        out_specs=[pl.BlockSpec((1, tm, d), lambda b, i: (b, i, 0))] * 2,
        out_shape=[jax.ShapeDtypeStruct((bn, sn, d), F32), jax.ShapeDtypeStruct((bn, sn, d), BF16)],
        compiler_params=_cparams("arbitrary", "arbitrary"),
        name="out_linear",
    )(*acts, *ws, x, mod, g)


CONV_HALO = 16


def _conv_chunk(src_ref, s, seg_len, w, bias, rows=SSD_CHUNK):
    ncol = src_ref.shape[2]
    if s > 0:
        prev = src_ref[0, s - CONV_HALO:s, :].astype(F32)
    else:
        prev = jnp.zeros((CONV_HALO, ncol), F32)
    cur = src_ref[0, s:s + rows, :].astype(F32)
    if s + rows < seg_len:
        nxt = src_ref[0, s + rows:s + rows + CONV_HALO, :].astype(F32)
    else:
        nxt = jnp.zeros((CONV_HALO, ncol), F32)
    win = jnp.concatenate([prev, cur, nxt], axis=0)
    taps = w.shape[0]
    acc = bias
    for k in range(taps):
        off = CONV_HALO - taps // 2 + k
        acc = acc + win[off:off + rows, :] * w[k:k + 1, :]
    return acc


def _lane_head_expand(cols, width):
    nh = len(cols)
    hd = width // nh
    rows = cols[0].shape[0]
    lane_head = lax.broadcasted_iota(jnp.int32, (rows, width), 1) // hd
    out = jnp.broadcast_to(cols[nh - 1], (rows, width))
    for h in range(nh - 2, -1, -1):
        out = jnp.where(lane_head == h, jnp.broadcast_to(cols[h], (rows, width)), out)
    return out


def _ssd_kernel(xs_x, bm_x, cm_x, xs_c, bm_c, cm_c, dt_x, dt_c, z_x, z_c,
                cw_xs, cw_b, cw_c, cb_xs, cb_b, cb_c, alog_ref, dtb_ref, dsk_ref, ng_ref,
                y_x, y_c,
                xs_s, bm_s, cm_s, dt_s, y_s, st_s, cs_s, cst_s, *, sx, sc):
    q = SSD_CHUNK
    nc_c, nc_x = sc // q, sx // q
    nc = nc_c + nc_x
    gw = xs_s.shape[1]

    for seg_ref3, seg_len, base in (((xs_c, bm_c, cm_c), sc, 0), ((xs_x, bm_x, cm_x), sx, sc)):
        for ci in range(seg_len // q):
            s = ci * q
            for src, dst, w_ref, b_ref in zip(seg_ref3, (xs_s, bm_s, cm_s), (cw_xs, cw_b, cw_c),
                                              (cb_xs, cb_b, cb_c)):
                dst[base + s:base + s + q, :] = _silu(_conv_chunk(src, s, seg_len, w_ref[...], b_ref[...]))
    dt_s[0:sc, :] = _softplus(dt_c[0] + dtb_ref[0])
    dt_s[sc:sc + sx, :] = _softplus(dt_x[0] + dtb_ref[0])

    a_neg = -jnp.exp(alog_ref[0])
    row = lax.broadcasted_iota(jnp.int32, (q, q), 0)
    col = lax.broadcasted_iota(jnp.int32, (q, q), 1)
    tri = ((col <= row).astype(F32), (col >= row).astype(F32))
    keep = (col <= row, col >= row)
    lane_head = lax.broadcasted_iota(jnp.int32, (q, gw), 1) // SSD_HEAD_DIM

    st_s[...] = jnp.zeros_like(st_s)
    y_s[...] = jnp.zeros_like(y_s)

    for ci in range(nc):
        la = dt_s[ci * q:(ci + 1) * q, :] * a_neg
        for d in range(2):
            cs = jnp.dot(tri[d], la, preferred_element_type=F32, precision=HIGHEST)
            cs_s[d, ci * q:(ci + 1) * q, :] = cs
            cst_s[d, ci * q:(ci + 1) * q, :] = cs.T

    def chunk_body(i, carry):
        for d in range(2):
            if d == 0:
                ci = i
            else:
                ci = jnp.where(i < nc_c, nc_c - 1 - i, nc + nc_c - 1 - i)
            r0 = pl.multiple_of(ci * q, q)
            xs = xs_s[pl.ds(r0, q), :]
            bm = bm_s[pl.ds(r0, q), :]
            cm = cm_s[pl.ds(r0, q), :]
            dt = dt_s[pl.ds(r0, q), :]
            cs = cs_s[d, pl.ds(r0, q), :]
            cs_t = cst_s[d, pl.ds(r0, q), :]
            cb = _dot_nt(cm.astype(BF16), bm.astype(BF16))
            heads = [d * SSD_HPG + h for h in range(SSD_HPG)]
            dt_mat = _lane_head_expand([dt[:, c:c + 1] for c in heads], gw)
            cs_mat = _lane_head_expand([cs[:, c:c + 1] for c in heads], gw)
            xd = xs * dt_mat
            xd_b = xd.astype(BF16)
            y = jnp.zeros((q, gw), F32)
            for h, c in enumerate(heads):
                diff = cs[:, c:c + 1] - cs_t[c:c + 1, :]
                lmat = jnp.exp(jnp.where(keep[d], diff, NEG_INF))
                y = jnp.where(lane_head == h, _dot((cb * lmat).astype(BF16), xd_b), y)
            st = st_s[d]
            y = y + _dot(cm.astype(BF16), st.astype(BF16)) * jnp.exp(cs_mat)
            end = q - 1 if d == 0 else 0
            cs_end = cs_mat[end:end + 1, :]
            s_new = _dot(bm.T.astype(BF16), (xd * jnp.exp(cs_end - cs_mat)).astype(BF16))
            st_s[d] = st * jnp.exp(cs_end) + s_new
            y_s[pl.ds(r0, q), :] = y_s[pl.ds(r0, q), :] + y
        return carry

    lax.fori_loop(0, nc, chunk_body, 0)

    for ci in range(nc):
        s = ci * q
        if ci < nc_c:
            z = z_c[0, s:s + q, :]
        else:
            z = z_x[0, s - sc:s - sc + q, :]
        y = (y_s[s:s + q, :] + dsk_ref[0] * xs_s[s:s + q, :]) * _silu(z.astype(F32))
        ms = jnp.mean(y * y, axis=-1, keepdims=True)
        out = (y * lax.rsqrt(ms + EPS) * ng_ref[0]).astype(BF16)
        if ci < nc_c:
            y_c[0, s:s + q, :] = out
        else:
            y_x[0, s - sc:s - sc + q, :] = out


def _ssd_mixer(xbc_x, xbc_c, dt_x, dt_c, z_x, z_c, conv_w, conv_b, alog_g, dtb_g, dsk_g, ng_g):
    bsz, sx, _ = xbc_x.shape
    sc = xbc_c.shape[1]
    g, gw, n = SSD_GROUPS, SSD_GW, SSD_STATE
    nb = SSD_INNER // n
    taps = conv_w.shape[0]

    def seq(s, w, off):
        return pl.BlockSpec((1, s, w), lambda b, j, off=off: (b, 0, off + j))

    def par(r, w, off):
        return pl.BlockSpec((r, w), lambda b, j, off=off: (0, off + j))

    def grp(w):
        return pl.BlockSpec((1, 1, w), lambda b, j: (j, 0, 0))

    in_specs = [seq(sx, gw, 0), seq(sx, n, nb), seq(sx, n, nb + g),
                seq(sc, gw, 0), seq(sc, n, nb), seq(sc, n, nb + g),
                seq(sx, LANES, 0), seq(sc, LANES, 0), seq(sx, gw, 0), seq(sc, gw, 0),
                par(taps, gw, 0), par(taps, n, nb), par(taps, n, nb + g),
                par(1, gw, 0), par(1, n, nb), par(1, n, nb + g),
                grp(LANES), grp(LANES), grp(gw), grp(gw)]
    stot = sx + sc
    return pl.pallas_call(
        functools.partial(_ssd_kernel, sx=sx, sc=sc),
        grid=(bsz, g),
        in_specs=in_specs,
        out_specs=[seq(sx, gw, 0), seq(sc, gw, 0)],
        out_shape=[jax.ShapeDtypeStruct((bsz, sx, SSD_INNER), BF16),
                   jax.ShapeDtypeStruct((bsz, sc, SSD_INNER), BF16)],
        scratch_shapes=[pltpu.VMEM((stot, gw), F32), pltpu.VMEM((stot, n), F32), pltpu.VMEM((stot, n), F32),
                        pltpu.VMEM((stot, LANES), F32), pltpu.VMEM((stot, gw), F32), pltpu.VMEM((2, n, gw), F32),
                        pltpu.VMEM((2, stot, LANES), F32), pltpu.VMEM((2, stot, LANES), F32)],
        compiler_params=_cparams("arbitrary", "arbitrary"),
        name="ssd_mixer",
    )(xbc_x, xbc_x, xbc_x, xbc_c, xbc_c, xbc_c, dt_x, dt_c, z_x, z_c,
      conv_w, conv_w, conv_w, conv_b, conv_b, conv_b, alog_g, dtb_g, dsk_g, ng_g)


def _lru_kernel(xl_x, xl_c, gt_x, gt_c, cw, cb, wa, wx, ba, bx, lam, y_x, y_c,
                xr_s, a_s, b_s, y_s, *, sx, sc):
    q = SSD_CHUNK
    stot = sx + sc
    w = xr_s.shape[1]
    for src, seg_len, base in ((xl_c, sc, 0), (xl_x, sx, sc)):
        for ci in range(seg_len // q):
            s = ci * q
            xr_s[base + s:base + s + q, :] = _conv_chunk(src, s, seg_len, cw[...], cb[...])

    ng = stot // SUBLANES
    ng_c = sc // SUBLANES
    sub = lax.broadcasted_iota(jnp.int32, (SUBLANES, w), 0)
    rt = q
    for d in range(2):
        nsp = _softplus(-lam[d:d + 1, :])
        for ci in range(stot // rt):
            s = ci * rt
            xr = xr_s[s:s + rt, :]
            xb = xr.astype(BF16)
            r = _sigmoid(_dot(xb, wa[d, 0]) + ba[d:d + 1, :])
            ig = _sigmoid(_dot(xb, wx[d, 0]) + bx[d:d + 1, :])
            a = jnp.exp(-LRU_C * r * nsp)
            a_s[d, s:s + rt, :] = a
            b_s[d, s:s + rt, :] = jnp.sqrt(1.0 - a * a) * (ig * xr)

    y_s[...] = jnp.zeros_like(y_s)

    def group_body(k, carries):
        new = []
        for d in range(2):
            if d == 0:
                gi = k
            else:
                gi = jnp.where(k < ng_c, ng_c - 1 - k, ng + ng_c - 1 - k)
            r0 = pl.multiple_of(gi * SUBLANES, SUBLANES)
            a = a_s[d, pl.ds(r0, SUBLANES), :]
            b = b_s[d, pl.ds(r0, SUBLANES), :]
            for sh in (1, 2, 4):
                if d == 0:
                    valid = sub >= sh
                    a_sh = pltpu.roll(a, sh, axis=0)
                    b_sh = pltpu.roll(b, sh, axis=0)
                else:
                    valid = sub < SUBLANES - sh
                    a_sh = pltpu.roll(a, SUBLANES - sh, axis=0)
                    b_sh = pltpu.roll(b, SUBLANES - sh, axis=0)
                b = jnp.where(valid, a * b_sh + b, b)
                a = jnp.where(valid, a * a_sh, a)
            h = a * carries[d] + b
            y_s[pl.ds(r0, SUBLANES), :] = y_s[pl.ds(r0, SUBLANES), :] + h
            last = h[SUBLANES - 1:SUBLANES, :] if d == 0 else h[0:1, :]
            new.append(jnp.broadcast_to(last, (SUBLANES, w)))
        return tuple(new)

    zero = jnp.zeros((SUBLANES, w), F32)
    lax.fori_loop(0, ng, group_body, (zero, zero))

    for ci in range(stot // rt):
        s = ci * rt
        if s < sc:
            gate = gt_c[0, s:s + rt, :]
        else:
            gate = gt_x[0, s - sc:s - sc + rt, :]
        out = (y_s[s:s + rt, :] * _gelu_tanh(gate.astype(F32))).astype(BF16)
        if s < sc:
            y_c[0, s:s + rt, :] = out
        else:
            y_x[0, s - sc:s - sc + rt, :] = out


def _lru_mixer(xl_x, xl_c, gt_x, gt_c, conv_w, conv_b, wa_bd, wx_bd, ba, bx, lam):
    bsz, sx, width = xl_x.shape
    sc = xl_c.shape[1]
    w = LRU_TILE
    taps = conv_w.shape[0]

    def seq(s):
        return pl.BlockSpec((1, s, w), lambda b, j: (b, 0, j))

    def par(r):
        return pl.BlockSpec((r, w), lambda b, j: (0, j))

    wspec = pl.BlockSpec((2, 1, w, w), lambda b, j: (0, j, 0, 0))
    stot = sx + sc
    return pl.pallas_call(
        functools.partial(_lru_kernel, sx=sx, sc=sc),
        grid=(bsz, width // w),
        in_specs=[seq(sx), seq(sc), seq(sx), seq(sc), par(taps), par(1), wspec, wspec, par(2), par(2), par(2)],
        out_specs=[seq(sx), seq(sc)],
        out_shape=[jax.ShapeDtypeStruct((bsz, sx, width), BF16), jax.ShapeDtypeStruct((bsz, sc, width), BF16)],
        scratch_shapes=[pltpu.VMEM((stot, w), F32), pltpu.VMEM((2, stot, w), F32), pltpu.VMEM((2, stot, w), F32),
                        pltpu.VMEM((stot, w), F32)],
        compiler_params=_cparams("arbitrary", "arbitrary"),
        name="lru_mixer",
    )(xl_x, xl_c, gt_x, gt_c, conv_w, conv_b, wa_bd, wx_bd, ba, bx, lam)


def _rpb_table_kernel(rpb_ref, o_ref, *, n_dr, n_dc):
    h = pl.program_id(0)
    w = GRID_W
    qcol = lax.broadcasted_iota(jnp.int32, (w, 2 * w), 0)
    lane = lax.broadcasted_iota(jnp.int32, (w, 2 * w), 1)
    kcol = lane % w
    hi = lane >= w
    rel = kcol - qcol + (NA_KW - 1)
    cstart = jnp.clip(qcol - NA_KW // 2, 0, w - NA_KW)
    in_win = (kcol >= cstart) & (kcol < cstart + NA_KW)
    for d in range(n_dr + 1):
        acc = jnp.full((w, 2 * w), NEG_INF, F32)
        for dc in range(n_dc):
            lo = rpb_ref[(h * n_dr + d - 1) * n_dc + dc] if d >= 1 else NEG_INF
            up = rpb_ref[(h * n_dr + d) * n_dc + dc] if d < n_dr else NEG_INF
            acc = jnp.where(rel == dc, jnp.where(hi, up, lo), acc)
        valid = in_win
        if d == 0:
            valid = valid & hi
        if d == n_dr:
            valid = valid & jnp.logical_not(hi)
        o_ref[0, d] = jnp.where(valid, acc, NEG_INF)


def _rpb_table(rpb):
    nh, n_dr, n_dc = rpb.shape
    return pl.pallas_call(
        functools.partial(_rpb_table_kernel, n_dr=n_dr, n_dc=n_dc),
        grid=(nh,),
        in_specs=[pl.BlockSpec(memory_space=pltpu.SMEM)],
        out_specs=pl.BlockSpec((1, n_dr + 1, GRID_W, 2 * GRID_W), lambda h: (h, 0, 0, 0)),
        out_shape=jax.ShapeDtypeStruct((nh, n_dr + 1, GRID_W, 2 * GRID_W), F32),
        compiler_params=_cparams("arbitrary"),
        name="rpb_table",
    )(rpb.reshape(-1))


def _na_kernel(*refs, sx, sc, want_ctx):
    if want_ctx:
        q_x, k_x, v_x, q_c, k_c, v_c, tab, o_x, o_c = refs
    else:
        q_x, k_x, v_x, k_c, v_c, tab, o_x = refs
    w = GRID_W
    rows = sx // w
    qb = NA_QROWS * w
    kb = NA_KROWS * w
    n_blk = rows // NA_QROWS
    scale = NA_HEAD_DIM ** -0.5
    lane = lax.broadcasted_iota(jnp.int32, (1, 2 * NA_HEAD_DIM), 1)
    in_head = (lane < NA_HEAD_DIM, lane >= NA_HEAD_DIM)
    keyrow = lax.broadcasted_iota(jnp.int32, (1, kb), 1) // w
    kc = k_c[0]
    vc = v_c[0]
    n_tab = tab.shape[1]

    def softmax_pv(parts):
        m = parts[0][0].max(axis=-1, keepdims=True)
        for s, _ in parts[1:]:
            m = jnp.maximum(m, s.max(axis=-1, keepdims=True))
        acc, den = None, None
        for s, v in parts:
            e = jnp.exp(s - m)
            den = e.sum(axis=-1, keepdims=True) if den is None else den + e.sum(axis=-1, keepdims=True)
            pv = _dot(e.astype(BF16), v)
            acc = pv if acc is None else acc + pv
        return acc / den

    def block_body(rb, carry):
        ws = jnp.clip(NA_QROWS * rb - NA_KH // 2, 0, rows - NA_KROWS)
        q0 = pl.multiple_of(rb * qb, qb)
        k0 = pl.multiple_of(ws * w, w)
        qblk = q_x[0, pl.ds(q0, qb), :]
        kwin = k_x[0, pl.ds(k0, kb), :]
        vwin = v_x[0, pl.ds(k0, kb), :]
        out = jnp.zeros((qb, 2 * NA_HEAD_DIM), F32)
        for hh in range(2):
            qm = jnp.where(in_head[hh], qblk * scale, jnp.zeros_like(qblk))
            s_loc = _dot_nt(qm, kwin)
            s_ctx = _dot_nt(qm, kc)
            pieces = []
            for rq in range(NA_QROWS):
                r = NA_QROWS * rb + rq
                rs = jnp.clip(r - NA_KH // 2, 0, rows - NA_KH)
                lo = rs - ws
                valid = (keyrow >= lo) & (keyrow < lo + NA_KH)
                blocks = []
                for ip in range(NA_KROWS // 2):
                    dr_lo = ws + 2 * ip - r + NA_KH - 1
                    blocks.append(tab[hh, jnp.clip(dr_lo + 1, 0, n_tab - 1)])
                bias = jnp.concatenate(blocks, axis=1)
                piece = s_loc[rq * w:(rq + 1) * w, :] + bias
                pieces.append(jnp.where(valid, piece, NEG_INF))
            s_loc = jnp.concatenate(pieces, axis=0)
            o = softmax_pv([(s_loc, vwin), (s_ctx, vc)])
            out = jnp.where(in_head[hh], o, out)
        o_x[0, pl.ds(q0, qb), :] = out.astype(o_x.dtype)
        return carry

    lax.fori_loop(0, n_blk, block_body, 0)

    if want_ctx:
        qc = q_c[0]
        out = jnp.zeros((sc, 2 * NA_HEAD_DIM), F32)
        for hh in range(2):
            qm = jnp.where(in_head[hh], qc * scale, jnp.zeros_like(qc))
            o = softmax_pv([(_dot_nt(qm, kc), vc)])
            out = jnp.where(in_head[hh], o, out)
        o_c[0] = out.astype(o_c.dtype)


def _na_attention(q_x, k_x, v_x, q_c, k_c, v_c, table, want_ctx):
    bsz, sx, dim = q_x.shape
    sc = k_c.shape[1]
    pw = 2 * NA_HEAD_DIM
    n_pair = dim // pw

    def seq(s):
        return pl.BlockSpec((1, s, pw), lambda p, b: (b, 0, p))

    tspec = pl.BlockSpec((2,) + table.shape[1:], lambda p, b: (p, 0, 0, 0))
    if want_ctx:
        args = (q_x, k_x, v_x, q_c, k_c, v_c, table)
        in_specs = [seq(sx)] * 3 + [seq(sc)] * 3 + [tspec]
        out_specs = [seq(sx), seq(sc)]
        out_shape = [jax.ShapeDtypeStruct((bsz, sx, dim), BF16), jax.ShapeDtypeStruct((bsz, sc, dim), BF16)]
    else:
        args = (q_x, k_x, v_x, k_c, v_c, table)
        in_specs = [seq(sx)] * 3 + [seq(sc)] * 2 + [tspec]
        out_specs = [seq(sx)]
        out_shape = [jax.ShapeDtypeStruct((bsz, sx, dim), BF16)]
    res = pl.pallas_call(
        functools.partial(_na_kernel, sx=sx, sc=sc, want_ctx=want_ctx),
        grid=(n_pair, bsz),
        in_specs=in_specs, out_specs=out_specs, out_shape=out_shape,
        compiler_params=_cparams("arbitrary", "arbitrary"),
        name="na_attention",
    )(*args)
    return res if want_ctx else (res[0], None)


def _top_rows(s, k, exact, want_rank=True):
    n, tt = s.shape
    top_id = lax.broadcasted_iota(jnp.int32, (k, tt), 0)
    if exact:
        rowid = lax.broadcasted_iota(jnp.int32, (n, tt), 0).astype(F32)
    work = s
    top = jnp.zeros((k, tt), F32)
    rank = jnp.full((n, tt), float(k), F32) if want_rank else None
    for it in range(k):
        m = jnp.max(work, axis=0, keepdims=True)
        sel = work == m
        if exact:
            sel = rowid == jnp.min(jnp.where(sel, rowid, float(n)), axis=0, keepdims=True)
        top = jnp.where(top_id == it, m, top)
        if want_rank:
            rank = jnp.where(sel, float(it), rank)
        work = jnp.where(sel, -jnp.inf, work)
    picked = work == -jnp.inf
    n_sel = jnp.sum(jnp.where(picked, 1.0, 0.0), axis=0, keepdims=True)
    return top, rank, picked, n_sel


def _peer_route_kernel(h_ref, wq_ref, keys_ref, cnt_ref, rk_ref, e0_ref, e1_ref, s_s, top_s, rank_s, cnti_s, z_s):
    nk, k, ic = PEER_KEYS, PEER_TOPK, PEER_ICHUNK
    tt = h_ref.shape[1]
    sw = top_s.shape[2]
    n_strip = tt // sw
    n_unit = PEER_HEADS * n_strip

    def scores(u):
        h, si = u // n_strip, u % n_strip
        rows = pl.ds(pl.multiple_of(si * sw, sw), sw)
        q = _dot(h_ref[0, rows, :], wq_ref[h]).astype(BF16)
        for z in range(2):
            s_s[u % 2, z] = _dot_nt(keys_ref[2 * h + z], q[:, z * nk:(z + 1) * nk])

    def pair_stage(t0, t1, exact):
        cand = jnp.concatenate([t0[0:1] + t1] + [t0[a:a + 1] + t1[0:8] for a in range(1, 8)]
                               + [t0[8:16] + t1[0:1]], axis=0)
        _, _, picked, n_sel = _top_rows(cand, k, exact, want_rank=False)
        pf = jnp.where(picked, 1.0, 0.0)
        z_sum = jnp.sum(pf * jnp.exp(cand - cand[0:1]), axis=0, keepdims=True)
        cnts = [jnp.sum(pf[0:k], axis=0, keepdims=True)]
        cnts += [jnp.sum(pf[k + 8 * (a - 1):k + 8 * a], axis=0, keepdims=True) for a in range(1, 8)]
        cnts += [pf[k + 56 + a:k + 57 + a] for a in range(8)]
        return cnts, z_sum, n_sel

    def route(u, prefetch):
        h, si = u // n_strip, u % n_strip
        cols = pl.ds(pl.multiple_of(si * sw, sw), sw)
        s0, s1 = s_s[u % 2, 0], s_s[u % 2, 1]
        if prefetch:
            scores(u + 1)

        t0, _, _, n0 = _top_rows(s0, k, False, want_rank=False)
        t1, rank1, _, n1 = _top_rows(s1, k, False)
        cnts, z_sum, n2 = pair_stage(t0, t1, False)
        cnt_i = jnp.zeros_like(s0)
        for a in range(k):
            cnt_i = jnp.where(s0 == t0[a:a + 1], cnts[a], cnt_i)
        top_s[0], top_s[1] = t0, t1
        rank_s[...] = rank1
        cnti_s[...] = cnt_i
        z_s[...] = z_sum
        ties = jnp.max(jnp.abs(n0 - float(k)) + jnp.abs(n1 - float(k)) + jnp.abs(n2 - float(k)))

        @pl.when(ties > 0.5)
        def _():
            t0, rank0, _, _ = _top_rows(s0, k, True)
            t1, rank1, _, _ = _top_rows(s1, k, True)
            cnts, z_sum, _ = pair_stage(t0, t1, True)
            cnt_i = jnp.zeros_like(s0)
            for a in range(k):
                cnt_i = jnp.where(rank0 == float(a), cnts[a], cnt_i)
            top_s[0], top_s[1] = t0, t1
            rank_s[...] = rank1
            cnti_s[...] = cnt_i
            z_s[...] = z_sum

        cnt_i = cnti_s[...]
        e0 = jnp.exp(s0 - top_s[0, 0:1]) / z_s[...]
        hi = pl.ds(pl.multiple_of(h * ic, ic), ic)
        hj = pl.ds(pl.multiple_of(h * nk, nk), nk)
        for c in range(nk // ic):
            cnt_ref[0, c, hi, cols] = cnt_i[c * ic:(c + 1) * ic]
            e0_ref[0, c, hi, cols] = e0[c * ic:(c + 1) * ic]
        rk_ref[0, hj, cols] = rank_s[...].astype(rk_ref.dtype)
        e1_ref[0, hj, cols] = jnp.exp(s1 - top_s[1, 0:1]).astype(e1_ref.dtype)

    scores(jnp.int32(0))

    def unit(u, carry):
        route(u, True)
        return carry

    lax.fori_loop(0, n_unit - 1, unit, 0)
    route(jnp.int32(n_unit - 1), False)


ROUTE_STRIP = 256


def _peer_route(hx, w_q, keys, tt):
    bn, sn, d = hx.shape
    nk, nh, ic = PEER_KEYS, PEER_HEADS, PEER_ICHUNK
    spec_i = pl.BlockSpec((1, nk // ic, nh * ic, tt), lambda b, i: (b, 0, 0, i))
    spec_j = pl.BlockSpec((1, nh * nk, tt), lambda b, i: (b, 0, i))
    shape_i = jax.ShapeDtypeStruct((bn, nk // ic, nh * ic, sn), F32)
    shape_j = jax.ShapeDtypeStruct((bn, nh * nk, sn), BF16)
    sw = min(ROUTE_STRIP, tt)
    return pl.pallas_call(
        _peer_route_kernel,
        grid=(bn, sn // tt),
        in_specs=[pl.BlockSpec((1, tt, d), lambda b, i: (b, i, 0)),
                  pl.BlockSpec(w_q.shape, lambda b, i: (0, 0, 0)),
                  pl.BlockSpec(keys.shape, lambda b, i: (0, 0, 0))],
        out_specs=[spec_i, spec_j, spec_i, spec_j],
        out_shape=[shape_i, shape_j, shape_i, shape_j],
        scratch_shapes=[pltpu.VMEM((2, 2, nk, sw), F32), pltpu.VMEM((2, PEER_TOPK, sw), F32),
                        pltpu.VMEM((nk, sw), F32), pltpu.VMEM((nk, sw), F32), pltpu.VMEM((1, sw), F32)],
        compiler_params=_cparams("arbitrary", "arbitrary"),
        name="peer_route",
    )(hx, w_q, keys)


def _peer_dense_kernel(h_ref, cnt_ref, rk_ref, e0_ref, e1_ref, u_ref, vt_ref, x_ref, g_ref, o_ref,
                       acc_ref, act_ref, p_ref, hx_s, rk_s, e1_s, cnt_s, e0_s):
    nk, ic = PEER_KEYS, PEER_ICHUNK
    ck = pl.program_id(2)
    n_slab = act_ref.shape[0]
    sw = min(PEER_STRIP, n_slab * LANES)
    per = sw // LANES
    n_strip = n_slab // per
    il_group, j_group = 4, 2
    pk = _rows_per_word(BF16)

    @pl.when(ck == 0)
    def _():
        acc_ref[...] = jnp.zeros_like(acc_ref)
        hx_s[...] = h_ref[0]
        for t in range(n_slab):
            rk_s[t] = _to_words(rk_ref[0, :, t * LANES:(t + 1) * LANES])
            e1_s[t] = _to_words(e1_ref[0, :, t * LANES:(t + 1) * LANES])

    for t in range(n_slab):
        cnt_s[t] = cnt_ref[0, 0, :, t * LANES:(t + 1) * LANES]
        e0_s[t] = e0_ref[0, 0, :, t * LANES:(t + 1) * LANES]

    def activations(s):
        r0 = pl.multiple_of(s * sw, sw)
        a = _gelu_tanh(_dot_nt(_from_words(u_ref[...], BF16), hx_s[pl.ds(r0, sw), :])).astype(BF16)
        for k in range(per):
            act_ref[s * per + k] = _to_words(a[:, k * LANES:(k + 1) * LANES])

    def gate_weights(t, ig):
        jr = nk // j_group
        wgt = [[None] * il_group for _ in range(j_group)]
        for h in range(PEER_HEADS):
            cnt8 = cnt_s[t, h * ic:(h + 1) * ic, :]
            e08 = e0_s[t, h * ic:(h + 1) * ic, :]
            rows = []
            for g in range(il_group):
                il = ig * il_group + g
                rows.append((jnp.broadcast_to(cnt8[il:il + 1], (jr, LANES)).astype(BF16),
                             jnp.broadcast_to(e08[il:il + 1], (jr, LANES)).astype(BF16)))
            for jg in range(j_group):
                j0 = (h * nk + jg * jr) // pk
                rk = _from_words(rk_s[t, j0:j0 + jr // pk, :], BF16)
                e1 = _from_words(e1_s[t, j0:j0 + jr // pk, :], BF16)
                for g in range(il_group):
                    term = jnp.where(rk < rows[g][0], e1 * rows[g][1], jnp.zeros_like(e1))
                    wgt[jg][g] = term if wgt[jg][g] is None else wgt[jg][g] + term
        for jg in range(j_group):
            for g in range(il_group):
                r0 = ((ig * il_group + g) * nk + jg * jr) // pk
                act = _from_words(act_ref[t, r0:r0 + jr // pk, :], BF16)
                p_ref[t, r0:r0 + jr // pk, :] = _to_words(wgt[jg][g] * act)

    def combine(s):
        for k in range(per):
            for ig in range(ic // il_group):
                gate_weights(s * per + k, ig)
        p = _from_words(jnp.concatenate([p_ref[s * per + k] for k in range(per)], axis=1), BF16)
        y = _dot(_from_words(vt_ref[...], BF16), p)
        for k in range(per):
            acc_ref[s * per + k] += y[:, k * LANES:(k + 1) * LANES]

    activations(0)

    def strip(s, carry):
        activations(s + 1)
        combine(s)
        return carry

    lax.fori_loop(0, n_strip - 1, strip, 0)
    combine(n_strip - 1)

    @pl.when(ck == pl.num_programs(2) - 1)
    def _():
        for t in range(n_slab):
            rows = slice(t * LANES, (t + 1) * LANES)
            o_ref[0, rows, :] = x_ref[0, rows, :] + g_ref[0] * acc_ref[t].T


PEER_STRIP = 256


def _rows_per_word(dt):
    return 4 // jnp.dtype(dt).itemsize


def _to_words(x):
    return pltpu.bitcast(x, jnp.uint32) if x.dtype.itemsize == 2 else x


def _from_words(x, dt):
    return pltpu.bitcast(x, dt) if jnp.dtype(dt).itemsize == 2 else x


def _pack_weight_kernel(w_ref, o_ref, *, transpose):
    w = w_ref[0].T if transpose else w_ref[0]
    o_ref[...] = _to_words(w.astype(BF16))


def _pack_weight(w, layer, transpose):
    _, rows, cols = w.shape
    pk = _rows_per_word(BF16)
    blk = 1024
    if transpose:
        out_shape, out_spec = (cols // pk, rows), pl.BlockSpec((cols // pk, blk), lambda i: (0, i))
    else:
        out_shape, out_spec = (rows // pk, cols), pl.BlockSpec((blk // pk, cols), lambda i: (i, 0))
    return pl.pallas_call(
        functools.partial(_pack_weight_kernel, transpose=transpose),
        grid=(rows // blk,),
        in_specs=[pl.BlockSpec((1, blk, cols), lambda i: (layer, i, 0))],
        out_specs=out_spec,
        out_shape=jax.ShapeDtypeStruct(out_shape, jnp.uint32 if pk == 2 else BF16),
        compiler_params=_cparams("arbitrary"),
        name="pack_weight",
    )(w)


def _peer_dense(hx, route, u, v_t, x, gate, tt):
    bn, sn, d = hx.shape
    nk, nh, ic = PEER_KEYS, PEER_HEADS, PEER_ICHUNK
    ne = ic * nk
    pk = _rows_per_word(BF16)
    wdt = jnp.uint32 if pk == 2 else BF16
    n_chunk = u.shape[0] * pk // ne
    spec_i = pl.BlockSpec((1, 1, nh * ic, tt), lambda b, i, c: (b, c, 0, i))
    spec_j = pl.BlockSpec((1, nh * nk, tt), lambda b, i, c: (b, 0, i))
    tok = pl.BlockSpec((1, tt, d), lambda b, i, c: (b, i, 0))
    n_slab = tt // LANES
    return pl.pallas_call(
        _peer_dense_kernel,
        grid=(bn, sn // tt, n_chunk),
        in_specs=[tok, spec_i, spec_j, spec_i, spec_j,
                  pl.BlockSpec((ne // pk, d), lambda b, i, c: (c, 0)),
                  pl.BlockSpec((d // pk, ne), lambda b, i, c: (0, c)),
                  tok,
                  pl.BlockSpec((1, 1, d), lambda b, i, c: (b, 0, 0))],
        out_specs=tok,
        out_shape=jax.ShapeDtypeStruct((bn, sn, d), F32),
        scratch_shapes=[pltpu.VMEM((n_slab, d, LANES), F32), pltpu.VMEM((n_slab, ne // pk, LANES), wdt),
                        pltpu.VMEM((n_slab, ne // pk, LANES), wdt), pltpu.VMEM((tt, d), BF16),
                        pltpu.VMEM((n_slab, nh * nk // pk, LANES), wdt),
                        pltpu.VMEM((n_slab, nh * nk // pk, LANES), wdt),
                        pltpu.VMEM((n_slab, nh * ic, LANES), F32), pltpu.VMEM((n_slab, nh * ic, LANES), F32)],
        compiler_params=_cparams("arbitrary", "arbitrary", "arbitrary"),
        name="peer_dense",
    )(hx, *route, u, v_t, x, gate)


def _peer(hx, x, gate, w_q, keys, u, v_t, tt=1024):
    tt = min(tt, hx.shape[1])
    route = _peer_route(hx, w_q, keys, tt)
    return _peer_dense(hx, route, u, v_t, x, gate, tt)


def _group_lanes(p, width):
    g = p.reshape(2, SSD_GROUPS, SSD_HPG).transpose(1, 0, 2).reshape(SSD_GROUPS, 2 * SSD_HPG)
    return jnp.pad(g, ((0, 0), (0, width - 2 * SSD_HPG))).reshape(SSD_GROUPS, 1, width)


def _block_diag(w, tile):
    two, nb, bd, _ = w.shape
    per = tile // bd
    w = w.reshape(two, nb // per, per, bd, bd)
    eye = jnp.eye(per, dtype=w.dtype)
    return jnp.einsum("dtpij,pq->dtpiqj", w, eye).reshape(two, nb // per, tile, tile)


def kernel(x, c, ctx, c_ctx, ada_w, ada_b, norm1_g, norm2_g, ev_w_in, ev_conv_w, ev_conv_b, ev_a_log,
           ev_dt_bias, ev_d, ev_ssd_norm_g, ev_lru_conv_w, ev_lru_conv_b, ev_lru_wa, ev_lru_ba, ev_lru_wx,
           ev_lru_bx, ev_lru_lam, ev_w_out, od_w_qkv, od_q_norm_g, od_k_norm_g, od_rpb, od_w_o,
           pe_w_q, pe_keys, pe_u, pe_v):
    bsz, sx, d = x.shape
    sc = ctx.shape[1]
    depth = ada_w.shape[0]

    n_c = bsz + 1
    rows = -(-n_c // SUBLANES) * SUBLANES
    c_all = jnp.concatenate([c, c_ctx[None], jnp.zeros((rows - n_c, d), F32)], axis=0)
    mods = _ada_mods(c_all, ada_w, ada_b).reshape(depth, rows, 6, d)

    ctx = ctx.reshape(1, bsz * sc, d)

    def per_batch(t):
        return t.reshape(bsz, sc, t.shape[-1])

    for layer in range(depth):
        last = layer == depth - 1
        j = layer // 2
        mod_x = mods[layer, :bsz]
        mod_c = mods[layer, bsz:bsz + 1]
        g1 = norm1_g[layer][None]
        g2 = norm2_g[layer][None]
        want_ctx = not last

        if layer % 2 == 0:
            w_in = ev_w_in[j]
            o_dt, o_xl = SSD_XBC, SSD_XBC + 2 * SSD_HEADS
            o_z = o_xl + LRU_WIDTH
            o_gate = o_z + SSD_INNER
            w_dt = w_in[:, o_dt:o_xl].reshape(d, 2, SSD_GROUPS, SSD_HPG).transpose(0, 2, 1, 3)
            w_dt = jnp.pad(w_dt.reshape(d, SSD_GROUPS, 2 * SSD_HPG), ((0, 0), (0, 0), (0, LANES - 2 * SSD_HPG)))
            ws = [w_in[:, :o_dt].astype(BF16), w_dt.reshape(d, SSD_GROUPS * LANES).astype(BF16),
                  w_in[:, o_xl:o_z].astype(BF16), w_in[:, o_z:o_gate].astype(BF16), w_in[:, o_gate:].astype(BF16)]
            dts = [BF16, F32, BF16, BF16, BF16]
            px = _nm_linear(x, g1, mod_x[:, 0:2], ws, dts)
            pc = [per_batch(t) for t in _nm_linear(ctx, g1, mod_c[:, 0:2], ws, dts)]
            y_ssd_x, y_ssd_c = _ssd_mixer(
                px[0], pc[0], px[1], pc[1], px[3], pc[3], ev_conv_w[j], ev_conv_b[j][None],
                _group_lanes(ev_a_log[j], LANES), _group_lanes(ev_dt_bias[j], LANES),
                jnp.repeat(ev_d[j], SSD_HEAD_DIM).reshape(SSD_GROUPS, 1, SSD_GW),
                ev_ssd_norm_g[j].reshape(SSD_GROUPS, 1, SSD_GW))
            y_lru_x, y_lru_c = _lru_mixer(
                px[2], pc[2], px[4], pc[4], ev_lru_conv_w[j], ev_lru_conv_b[j][None],
                _block_diag(ev_lru_wa[j], LRU_TILE).astype(BF16), _block_diag(ev_lru_wx[j], LRU_TILE).astype(BF16),
                ev_lru_ba[j], ev_lru_bx[j], ev_lru_lam[j])
            w_out = ev_w_out[j].astype(BF16)
            w_outs = [w_out[:SSD_INNER], w_out[SSD_INNER:]]
            acts_x = [y_ssd_x, y_lru_x]
            acts_c = [y_ssd_c.reshape(1, bsz * sc, -1), y_lru_c.reshape(1, bsz * sc, -1)]
        else:
            w_qkv = od_w_qkv[j].astype(BF16)
            nd = w_qkv.shape[1] // 3
            ws = [w_qkv[:, :nd], w_qkv[:, nd:2 * nd], w_qkv[:, 2 * nd:]]
            gains = [jnp.tile(od_q_norm_g[j], NA_HEADS)[None], jnp.tile(od_k_norm_g[j], NA_HEADS)[None], None]
            q_x, k_x, v_x = _nm_linear(x, g1, mod_x[:, 0:2], ws, [BF16] * 3, gains)
            q_c, k_c, v_c = [per_batch(t) for t in _nm_linear(ctx, g1, mod_c[:, 0:2], ws, [BF16] * 3, gains)]
            table = _rpb_table(od_rpb[j])
            o_x, o_c = _na_attention(q_x, k_x, v_x, q_c, k_c, v_c, table, want_ctx)
            w_outs = [od_w_o[j].astype(BF16)]
            acts_x = [o_x]
            acts_c = [o_c.reshape(1, bsz * sc, -1)] if want_ctx else None

        w_q = pe_w_q[layer].reshape(d, PEER_HEADS, 2 * PEER_KEYS).transpose(1, 0, 2).astype(BF16)
        keys = pe_keys[layer].reshape(2 * PEER_HEADS, PEER_KEYS, -1).astype(BF16)
        u = _pack_weight(pe_u, layer, transpose=False)
        v_t = _pack_weight(pe_v, layer, transpose=True)

        x, hx = _out_linear(acts_x, w_outs, x, mod_x[:, 2:5], g2)
        x = _peer(hx, x, mod_x[:, 5:6], w_q, keys, u, v_t)
        if want_ctx:
            ctx, hc = _out_linear(acts_c, w_outs, ctx, mod_c[:, 2:5], g2)
            ctx = _peer(hc, ctx, mod_c[:, 5:6], w_q, keys, u, v_t)
    return x
```
